```python
import jax, jax.numpy as jnp
from jax import lax
import numpy as np

D_MODEL = 2048
BATCH = 2
SEQ = 16384
DEPTH = 2

N_EVEN = (DEPTH + 1) // 2
N_ODD = DEPTH // 2
PLE_DIM = 256

HG_HEADS = 8
HG_DK = 128
HG_DV = 128
HG_W = HG_HEADS * HG_DK
HG_CHUNK = 64

MLA_HEADS = 8
MLA_Q_RANK = 512
MLA_KV_RANK = 512
MLA_NOPE = 128
MLA_ROPE = 64
MLA_V = 128
MLA_QK = MLA_NOPE + MLA_ROPE
Q_BLOCK = 128

EVEN_IN = 4 * HG_W + MLA_Q_RANK + MLA_KV_RANK + MLA_ROPE
EVEN_MIX = HG_HEADS * HG_DV + MLA_HEADS * MLA_V

RET_HEADS = 8
RET_DK = 256
RET_DV = 512
RET_CHUNK = 128
ODD_IN = RET_HEADS * (2 * RET_DK + 2 * RET_DV)
ODD_MIX = RET_HEADS * RET_DV

D_FF = 5632
CONV_W = 3

ROPE_BASE = 10000.0
EPS = 1e-6

kernel_name = "hybrid_hgrn2_mla_retnet_convffn"


def rmsnorm(x, g=None):
    xf = x.astype(jnp.float32)
    y = xf * lax.rsqrt(jnp.mean(xf * xf, axis=-1, keepdims=True) + EPS)
    if g is not None:
        y = y * g.astype(jnp.float32)
    return y.astype(x.dtype)


def rope(x, positions):
    half = x.shape[-1] // 2
    inv_freq = ROPE_BASE ** (-jnp.arange(half, dtype=jnp.float32) / half)
    ang = positions.astype(jnp.float32)[:, :, None, None] * inv_freq
    cos, sin = jnp.cos(ang), jnp.sin(ang)
    x1 = x[..., :half].astype(jnp.float32)
    x2 = x[..., half:].astype(jnp.float32)
    return jnp.concatenate([x1 * cos - x2 * sin, x2 * cos + x1 * sin], axis=-1).astype(x.dtype)


def to_chunks(t, c):
    b, s, h, d = t.shape
    return t.reshape(b, s // c, c, h, d).transpose(1, 0, 3, 2, 4)


def from_chunks(t):
    n, b, h, c, d = t.shape
    return t.transpose(1, 0, 3, 2, 4).reshape(b, n * c, h, d)


def hgrn2(q, f_logit, v, lb):
    dt = v.dtype
    c = HG_CHUNK
    lb = lb.reshape(HG_HEADS, HG_DK).astype(jnp.float32)
    f = lb + (1.0 - lb) * jax.nn.sigmoid(f_logit.astype(jnp.float32))
    k = 1.0 - f
    qc, kc, vc, lfc = (to_chunks(t.astype(jnp.float32), c) for t in (q, k, v, jnp.log(f)))
    bc = jnp.cumsum(lfc, axis=3)
    causal = jnp.tril(jnp.ones((c, c), dtype=bool))

    def step(state, xs):
        qj, kj, vj, bj = xs
        ref = bj[:, :, c // 2 - 1:c // 2, :]
        q_rel = qj * jnp.exp(bj - ref)
        k_rel = kj * jnp.exp(ref - bj)
        a = jnp.where(causal, jnp.einsum('bhtk,bhsk->bhts', q_rel, k_rel), 0.0)
        out = (jnp.einsum('bhts,bhsv->bhtv', a, vj)
               + jnp.einsum('bhtk,bhkv->bhtv', qj * jnp.exp(bj), state))
        b_last = bj[:, :, -1:, :]
        state = (jnp.exp(b_last[:, :, 0, :, None]) * state
                 + jnp.einsum('bhsk,bhsv->bhkv', kj * jnp.exp(b_last - bj), vj))
        return state, out

    s0 = jnp.zeros((q.shape[0], HG_HEADS, HG_DK, HG_DV), jnp.float32)
    _, out = lax.scan(step, s0, (qc, kc, vc, bc))
    return from_chunks(out).astype(dt)


def causal_attention(q, k, v):
    b, s, h, dqk = q.shape
    nb = s // Q_BLOCK
    scale = dqk ** -0.5
    qb = q.reshape(b, nb, Q_BLOCK, h, dqk).transpose(1, 0, 2, 3, 4)
    kpos = jnp.arange(s)

    def block(args):
        qi, bi = args
        sc = jnp.einsum('bqhd,bkhd->bhqk', qi, k, preferred_element_type=jnp.float32) * scale
        qpos = bi * Q_BLOCK + jnp.arange(Q_BLOCK)
        sc = jnp.where(kpos[None, :] <= qpos[:, None], sc, -jnp.inf)
        w = jax.nn.softmax(sc, axis=-1)
        return jnp.einsum('bhqk,bkhd->bqhd', w.astype(v.dtype), v,
                          preferred_element_type=jnp.float32).astype(v.dtype)

    out = lax.map(block, (qb, jnp.arange(nb)))
    return out.transpose(1, 0, 2, 3, 4).reshape(b, s, h, v.shape[-1])


def retention(q, k, v):
    dt = v.dtype
    c = RET_CHUNK
    log_g = jnp.log(1.0 - 2.0 ** (-5.0 - jnp.arange(RET_HEADS, dtype=jnp.float32)))
    idx = jnp.arange(c, dtype=jnp.float32)
    diff = idx[:, None] - idx[None, :]
    causal = diff >= 0
    d_intra = jnp.where(causal, jnp.exp(jnp.where(causal, diff, 0.0) * log_g[:, None, None]), 0.0)
    q_dec = jnp.exp((idx + 1.0) * log_g[:, None])[..., None]
    k_dec = jnp.exp((c - 1.0 - idx) * log_g[:, None])[..., None]
    c_dec = jnp.exp(c * log_g)[:, None, None]
    qc, kc, vc = (to_chunks(t.astype(jnp.float32), c) for t in (q, k * RET_DK ** -0.5, v))

    def step(r, xs):
        qj, kj, vj = xs
        a = jnp.einsum('bhtk,bhsk->bhts', qj, kj) * d_intra
        out = (jnp.einsum('bhts,bhsv->bhtv', a, vj)
               + jnp.einsum('bhtk,bhkv->bhtv', qj * q_dec, r))
        r = c_dec * r + jnp.einsum('bhsk,bhsv->bhkv', kj * k_dec, vj)
        return r, out

    r0 = jnp.zeros((q.shape[0], RET_HEADS, RET_DK, RET_DV), jnp.float32)
    _, out = lax.scan(step, r0, (qc, kc, vc))
    return from_chunks(out).astype(dt)


def even_mixer(u, positions, w_in, lb, q_a_norm, kv_a_norm, w_uq, w_ukv,
               q_norm, k_norm, hg_onorm, w_out):
    b, s, _ = u.shape
    z = u @ w_in
    cuts = [HG_W, 2 * HG_W, 3 * HG_W, 4 * HG_W, 4 * HG_W + MLA_Q_RANK,
            4 * HG_W + MLA_Q_RANK + MLA_KV_RANK]
    hq, hf, hi, hg, cq, ckv, kpe = jnp.split(z, cuts, axis=-1)
    o_a = hgrn2(hq.reshape(b, s, HG_HEADS, HG_DK), hf.reshape(b, s, HG_HEADS, HG_DK),
                hi.reshape(b, s, HG_HEADS, HG_DV), lb)
    o_a = rmsnorm(o_a, hg_onorm) * jax.nn.silu(hg.reshape(b, s, HG_HEADS, HG_DV))
    q = (rmsnorm(cq, q_a_norm) @ w_uq).reshape(b, s, MLA_HEADS, MLA_QK)
    kv = (rmsnorm(ckv, kv_a_norm) @ w_ukv).reshape(b, s, MLA_HEADS, MLA_NOPE + MLA_V)
    k_nope, v = kv[..., :MLA_NOPE], kv[..., MLA_NOPE:]
    k = jnp.concatenate([k_nope, jnp.broadcast_to(kpe[:, :, None, :], (b, s, MLA_HEADS, MLA_ROPE))], axis=-1)
    q = rmsnorm(q, q_norm)
    k = rmsnorm(k, k_norm)
    q = jnp.concatenate([q[..., :MLA_NOPE], rope(q[..., MLA_NOPE:], positions)], axis=-1)
    k = jnp.concatenate([k[..., :MLA_NOPE], rope(k[..., MLA_NOPE:], positions)], axis=-1)
    o_b = causal_attention(q, k, v)
    mix = jnp.concatenate([o_a.reshape(b, s, -1), o_b.reshape(b, s, -1)], axis=-1)
    return mix @ w_out


def odd_mixer(u, positions, w_in, w_out):
    b, s, _ = u.shape
    z = u @ w_in
    qw = RET_HEADS * RET_DK
    q, k, v, g = jnp.split(z, [qw, 2 * qw, 2 * qw + ODD_MIX], axis=-1)
    q = rope(q.reshape(b, s, RET_HEADS, RET_DK), positions)
    k = rope(k.reshape(b, s, RET_HEADS, RET_DK), positions)
    o = retention(q, k, v.reshape(b, s, RET_HEADS, RET_DV))
    o = rmsnorm(o) * jax.nn.silu(g.reshape(b, s, RET_HEADS, RET_DV))
    return o.reshape(b, s, ODD_MIX) @ w_out


def conv_ffn(u, w_gate, w_up, conv_w, conv_b, w_down):
    s = u.shape[1]
    a = u @ w_gate
    a_pad = jnp.pad(a, ((0, 0), (CONV_W - 1, 0), (0, 0)))
    c = conv_b
    for t in range(CONV_W):
        c = c + conv_w[t] * a_pad[:, t:t + s]
    return (jax.nn.silu(c) * (u @ w_up)) @ w_down


def setup_inputs(seed: int = 0) -> dict:
    key = jax.random.key(seed)
    ks = iter(jax.random.split(key, 32))

    def nrm(shape, scale):
        return jax.random.normal(next(ks), shape, jnp.float32) * scale

    def gain(shape):
        return 1.0 + 0.05 * jax.random.normal(next(ks), shape, jnp.float32)

    x = nrm((BATCH, SEQ, D_MODEL), 1.0)
    p = nrm((DEPTH, BATCH, SEQ, PLE_DIM), 1.0)
    positions = (jax.random.randint(next(ks), (BATCH, 1), 0, 1024, dtype=jnp.int32)
                 + jnp.arange(SEQ, dtype=jnp.int32)[None, :])
    return {
        "x": x,
        "p": p,
        "positions": positions,
        "norm_mix": gain((DEPTH, D_MODEL)),
        "norm_ffn": gain((DEPTH, D_MODEL)),
        "norm_ple": gain((DEPTH, D_MODEL)),
        "e_w_in": nrm((N_EVEN, D_MODEL, EVEN_IN), D_MODEL ** -0.5),
        "e_lb_logits": nrm((N_EVEN + 1, HG_W), 0.1),
        "e_q_a_norm": gain((N_EVEN, MLA_Q_RANK)),
        "e_kv_a_norm": gain((N_EVEN, MLA_KV_RANK)),
        "e_w_uq": nrm((N_EVEN, MLA_Q_RANK, MLA_HEADS * MLA_QK), MLA_Q_RANK ** -0.5),
        "e_w_ukv": nrm((N_EVEN, MLA_KV_RANK, MLA_HEADS * (MLA_NOPE + MLA_V)), MLA_KV_RANK ** -0.5),
        "e_q_norm": gain((N_EVEN, MLA_QK)),
        "e_k_norm": gain((N_EVEN, MLA_QK)),
        "e_hg_onorm": gain((N_EVEN, HG_DV)),
        "e_w_out": nrm((N_EVEN, EVEN_MIX, D_MODEL), EVEN_MIX ** -0.5),
        "o_w_in": nrm((N_ODD, D_MODEL, ODD_IN), D_MODEL ** -0.5),
        "o_w_out": nrm((N_ODD, ODD_MIX, D_MODEL), ODD_MIX ** -0.5),
        "ffn_w_gate": nrm((DEPTH, D_MODEL, D_FF), D_MODEL ** -0.5),
        "ffn_w_up": nrm((DEPTH, D_MODEL, D_FF), D_MODEL ** -0.5),
        "ffn_conv_w": nrm((DEPTH, CONV_W, D_FF), CONV_W ** -0.5),
        "ffn_conv_b": nrm((DEPTH, D_FF), 0.02),
        "ffn_w_down": nrm((DEPTH, D_FF, D_MODEL), D_FF ** -0.5),
        "ple_w_proj": nrm((DEPTH, PLE_DIM, D_MODEL), PLE_DIM ** -0.5),
        "ple_w_gate": nrm((DEPTH, D_MODEL, D_MODEL), D_MODEL ** -0.5),
    }


def reference(x, p, positions, norm_mix, norm_ffn, norm_ple,
              e_w_in, e_lb_logits, e_q_a_norm, e_kv_a_norm, e_w_uq, e_w_ukv,
              e_q_norm, e_k_norm, e_hg_onorm, e_w_out,
              o_w_in, o_w_out,
              ffn_w_gate, ffn_w_up, ffn_conv_w, ffn_conv_b, ffn_w_down,
              ple_w_proj, ple_w_gate):
    h = x
    lb_all = jnp.cumsum(jax.nn.softmax(e_lb_logits.astype(jnp.float32), axis=0), axis=0)
    for i in range(DEPTH):
        j = i // 2
        u = rmsnorm(h, norm_mix[i])
        if i % 2 == 0:
            mix = even_mixer(u, positions, e_w_in[j], lb_all[j], e_q_a_norm[j], e_kv_a_norm[j],
                             e_w_uq[j], e_w_ukv[j], e_q_norm[j], e_k_norm[j], e_hg_onorm[j], e_w_out[j])
        else:
            mix = odd_mixer(u, positions, o_w_in[j], o_w_out[j])
        h = h + mix
        h = h + conv_ffn(rmsnorm(h, norm_ffn[i]), ffn_w_gate[i], ffn_w_up[i],
                         ffn_conv_w[i], ffn_conv_b[i], ffn_w_down[i])
        gate = jax.nn.sigmoid((rmsnorm(h, norm_ple[i]) @ ple_w_gate[i]).astype(jnp.float32))
        h = h + (p[i] @ ple_w_proj[i]) * gate.astype(h.dtype)
    return h
```

```python
import functools

import jax
import jax.numpy as jnp
from jax import lax
from jax.experimental import pallas as pl
from jax.experimental.pallas import tpu as pltpu

F32 = jnp.float32
BF16 = jnp.bfloat16

D_MODEL = 2048
PLE_DIM = 256
HG_HEADS = 8
HG_DK = 128
HG_DV = 128
HG_W = HG_HEADS * HG_DK
HG_CHUNK = 64
MLA_HEADS = 8
MLA_Q_RANK = 512
MLA_KV_RANK = 512
MLA_NOPE = 128
MLA_ROPE = 64
MLA_V = 128
MLA_QK = MLA_NOPE + MLA_ROPE
MLA_PAD = 256
RET_HEADS = 8
RET_DK = 256
RET_DV = 512
RET_CHUNK = 256
D_FF = 5632
ROPE_BASE = 10000.0
EPS = 1e-6

VMEM_LIMIT = 56 * 2**20

NT = (((1,), (1,)), ((), ()))
TN = (((0,), (0,)), ((), ()))


def _params(*sem):
    return pltpu.CompilerParams(dimension_semantics=sem, vmem_limit_bytes=VMEM_LIMIT)


def _tile(n, pref):
    t = min(n, pref)
    assert n % t == 0, (n, pref)
    return t


def _dot(a, b):
    return jnp.dot(a, b, preferred_element_type=F32)


def _rms(x, g=None, n=None):
    n = x.shape[-1] if n is None else n
    y = x * lax.rsqrt(jnp.sum(x * x, axis=-1, keepdims=True) * (1.0 / n) + EPS)
    return y if g is None else y * g


def _norm_into(x_ref, g_ref, dst_ref, rows):
    g = g_ref[...]

    def body(c, carry):
        r0 = pl.multiple_of(c * rows, rows)
        dst_ref[pl.ds(r0, rows), :] = _rms(x_ref[pl.ds(r0, rows), :], g).astype(dst_ref.dtype)
        return carry

    lax.fori_loop(0, x_ref.shape[0] // rows, body, 0)


def _rope_tab_kernel(pos_ref, f_ref, sgn_ref, c_ref, s_ref):
    ang = pos_ref[...].astype(F32) * f_ref[...]
    c_ref[...] = jnp.cos(ang)
    s_ref[...] = jnp.sin(ang) * sgn_ref[...]


def rope_tables(pos_col, freq, sgn, name):
    t = pos_col.shape[0]
    tm = _tile(t, 1024)
    return pl.pallas_call(
        _rope_tab_kernel,
        grid=(t // tm,),
        in_specs=[pl.BlockSpec((tm, 1), lambda i: (i, 0)),
                  pl.BlockSpec((1, 128), lambda i: (0, 0)),
                  pl.BlockSpec((1, 128), lambda i: (0, 0))],
        out_specs=[pl.BlockSpec((tm, 128), lambda i: (i, 0)),
                   pl.BlockSpec((tm, 128), lambda i: (i, 0))],
        out_shape=[jax.ShapeDtypeStruct((t, 128), F32)] * 2,
        compiler_params=_params("parallel"),
        name=name,
    )(pos_col, freq, sgn)


def _norm_matmul_kernel(x_ref, g_ref, w_ref, o_ref, xn_ref):
    @pl.when(pl.program_id(1) == 0)
    def _():
        _norm_into(x_ref, g_ref, xn_ref, 256)

    o_ref[...] = _dot(xn_ref[...], w_ref[...]).astype(o_ref.dtype)


def norm_matmul(x, g, w, out_dtype, tm, tn, name):
    t, k = x.shape
    n = w.shape[1]
    tm, tn = _tile(t, tm), _tile(n, tn)
    return pl.pallas_call(
        _norm_matmul_kernel,
        grid=(t // tm, n // tn),
        in_specs=[pl.BlockSpec((tm, k), lambda i, j: (i, 0)),
                  pl.BlockSpec((1, k), lambda i, j: (0, 0)),
                  pl.BlockSpec((k, tn), lambda i, j: (0, j))],
        out_specs=pl.BlockSpec((tm, tn), lambda i, j: (i, j)),
        out_shape=jax.ShapeDtypeStruct((t, n), out_dtype),
        scratch_shapes=[pltpu.VMEM((tm, k), BF16)],
        compiler_params=_params("parallel", "arbitrary"),
        name=name,
    )(x, g, w)


def _ret_in_kernel(x_ref, g_ref, w_ref, cos_ref, sin_ref, o_ref, xn_ref, *, n_q_tiles, n_rope_tiles):
    j = pl.program_id(1)

    @pl.when(j == 0)
    def _():
        _norm_into(x_ref, g_ref, xn_ref, 256)

    acc = _dot(xn_ref[...], w_ref[...])

    @pl.when(j < n_rope_tiles)
    def _():
        scale = jnp.where(j >= n_q_tiles, RET_DK ** -0.5, 1.0).astype(F32)
        cos = cos_ref[...] * scale
        sin = sin_ref[...] * scale
        half = RET_DK // 2
        for hh in range(acc.shape[1] // RET_DK):
            x1 = acc[:, hh * RET_DK:hh * RET_DK + half]
            x2 = acc[:, hh * RET_DK + half:(hh + 1) * RET_DK]
            o_ref[:, hh * RET_DK:hh * RET_DK + half] = (x1 * cos - x2 * sin).astype(o_ref.dtype)
            o_ref[:, hh * RET_DK + half:(hh + 1) * RET_DK] = (x2 * cos + x1 * sin).astype(o_ref.dtype)

    @pl.when(j >= n_rope_tiles)
    def _():
        o_ref[...] = acc.astype(o_ref.dtype)


def ret_in_proj(x, g, w, cos, sin, tm, tn):
    t, k = x.shape
    n = w.shape[1]
    tm, tn = _tile(t, tm), _tile(n, tn)
    qw = RET_HEADS * RET_DK
    kern = functools.partial(_ret_in_kernel, n_q_tiles=qw // tn, n_rope_tiles=2 * qw // tn)
    return pl.pallas_call(
        kern,
        grid=(t // tm, n // tn),
        in_specs=[pl.BlockSpec((tm, k), lambda i, j: (i, 0)),
                  pl.BlockSpec((1, k), lambda i, j: (0, 0)),
                  pl.BlockSpec((k, tn), lambda i, j: (0, j)),
                  pl.BlockSpec((tm, 128), lambda i, j: (i, 0)),
                  pl.BlockSpec((tm, 128), lambda i, j: (i, 0))],
        out_specs=pl.BlockSpec((tm, tn), lambda i, j: (i, j)),
        out_shape=jax.ShapeDtypeStruct((t, n), BF16),
        scratch_shapes=[pltpu.VMEM((tm, k), BF16)],
        compiler_params=_params("parallel", "arbitrary"),
        name="ret_in_proj",
    )(x, g, w, cos, sin)


def _res_matmul_kernel(*refs, n_lhs):
    h_ref, o_ref = refs[0], refs[-1]
    acc = h_ref[...]
    for a_ref, w_ref in zip(refs[1:1 + n_lhs], refs[1 + n_lhs:1 + 2 * n_lhs]):
        acc = acc + _dot(a_ref[...], w_ref[...])
    o_ref[...] = acc


def res_matmul(h, lhs, ws, tm, tn, name):
    t, n = h.shape
    tm, tn = _tile(t, tm), _tile(n, tn)
    in_specs = [pl.BlockSpec((tm, tn), lambda i, j: (i, j))]
    in_specs += [pl.BlockSpec((tm, a.shape[1]), lambda i, j: (i, 0)) for a in lhs]
    in_specs += [pl.BlockSpec((w.shape[0], tn), lambda i, j: (0, j)) for w in ws]
    return pl.pallas_call(
        functools.partial(_res_matmul_kernel, n_lhs=len(lhs)),
        grid=(t // tm, n // tn),
        in_specs=in_specs,
        out_specs=pl.BlockSpec((tm, tn), lambda i, j: (i, j)),
        out_shape=jax.ShapeDtypeStruct((t, n), F32),
        compiler_params=_params("parallel", "arbitrary"),
        name=name,
    )(h, *lhs, *ws)


def _ffn_kernel(h_ref, g_ref, wg_ref, wu_ref, cw_ref, cb_ref, wd_ref, o_ref,
                u_ref, gs_ref, tail_ref, *, blocks_per_seq):
    i, j = pl.program_id(0), pl.program_id(1)
    tm = h_ref.shape[0]

    @pl.when(j == 0)
    def _():
        _norm_into(h_ref, g_ref, u_ref, 256)

    u = u_ref[...]
    a = _dot(u, wg_ref[...])
    up = _dot(u, wu_ref[...])

    seq_start = (i % blocks_per_seq) == 0

    @pl.when(seq_start)
    def _():
        gs_ref[0:8, :] = jnp.zeros((8, a.shape[1]), F32)

    @pl.when(jnp.logical_not(seq_start))
    def _():
        gs_ref[0:8, :] = tail_ref[j]

    gs_ref[8:8 + tm, :] = a
    tail_ref[j] = a[tm - 8:tm, :]
    cw = cw_ref[...]
    c = (cb_ref[...] + cw[0:1, :] * gs_ref[6:6 + tm, :] + cw[1:2, :] * gs_ref[7:7 + tm, :]
         + cw[2:3, :] * a)
    act = (c * jax.nn.sigmoid(c) * up).astype(BF16)
    contrib = _dot(act, wd_ref[...])

    @pl.when(j == 0)
    def _():
        o_ref[...] = h_ref[...] + contrib

    @pl.when(j > 0)
    def _():
        o_ref[...] += contrib


def conv_ffn(h, g, wg, wu, cw, cb, wd, seq, tm, tf):
    t, d = h.shape
    f = wg.shape[1]
    tm, tf = _tile(seq, tm), _tile(f, tf)
    kern = functools.partial(_ffn_kernel, blocks_per_seq=seq // tm)
    return pl.pallas_call(
        kern,
        grid=(t // tm, f // tf),
        in_specs=[pl.BlockSpec((tm, d), lambda i, j: (i, 0)),
                  pl.BlockSpec((1, d), lambda i, j: (0, 0)),
                  pl.BlockSpec((d, tf), lambda i, j: (0, j)),
                  pl.BlockSpec((d, tf), lambda i, j: (0, j)),
                  pl.BlockSpec((3, tf), lambda i, j: (0, j)),
                  pl.BlockSpec((1, tf), lambda i, j: (0, j)),
                  pl.BlockSpec((tf, d), lambda i, j: (j, 0))],
        out_specs=pl.BlockSpec((tm, d), lambda i, j: (i, 0)),
        out_shape=jax.ShapeDtypeStruct((t, d), F32),
        scratch_shapes=[pltpu.VMEM((tm, d), BF16),
                        pltpu.VMEM((tm + 8, tf), F32),
                        pltpu.VMEM((f // tf, 8, tf), F32)],
        compiler_params=_params("arbitrary", "arbitrary"),
        name="conv_ffn",
    )(h, g, wg, wu, cw, cb, wd)


def _ple_kernel(h_ref, g_ref, wgate_ref, p_ref, wproj_ref, o_ref, xn_ref, pb_ref):
    j = pl.program_id(1)
    tn = o_ref.shape[1]

    @pl.when(j == 0)
    def _():
        _norm_into(h_ref, g_ref, xn_ref, 256)
        pb_ref[...] = p_ref[...].astype(BF16)

    gate = jax.nn.sigmoid(_dot(xn_ref[...], wgate_ref[...]))
    proj = _dot(pb_ref[...], wproj_ref[...])
    c0 = pl.multiple_of(j * tn, tn)
    o_ref[...] = h_ref[:, pl.ds(c0, tn)] + proj * gate


def ple(h, g, wgate, p, wproj, tm, tn):
    t, d = h.shape
    tm, tn = _tile(t, tm), _tile(d, tn)
    return pl.pallas_call(
        _ple_kernel,
        grid=(t // tm, d // tn),
        in_specs=[pl.BlockSpec((tm, d), lambda i, j: (i, 0)),
                  pl.BlockSpec((1, d), lambda i, j: (0, 0)),
                  pl.BlockSpec((d, tn), lambda i, j: (0, j)),
                  pl.BlockSpec((tm, PLE_DIM), lambda i, j: (i, 0)),
                  pl.BlockSpec((PLE_DIM, tn), lambda i, j: (0, j))],
        out_specs=pl.BlockSpec((tm, tn), lambda i, j: (i, j)),
        out_shape=jax.ShapeDtypeStruct((t, d), F32),
        scratch_shapes=[pltpu.VMEM((tm, d), BF16), pltpu.VMEM((tm, PLE_DIM), BF16)],
        compiler_params=_params("parallel", "arbitrary"),
        name="ple",
    )(h, g, wgate, p, wproj)


def _hgrn_kernel(q_ref, f_ref, i_ref, g_ref, lbl_ref, on_ref, o_ref, st_ref):
    c = HG_CHUNK

    @pl.when(pl.program_id(1) == 0)
    def _():
        st_ref[...] = jnp.zeros(st_ref.shape, F32)

    lg = lbl_ref[...]
    e = jnp.exp(lg - jnp.max(lg, axis=0, keepdims=True))
    lb = e[0:1, :] / jnp.sum(e, axis=0, keepdims=True)
    onorm = on_ref[...]
    row = lax.broadcasted_iota(jnp.int32, (c, c), 0)
    col = lax.broadcasted_iota(jnp.int32, (c, c), 1)
    causal = row >= col
    tril = jnp.where(causal, 1.0, 0.0).astype(BF16)

    def chunk(ci, carry):
        r0 = pl.multiple_of(ci * c, c)
        rows = pl.ds(r0, c)
        for h in range(HG_HEADS):
            cols = slice(h * HG_DK, (h + 1) * HG_DK)
            q = q_ref[rows, cols]
            v = i_ref[rows, cols].astype(BF16)
            lbh = lb[:, cols]
            f = lbh + (1.0 - lbh) * jax.nn.sigmoid(f_ref[rows, cols])
            k = 1.0 - f
            lf = jnp.log(f)
            hi = lf.astype(BF16)
            r1 = lf - hi.astype(F32)
            mid = r1.astype(BF16)
            lo = (r1 - mid.astype(F32)).astype(BF16)
            bc = _dot(tril, hi) + _dot(tril, mid) + _dot(tril, lo)
            ref = bc[c // 2 - 1:c // 2, :]
            b_last = bc[c - 1:c, :]
            q_rel = (q * jnp.exp(bc - ref)).astype(BF16)
            k_rel = (k * jnp.exp(ref - bc)).astype(BF16)
            a = lax.dot_general(q_rel, k_rel, NT, preferred_element_type=F32)
            a = jnp.where(causal, a, 0.0).astype(BF16)
            st = st_ref[h]
            out = _dot(a, v) + lax.dot_general((q * jnp.exp(bc)).astype(BF16), st.astype(BF16), NT,
                                               preferred_element_type=F32)
            k_dec = (k * jnp.exp(b_last - bc)).astype(BF16)
            st_ref[h] = st * jnp.exp(b_last) + lax.dot_general(v, k_dec, TN, preferred_element_type=F32)
            g = g_ref[rows, cols]
            o_ref[rows, cols] = (_rms(out, onorm) * (g * jax.nn.sigmoid(g))).astype(o_ref.dtype)
        return carry

    lax.fori_loop(0, q_ref.shape[0] // c, chunk, 0)


def hgrn(z, lb_logits, onorm, batch, seq, cb):
    t = z.shape[0]
    cb = _tile(seq, cb)
    nblk = seq // cb
    spec = lambda part: pl.BlockSpec((cb, HG_W), lambda b, s, part=part: (b * nblk + s, part))
    return pl.pallas_call(
        _hgrn_kernel,
        grid=(batch, nblk),
        in_specs=[spec(0), spec(1), spec(2), spec(3),
                  pl.BlockSpec(lb_logits.shape, lambda b, s: (0, 0)),
                  pl.BlockSpec((1, HG_DV), lambda b, s: (0, 0))],
        out_specs=pl.BlockSpec((cb, HG_W), lambda b, s: (b * nblk + s, 0)),
        out_shape=jax.ShapeDtypeStruct((t, HG_W), BF16),
        scratch_shapes=[pltpu.VMEM((HG_HEADS, HG_DV, HG_DK), F32)],
        compiler_params=_params("parallel", "arbitrary"),
        name="hgrn2",
    )(z, z, z, z, lb_logits, onorm)


def _mla_prep_kernel(z_ref, qa_ref, kva_ref, wuq_ref, wukv_ref, qn_ref, kn_ref, cos_ref, sin_ref,
                     q_out, k_out, v_out):
    cq = _rms(z_ref[:, 0:MLA_Q_RANK], qa_ref[...]).astype(BF16)
    ckv = _rms(z_ref[:, MLA_Q_RANK:MLA_Q_RANK + MLA_KV_RANK], kva_ref[...]).astype(BF16)
    kpe = z_ref[:, MLA_Q_RANK + MLA_KV_RANK:]
    qf = _dot(cq, wuq_ref[...])
    kvf = _dot(ckv, wukv_ref[...])
    cos, sin = cos_ref[...], sin_ref[...]
    qn, kn = qn_ref[...], kn_ref[...]
    scale = MLA_QK ** -0.5

    def rope(x):
        return x * cos + pltpu.roll(x, 64, 1) * sin

    kpe_ss = jnp.sum(kpe * kpe, axis=-1, keepdims=True)
    k_rope = rope(kpe * kn[:, MLA_NOPE:])
    for h in range(MLA_HEADS):
        c0 = h * MLA_PAD
        qh = qf[:, c0:c0 + MLA_PAD]
        rq = lax.rsqrt(jnp.sum(qh * qh, axis=-1, keepdims=True) * (1.0 / MLA_QK) + EPS) * scale
        qh = qh * rq * qn
        q_out[:, c0:c0 + MLA_NOPE] = qh[:, :MLA_NOPE].astype(BF16)
        q_out[:, c0 + MLA_NOPE:c0 + MLA_PAD] = rope(qh[:, MLA_NOPE:]).astype(BF16)
        kh = kvf[:, c0:c0 + MLA_NOPE]
        rk = lax.rsqrt((jnp.sum(kh * kh, axis=-1, keepdims=True) + kpe_ss) * (1.0 / MLA_QK) + EPS)
        k_out[:, c0:c0 + MLA_NOPE] = (kh * rk * kn[:, :MLA_NOPE]).astype(BF16)
        k_out[:, c0 + MLA_NOPE:c0 + MLA_PAD] = (k_rope * rk).astype(BF16)
        v_out[:, h * MLA_V:(h + 1) * MLA_V] = kvf[:, c0 + MLA_NOPE:c0 + MLA_PAD].astype(BF16)


def mla_prep(z, qa, kva, wuq, wukv, qn, kn, cos, sin, tm):
    t, zw = z.shape
    tm = _tile(t, tm)
    full = lambda a: pl.BlockSpec(a.shape, lambda i: (0, 0))
    rows = lambda w: pl.BlockSpec((tm, w), lambda i: (i, 0))
    hw = MLA_HEADS * MLA_PAD
    return pl.pallas_call(
        _mla_prep_kernel,
        grid=(t // tm,),
        in_specs=[rows(zw), full(qa), full(kva), full(wuq), full(wukv), full(qn), full(kn),
                  rows(128), rows(128)],
        out_specs=[rows(hw), rows(hw), rows(MLA_HEADS * MLA_V)],
        out_shape=[jax.ShapeDtypeStruct((t, hw), BF16), jax.ShapeDtypeStruct((t, hw), BF16),
                   jax.ShapeDtypeStruct((t, MLA_HEADS * MLA_V), BF16)],
        compiler_params=_params("parallel"),
        name="mla_prep",
    )(z, qa, kva, wuq, wukv, qn, kn, cos, sin)


def _flash_kernel(q_ref, k_ref, v_ref, o_ref, m_ref, l_ref, acc_ref, *, blk):
    qi = pl.program_id(2)
    q = q_ref[...]
    m_ref[...] = jnp.full(m_ref.shape, -jnp.inf, F32)
    l_ref[...] = jnp.zeros(l_ref.shape, F32)
    acc_ref[...] = jnp.zeros(acc_ref.shape, F32)

    def step(j, masked):
        k0 = pl.multiple_of(j * blk, blk)
        s = lax.dot_general(q, k_ref[pl.ds(k0, blk), :], NT, preferred_element_type=F32)
        if masked:
            row = lax.broadcasted_iota(jnp.int32, s.shape, 0)
            col = lax.broadcasted_iota(jnp.int32, s.shape, 1)
            s = jnp.where(col <= row, s, -jnp.inf)
        m_prev = m_ref[...]
        m_new = jnp.maximum(m_prev, jnp.max(s, axis=-1, keepdims=True))
        alpha = jnp.exp(m_prev - m_new)
        p = jnp.exp(s - m_new)
        l_ref[...] = alpha * l_ref[...] + jnp.sum(p, axis=-1, keepdims=True)
        acc_ref[...] = alpha * acc_ref[...] + _dot(p.astype(BF16), v_ref[pl.ds(k0, blk), :])
        m_ref[...] = m_new

    def body(j, carry):
        step(j, False)
        return carry

    lax.fori_loop(0, qi, body, 0)
    step(qi, True)
    o_ref[...] = (acc_ref[...] / l_ref[...]).astype(o_ref.dtype)


def flash_attention(q, k, v, batch, seq, blk):
    t = q.shape[0]
    blk = _tile(seq, blk)
    nq = seq // blk
    return pl.pallas_call(
        functools.partial(_flash_kernel, blk=blk),
        grid=(batch, MLA_HEADS, nq),
        in_specs=[pl.BlockSpec((blk, MLA_PAD), lambda b, h, i: (b * nq + i, h)),
                  pl.BlockSpec((seq, MLA_PAD), lambda b, h, i: (b, h)),
                  pl.BlockSpec((seq, MLA_V), lambda b, h, i: (b, h))],
        out_specs=pl.BlockSpec((blk, MLA_V), lambda b, h, i: (b * nq + i, h)),
        out_shape=jax.ShapeDtypeStruct((t, MLA_HEADS * MLA_V), BF16),
        scratch_shapes=[pltpu.VMEM((blk, 1), F32), pltpu.VMEM((blk, 1), F32),
                        pltpu.VMEM((blk, MLA_V), F32)],
        compiler_params=_params("parallel", "parallel", "arbitrary"),
        name="mla_flash",
    )(q, k, v)


def _ret_kernel(q_ref, k_ref, v_ref, g_ref, o_ref, r_ref):
    c = RET_CHUNK
    hf = pl.program_id(1).astype(F32)

    @pl.when(pl.program_id(2) == 0)
    def _():
        r_ref[...] = jnp.zeros(r_ref.shape, F32)

    def log_gamma(shape):
        return jnp.log(1.0 - jnp.exp2(-5.0 - jnp.full(shape, hf, F32)))

    row = lax.broadcasted_iota(jnp.int32, (c, c), 0)
    col = lax.broadcasted_iota(jnp.int32, (c, c), 1)
    diff = (row - col).astype(F32)
    d_intra = jnp.where(diff >= 0, jnp.exp(jnp.where(diff >= 0, diff, 0.0) * log_gamma((c, c))), 0.0)
    idx = lax.broadcasted_iota(jnp.int32, (c, RET_DK), 0).astype(F32)
    lg_k = log_gamma((c, RET_DK))
    q_dec = jnp.exp((idx + 1.0) * lg_k)
    k_dec = jnp.exp((c - 1.0 - idx) * lg_k)
    c_dec = jnp.exp(c * log_gamma((1, RET_DV)))

    for ci in range(q_ref.shape[0] // c):
        rows = slice(ci * c, (ci + 1) * c)
        q, k, v = q_ref[rows, :], k_ref[rows, :], v_ref[rows, :]
        a = lax.dot_general(q, k, NT, preferred_element_type=F32) * d_intra
        r = r_ref[...]
        out = _dot(a.astype(BF16), v) + _dot((q.astype(F32) * q_dec).astype(BF16), r.astype(BF16))
        r_ref[...] = r * c_dec + lax.dot_general((k.astype(F32) * k_dec).astype(BF16), v, TN,
                                                 preferred_element_type=F32)
        g = g_ref[rows, :].astype(F32)
        o_ref[rows, :] = (_rms(out) * (g * jax.nn.sigmoid(g))).astype(o_ref.dtype)


def retention(z, batch, seq, cb):
    t = z.shape[0]
    cb = _tile(seq, cb)
    assert cb % RET_CHUNK == 0
    nblk = seq // cb
    nqk = RET_HEADS * RET_DK // RET_DK
    nv = (2 * RET_HEADS * RET_DK) // RET_DV
    return pl.pallas_call(
        _ret_kernel,
        grid=(batch, RET_HEADS, nblk),
        in_specs=[pl.BlockSpec((cb, RET_DK), lambda b, h, s: (b * nblk + s, h)),
                  pl.BlockSpec((cb, RET_DK), lambda b, h, s: (b * nblk + s, nqk + h)),
                  pl.BlockSpec((cb, RET_DV), lambda b, h, s: (b * nblk + s, nv + h)),
                  pl.BlockSpec((cb, RET_DV), lambda b, h, s: (b * nblk + s, nv + RET_HEADS + h))],
        out_specs=pl.BlockSpec((cb, RET_DV), lambda b, h, s: (b * nblk + s, h)),
        out_shape=jax.ShapeDtypeStruct((t, RET_HEADS * RET_DV), BF16),
        scratch_shapes=[pltpu.VMEM((RET_DK, RET_DV), F32)],
        compiler_params=_params("parallel", "parallel", "arbitrary"),
        name="retention",
    )(z, z, z, z)


def _rope_pad(a, axis):
    x1, x2 = jnp.split(a, 2, axis=axis)
    z = jnp.zeros_like(x1)
    return jnp.concatenate([x1, z, x2, z], axis=axis)


def _pad_qk_gain(g):
    return jnp.concatenate([g[:MLA_NOPE], _rope_pad(g[MLA_NOPE:], 0)])[None, :]


def kernel(x, p, positions, norm_mix, norm_ffn, norm_ple, e_w_in, e_lb_logits, e_q_a_norm, e_kv_a_norm, e_w_uq, e_w_ukv, e_q_norm, e_k_norm, e_hg_onorm, e_w_out, o_w_in, o_w_out, ffn_w_gate, ffn_w_up, ffn_conv_w, ffn_conv_b, ffn_w_down, ple_w_proj, ple_w_gate):
    batch, seq, d = x.shape
    t = batch * seq
    h = x.reshape(t, d)
    pos_col = positions.reshape(t, 1)

    ones = jnp.ones((1, 128), F32)
    f_ret = (ROPE_BASE ** (-jnp.arange(RET_DK // 2, dtype=F32) / (RET_DK // 2)))[None, :]
    f_mla = ROPE_BASE ** (-jnp.arange(MLA_ROPE // 2, dtype=F32) / (MLA_ROPE // 2))
    f_mla = _rope_pad(jnp.concatenate([f_mla, f_mla]), 0)[None, :]
    sgn_mla = _rope_pad(jnp.concatenate([-jnp.ones(32, F32), jnp.ones(32, F32)]), 0)[None, :]
    cos_ret, sin_ret = rope_tables(pos_col, f_ret, ones, "rope_tab_ret")
    cos_mla, sin_mla = rope_tables(pos_col, f_mla, sgn_mla, "rope_tab_mla")

    w_in = e_w_in[0]
    w_hg = w_in[:, :4 * HG_W].astype(BF16)
    off = 4 * HG_W + MLA_Q_RANK + MLA_KV_RANK
    w_mla = jnp.concatenate([w_in[:, 4 * HG_W:off], _rope_pad(w_in[:, off:], 1)], axis=1).astype(BF16)
    wuq = e_w_uq[0].reshape(MLA_Q_RANK, MLA_HEADS, MLA_QK)
    wuq = jnp.concatenate([wuq[..., :MLA_NOPE], _rope_pad(wuq[..., MLA_NOPE:], 2)], axis=-1)
    wuq = wuq.reshape(MLA_Q_RANK, MLA_HEADS * MLA_PAD).astype(BF16)
    wukv = e_w_ukv[0].astype(BF16)
    g_mix0 = norm_mix[0][None, :]

    z_hg = norm_matmul(h, g_mix0, w_hg, F32, 1024, 1024, "in_proj_hgrn")
    z_mla = norm_matmul(h, g_mix0, w_mla, F32, 1024, w_mla.shape[1], "in_proj_mla")
    o_a = hgrn(z_hg, e_lb_logits, e_hg_onorm[0][None, :], batch, seq, 512)
    q, k, v = mla_prep(z_mla, e_q_a_norm[0][None, :], e_kv_a_norm[0][None, :], wuq, wukv,
                       _pad_qk_gain(e_q_norm[0]), _pad_qk_gain(e_k_norm[0]), cos_mla, sin_mla, 512)
    o_b = flash_attention(q, k, v, batch, seq, 512)
    w_out = e_w_out[0].astype(BF16)
    h = res_matmul(h, [o_a, o_b], [w_out[:HG_W], w_out[HG_W:]], 1024, 1024, "out_proj_even")

    def channel_mix(h, i):
        h = conv_ffn(h, norm_ffn[i][None, :], ffn_w_gate[i].astype(BF16), ffn_w_up[i].astype(BF16),
                     ffn_conv_w[i], ffn_conv_b[i][None, :], ffn_w_down[i].astype(BF16), seq, 512, 512)
        return ple(h, norm_ple[i][None, :], ple_w_gate[i].astype(BF16), p[i].reshape(t, PLE_DIM),
                   ple_w_proj[i].astype(BF16), 1024, 1024)

    h = channel_mix(h, 0)

    z_ret = ret_in_proj(h, norm_mix[1][None, :], o_w_in[0].astype(BF16), cos_ret, sin_ret, 1024, 1024)
    o_r = retention(z_ret, batch, seq, 512)
    h = res_matmul(h, [o_r], [o_w_out[0].astype(BF16)], 1024, 1024, "out_proj_odd")
    h = channel_mix(h, 1)
    return h.reshape(batch, seq, d)
```

```python
import functools

import jax
import jax.numpy as jnp
from jax import lax
from jax.experimental import pallas as pl
from jax.experimental.pallas import tpu as pltpu

F32 = jnp.float32
BF16 = jnp.bfloat16

D_MODEL = 2048
PLE_DIM = 256
HG_HEADS = 8
HG_DK = 128
HG_DV = 128
HG_W = HG_HEADS * HG_DK
HG_CHUNK = 64
MLA_HEADS = 8
MLA_Q_RANK = 512
MLA_KV_RANK = 512
MLA_NOPE = 128
MLA_ROPE = 64
MLA_V = 128
MLA_QK = MLA_NOPE + MLA_ROPE
MLA_PAD = 256
VT_ROWS = MLA_V + 16
LOG2E = 1.4426950408889634
RET_HEADS = 8
RET_DK = 256
RET_DV = 512
RET_CHUNK = 256
D_FF = 5632
FFN_STRIP = 256
ROPE_BASE = 10000.0
EPS = 1e-6

VMEM_LIMIT = 56 * 2**20

NT = (((1,), (1,)), ((), ()))
TN = (((0,), (0,)), ((), ()))


def _params(*sem):
    return pltpu.CompilerParams(dimension_semantics=sem, vmem_limit_bytes=VMEM_LIMIT)


def _tile(n, pref):
    t = min(n, pref)
    assert n % t == 0, (n, pref)
    return t


def _dot(a, b):
    return jnp.dot(a, b, preferred_element_type=F32)


def _rms(x, g=None, n=None):
    n = x.shape[-1] if n is None else n
    y = x * lax.rsqrt(jnp.sum(x * x, axis=-1, keepdims=True) * (1.0 / n) + EPS)
    return y if g is None else y * g


def _norm_into(x_ref, g_ref, dst_ref, rows):
    g = g_ref[...]

    def body(c, carry):
        r0 = pl.multiple_of(c * rows, rows)
        dst_ref[pl.ds(r0, rows), :] = _rms(x_ref[pl.ds(r0, rows), :], g).astype(dst_ref.dtype)
        return carry

    lax.fori_loop(0, x_ref.shape[0] // rows, body, 0)


def _rope_tab_kernel(pos_ref, f_ref, sgn_ref, c_ref, s_ref):
    ang = pos_ref[...].astype(F32) * f_ref[...]
    c_ref[...] = jnp.cos(ang)
    s_ref[...] = jnp.sin(ang) * sgn_ref[...]


def rope_tables(pos_col, freq, sgn, name):
    t = pos_col.shape[0]
    tm = _tile(t, 1024)
    return pl.pallas_call(
        _rope_tab_kernel,
        grid=(t // tm,),
        in_specs=[pl.BlockSpec((tm, 1), lambda i: (i, 0)),
                  pl.BlockSpec((1, 128), lambda i: (0, 0)),
                  pl.BlockSpec((1, 128), lambda i: (0, 0))],
        out_specs=[pl.BlockSpec((tm, 128), lambda i: (i, 0)),
                   pl.BlockSpec((tm, 128), lambda i: (i, 0))],
        out_shape=[jax.ShapeDtypeStruct((t, 128), F32)] * 2,
        compiler_params=_params("parallel"),
        name=name,
    )(pos_col, freq, sgn)


def _norm_matmul_kernel(x_ref, g_ref, w_ref, o_ref, xn_ref):
    @pl.when(pl.program_id(1) == 0)
    def _():
        _norm_into(x_ref, g_ref, xn_ref, 256)

    o_ref[...] = _dot(xn_ref[...], w_ref[...]).astype(o_ref.dtype)


def norm_matmul(x, g, w, out_dtype, tm, tn, name):
    t, k = x.shape
    n = w.shape[1]
    tm, tn = _tile(t, tm), _tile(n, tn)
    return pl.pallas_call(
        _norm_matmul_kernel,
        grid=(t // tm, n // tn),
        in_specs=[pl.BlockSpec((tm, k), lambda i, j: (i, 0)),
                  pl.BlockSpec((1, k), lambda i, j: (0, 0)),
                  pl.BlockSpec((k, tn), lambda i, j: (0, j))],
        out_specs=pl.BlockSpec((tm, tn), lambda i, j: (i, j)),
        out_shape=jax.ShapeDtypeStruct((t, n), out_dtype),
        scratch_shapes=[pltpu.VMEM((tm, k), BF16)],
        compiler_params=_params("parallel", "arbitrary"),
        name=name,
    )(x, g, w)


def _ret_in_kernel(x_ref, g_ref, w_ref, cos_ref, sin_ref, o_ref, xn_ref, *, n_q_tiles, n_rope_tiles):
    j = pl.program_id(1)

    @pl.when(j == 0)
    def _():
        _norm_into(x_ref, g_ref, xn_ref, 256)

    acc = _dot(xn_ref[...], w_ref[...])

    @pl.when(j < n_rope_tiles)
    def _():
        scale = jnp.where(j >= n_q_tiles, RET_DK ** -0.5, 1.0).astype(F32)
        cos = cos_ref[...] * scale
        sin = sin_ref[...] * scale
        half = RET_DK // 2
        for hh in range(acc.shape[1] // RET_DK):
            x1 = acc[:, hh * RET_DK:hh * RET_DK + half]
            x2 = acc[:, hh * RET_DK + half:(hh + 1) * RET_DK]
            o_ref[:, hh * RET_DK:hh * RET_DK + half] = (x1 * cos - x2 * sin).astype(o_ref.dtype)
            o_ref[:, hh * RET_DK + half:(hh + 1) * RET_DK] = (x2 * cos + x1 * sin).astype(o_ref.dtype)

    @pl.when(j >= n_rope_tiles)
    def _():
        o_ref[...] = acc.astype(o_ref.dtype)


def ret_in_proj(x, g, w, cos, sin, tm, tn):
    t, k = x.shape
    n = w.shape[1]
    tm, tn = _tile(t, tm), _tile(n, tn)
    qw = RET_HEADS * RET_DK
    kern = functools.partial(_ret_in_kernel, n_q_tiles=qw // tn, n_rope_tiles=2 * qw // tn)
    return pl.pallas_call(
        kern,
        grid=(t // tm, n // tn),
        in_specs=[pl.BlockSpec((tm, k), lambda i, j: (i, 0)),
                  pl.BlockSpec((1, k), lambda i, j: (0, 0)),
                  pl.BlockSpec((k, tn), lambda i, j: (0, j)),
                  pl.BlockSpec((tm, 128), lambda i, j: (i, 0)),
                  pl.BlockSpec((tm, 128), lambda i, j: (i, 0))],
        out_specs=pl.BlockSpec((tm, tn), lambda i, j: (i, j)),
        out_shape=jax.ShapeDtypeStruct((t, n), BF16),
        scratch_shapes=[pltpu.VMEM((tm, k), BF16)],
        compiler_params=_params("parallel", "arbitrary"),
        name="ret_in_proj",
    )(x, g, w, cos, sin)


def _res_matmul_kernel(*refs, n_lhs):
    h_ref, o_ref = refs[0], refs[-1]
    acc = h_ref[...]
    for a_ref, w_ref in zip(refs[1:1 + n_lhs], refs[1 + n_lhs:1 + 2 * n_lhs]):
        acc = acc + _dot(a_ref[...], w_ref[...])
    o_ref[...] = acc


def res_matmul(h, lhs, ws, tm, tn, name):
    t, n = h.shape
    tm, tn = _tile(t, tm), _tile(n, tn)
    in_specs = [pl.BlockSpec((tm, tn), lambda i, j: (i, j))]
    in_specs += [pl.BlockSpec((tm, a.shape[1]), lambda i, j: (i, 0)) for a in lhs]
    in_specs += [pl.BlockSpec((w.shape[0], tn), lambda i, j: (0, j)) for w in ws]
    return pl.pallas_call(
        functools.partial(_res_matmul_kernel, n_lhs=len(lhs)),
        grid=(t // tm, n // tn),
        in_specs=in_specs,
        out_specs=pl.BlockSpec((tm, tn), lambda i, j: (i, j)),
        out_shape=jax.ShapeDtypeStruct((t, n), F32),
        compiler_params=_params("parallel", "arbitrary"),
        name=name,
    )(h, *lhs, *ws)


def _ffn_kernel(h_ref, g_ref, wg_ref, wu_ref, cw_ref, cb_ref, wd_ref, o_ref,
                u_ref, gs_ref, tail_ref, *, blocks_per_seq):
    i, j = pl.program_id(0), pl.program_id(1)
    tm = h_ref.shape[0]

    @pl.when(j == 0)
    def _():
        _norm_into(h_ref, g_ref, u_ref, 256)
        o_ref[...] = h_ref[...]

    @pl.when(jnp.logical_and(i == 0, j == 0))
    def _():
        tail_ref[...] = jnp.zeros(tail_ref.shape, F32)

    u = u_ref[...]
    seq_start = (i % blocks_per_seq) == 0
    prev = jnp.where(seq_start, 0.0, tail_ref[j])
    cw, cb = cw_ref[...], cb_ref[...]
    contrib = None
    for c0 in range(0, wg_ref.shape[1], FFN_STRIP):
        cols = slice(c0, c0 + FFN_STRIP)
        a = _dot(u, wg_ref[:, cols])
        up = _dot(u, wu_ref[:, cols])
        gs_ref[0:8, cols] = prev[:, cols]
        gs_ref[8:8 + tm, cols] = a
        tail_ref[j, :, cols] = a[tm - 8:tm, :]
        c = (cb[:, cols] + cw[0:1, cols] * gs_ref[6:6 + tm, cols] + cw[1:2, cols] * gs_ref[7:7 + tm, cols]
             + cw[2:3, cols] * a)
        act = (c * jax.nn.sigmoid(c) * up).astype(BF16)
        part = _dot(act, wd_ref[cols, :])
        contrib = part if contrib is None else contrib + part
    o_ref[...] += contrib


def conv_ffn(h, g, wg, wu, cw, cb, wd, seq, tm, tf):
    t, d = h.shape
    f = wg.shape[1]
    tm, tf = _tile(seq, tm), _tile(f, tf)
    kern = functools.partial(_ffn_kernel, blocks_per_seq=seq // tm)
    return pl.pallas_call(
        kern,
        grid=(t // tm, f // tf),
        in_specs=[pl.BlockSpec((tm, d), lambda i, j: (i, 0)),
                  pl.BlockSpec((1, d), lambda i, j: (0, 0)),
                  pl.BlockSpec((d, tf), lambda i, j: (0, j)),
                  pl.BlockSpec((d, tf), lambda i, j: (0, j)),
                  pl.BlockSpec((3, tf), lambda i, j: (0, j)),
                  pl.BlockSpec((1, tf), lambda i, j: (0, j)),
                  pl.BlockSpec((tf, d), lambda i, j: (j, 0))],
        out_specs=pl.BlockSpec((tm, d), lambda i, j: (i, 0)),
        out_shape=jax.ShapeDtypeStruct((t, d), F32),
        scratch_shapes=[pltpu.VMEM((tm, d), BF16),
                        pltpu.VMEM((tm + 8, tf), F32),
                        pltpu.VMEM((f // tf, 8, tf), F32)],
        compiler_params=_params("arbitrary", "arbitrary"),
        name="conv_ffn",
    )(h, g, wg, wu, cw, cb, wd)


def _ple_kernel(h_ref, g_ref, wgate_ref, p_ref, wproj_ref, o_ref, xn_ref, pb_ref):
    j = pl.program_id(1)
    tn = o_ref.shape[1]

    @pl.when(j == 0)
    def _():
        _norm_into(h_ref, g_ref, xn_ref, 256)
        pb_ref[...] = p_ref[...].astype(BF16)

    gate = jax.nn.sigmoid(_dot(xn_ref[...], wgate_ref[...]))
    proj = _dot(pb_ref[...], wproj_ref[...])
    c0 = pl.multiple_of(j * tn, tn)
    o_ref[...] = h_ref[:, pl.ds(c0, tn)] + proj * gate


def ple(h, g, wgate, p, wproj, tm, tn):
    t, d = h.shape
    tm, tn = _tile(t, tm), _tile(d, tn)
    return pl.pallas_call(
        _ple_kernel,
        grid=(t // tm, d // tn),
        in_specs=[pl.BlockSpec((tm, d), lambda i, j: (i, 0)),
                  pl.BlockSpec((1, d), lambda i, j: (0, 0)),
                  pl.BlockSpec((d, tn), lambda i, j: (0, j)),
                  pl.BlockSpec((tm, PLE_DIM), lambda i, j: (i, 0)),
                  pl.BlockSpec((PLE_DIM, tn), lambda i, j: (0, j))],
        out_specs=pl.BlockSpec((tm, tn), lambda i, j: (i, j)),
        out_shape=jax.ShapeDtypeStruct((t, d), F32),
        scratch_shapes=[pltpu.VMEM((tm, d), BF16), pltpu.VMEM((tm, PLE_DIM), BF16)],
        compiler_params=_params("parallel", "arbitrary"),
        name="ple",
    )(h, g, wgate, p, wproj)


def _hgrn_kernel(q_ref, f_ref, i_ref, g_ref, lbl_ref, on_ref, o_ref, st_ref):
    c = HG_CHUNK

    @pl.when(pl.program_id(1) == 0)
    def _():
        st_ref[...] = jnp.zeros(st_ref.shape, F32)

    lg = lbl_ref[...]
    e = jnp.exp(lg - jnp.max(lg, axis=0, keepdims=True))
    lb = e[0:1, :] / jnp.sum(e, axis=0, keepdims=True)
    onorm = on_ref[...]
    row = lax.broadcasted_iota(jnp.int32, (c, c), 0)
    col = lax.broadcasted_iota(jnp.int32, (c, c), 1)
    causal = row >= col
    tril = jnp.where(causal, 1.0, 0.0).astype(BF16)

    def chunk(ci, carry):
        r0 = pl.multiple_of(ci * c, c)
        rows = pl.ds(r0, c)
        for h in range(HG_HEADS):
            cols = slice(h * HG_DK, (h + 1) * HG_DK)
            q = q_ref[rows, cols]
            v = i_ref[rows, cols].astype(BF16)
            lbh = lb[:, cols]
            f = lbh + (1.0 - lbh) * jax.nn.sigmoid(f_ref[rows, cols])
            k = 1.0 - f
            lf = jnp.log(f)
            hi = lf.astype(BF16)
            r1 = lf - hi.astype(F32)
            mid = r1.astype(BF16)
            lo = (r1 - mid.astype(F32)).astype(BF16)
            bc = _dot(tril, hi) + _dot(tril, mid) + _dot(tril, lo)
            ref = bc[c // 2 - 1:c // 2, :]
            b_last = bc[c - 1:c, :]
            q_rel = (q * jnp.exp(bc - ref)).astype(BF16)
            k_rel = (k * jnp.exp(ref - bc)).astype(BF16)
            a = lax.dot_general(q_rel, k_rel, NT, preferred_element_type=F32)
            a = jnp.where(causal, a, 0.0).astype(BF16)
            st = st_ref[h]
            out = _dot(a, v) + lax.dot_general((q * jnp.exp(bc)).astype(BF16), st.astype(BF16), NT,
                                               preferred_element_type=F32)
            k_dec = (k * jnp.exp(b_last - bc)).astype(BF16)
            st_ref[h] = st * jnp.exp(b_last) + lax.dot_general(v, k_dec, TN, preferred_element_type=F32)
            g = g_ref[rows, cols]
            o_ref[rows, cols] = (_rms(out, onorm) * (g * jax.nn.sigmoid(g))).astype(o_ref.dtype)
        return carry

    lax.fori_loop(0, q_ref.shape[0] // c, chunk, 0)


def hgrn(z, lb_logits, onorm, batch, seq, cb):
    t = z.shape[0]
    cb = _tile(seq, cb)
    nblk = seq // cb
    spec = lambda part: pl.BlockSpec((cb, HG_W), lambda b, s, part=part: (b * nblk + s, part))
    return pl.pallas_call(
        _hgrn_kernel,
        grid=(batch, nblk),
        in_specs=[spec(0), spec(1), spec(2), spec(3),
                  pl.BlockSpec(lb_logits.shape, lambda b, s: (0, 0)),
                  pl.BlockSpec((1, HG_DV), lambda b, s: (0, 0))],
        out_specs=pl.BlockSpec((cb, HG_W), lambda b, s: (b * nblk + s, 0)),
        out_shape=jax.ShapeDtypeStruct((t, HG_W), BF16),
        scratch_shapes=[pltpu.VMEM((HG_HEADS, HG_DV, HG_DK), F32)],
        compiler_params=_params("parallel", "arbitrary"),
        name="hgrn2",
    )(z, z, z, z, lb_logits, onorm)


def _mla_prep_kernel(z_ref, qa_ref, kva_ref, wuq_ref, wukv_ref, qn_ref, kn_ref, cos_ref, sin_ref,
                     qt_out, k_out, vt_out):
    cq = _rms(z_ref[:, 0:MLA_Q_RANK], qa_ref[...]).astype(BF16)
    ckv = _rms(z_ref[:, MLA_Q_RANK:MLA_Q_RANK + MLA_KV_RANK], kva_ref[...]).astype(BF16)
    kpe = z_ref[:, MLA_Q_RANK + MLA_KV_RANK:]
    qf = _dot(cq, wuq_ref[...])
    kvf = _dot(ckv, wukv_ref[...])
    cos, sin = cos_ref[...], sin_ref[...]
    qn, kn = qn_ref[...], kn_ref[...]
    scale = MLA_QK ** -0.5 * LOG2E
    tm = z_ref.shape[0]

    def rope(x):
        return x * cos + pltpu.roll(x, 64, 1) * sin

    kpe_ss = jnp.sum(kpe * kpe, axis=-1, keepdims=True)
    k_rope = rope(kpe * kn[:, MLA_NOPE:])
    ones_rows = jnp.where(lax.broadcasted_iota(jnp.int32, (VT_ROWS - MLA_V, tm), 0) == 0, 1.0, 0.0)
    for h in range(MLA_HEADS):
        c0 = h * MLA_PAD
        qh = qf[:, c0:c0 + MLA_PAD]
        rq = lax.rsqrt(jnp.sum(qh * qh, axis=-1, keepdims=True) * (1.0 / MLA_QK) + EPS) * scale
        qh = qh * rq * qn
        qt_out[h, 0:MLA_NOPE, :] = qh[:, :MLA_NOPE].T.astype(BF16)
        qt_out[h, MLA_NOPE:MLA_PAD, :] = rope(qh[:, MLA_NOPE:]).T.astype(BF16)
        kh = kvf[:, c0:c0 + MLA_NOPE]
        rk = lax.rsqrt((jnp.sum(kh * kh, axis=-1, keepdims=True) + kpe_ss) * (1.0 / MLA_QK) + EPS)
        k_out[:, c0:c0 + MLA_NOPE] = (kh * rk * kn[:, :MLA_NOPE]).astype(BF16)
        k_out[:, c0 + MLA_NOPE:c0 + MLA_PAD] = (k_rope * rk).astype(BF16)
        vt_out[h, 0:MLA_V, :] = kvf[:, c0 + MLA_NOPE:c0 + MLA_PAD].T.astype(BF16)
        vt_out[h, MLA_V:VT_ROWS, :] = ones_rows.astype(BF16)


def mla_prep(z, qa, kva, wuq, wukv, qn, kn, cos, sin, tm):
    t, zw = z.shape
    tm = _tile(t, tm)
    full = lambda a: pl.BlockSpec(a.shape, lambda i: (0, 0))
    rows = lambda w: pl.BlockSpec((tm, w), lambda i: (i, 0))
    cols = lambda r: pl.BlockSpec((MLA_HEADS, r, tm), lambda i: (0, 0, i))
    return pl.pallas_call(
        _mla_prep_kernel,
        grid=(t // tm,),
        in_specs=[rows(zw), full(qa), full(kva), full(wuq), full(wukv), full(qn), full(kn),
                  rows(128), rows(128)],
        out_specs=[cols(MLA_PAD), rows(MLA_HEADS * MLA_PAD), cols(VT_ROWS)],
        out_shape=[jax.ShapeDtypeStruct((MLA_HEADS, MLA_PAD, t), BF16),
                   jax.ShapeDtypeStruct((t, MLA_HEADS * MLA_PAD), BF16),
                   jax.ShapeDtypeStruct((MLA_HEADS, VT_ROWS, t), BF16)],
        compiler_params=_params("parallel"),
        name="mla_prep",
    )(z, qa, kva, wuq, wukv, qn, kn, cos, sin)


def _flash_kernel(qt_ref, k_ref, vt_ref, o_ref, acc_ref, *, tq, tk):
    qi = pl.program_id(2)
    qt = qt_ref[0]
    acc_ref[...] = jnp.zeros(acc_ref.shape, F32)

    def step(j, m_prev, masked):
        k0 = pl.multiple_of(j * tk, tk)
        s = _dot(k_ref[pl.ds(k0, tk), :], qt)
        if masked:
            kv = k0 + lax.broadcasted_iota(jnp.int32, s.shape, 0)
            qq = qi * tq + lax.broadcasted_iota(jnp.int32, s.shape, 1)
            s = jnp.where(kv <= qq, s, -jnp.inf)
        m_new = jnp.maximum(m_prev, jnp.max(s, axis=0, keepdims=True))
        alpha = jnp.exp2(m_prev - m_new)
        p = jnp.exp2(s - m_new).astype(BF16)
        acc_ref[...] = alpha * acc_ref[...] + _dot(vt_ref[0, :, pl.ds(k0, tk)], p)
        return m_new

    n_full = qi * (tq // tk)
    m = lax.fori_loop(0, n_full, lambda j, m: step(j, m, False), jnp.full((1, tq), -jnp.inf, F32))
    for d in range(tq // tk):
        m = step(n_full + d, m, True)
    acc = acc_ref[...]
    o_ref[...] = (acc[0:MLA_V, :] / acc[MLA_V:MLA_V + 1, :]).T.astype(o_ref.dtype)


def flash_attention(qt, k, vt, batch, seq, tq, tk):
    t = k.shape[0]
    tq = _tile(seq, tq)
    tk = _tile(tq, tk)
    nq = seq // tq
    return pl.pallas_call(
        functools.partial(_flash_kernel, tq=tq, tk=tk),
        grid=(batch, MLA_HEADS, nq),
        in_specs=[pl.BlockSpec((1, MLA_PAD, tq), lambda b, h, i: (h, 0, b * nq + i)),
                  pl.BlockSpec((seq, MLA_PAD), lambda b, h, i: (b, h)),
                  pl.BlockSpec((1, VT_ROWS, seq), lambda b, h, i: (h, 0, b))],
        out_specs=pl.BlockSpec((tq, MLA_V), lambda b, h, i: (b * nq + i, h)),
        out_shape=jax.ShapeDtypeStruct((t, MLA_HEADS * MLA_V), BF16),
        scratch_shapes=[pltpu.VMEM((VT_ROWS, tq), F32)],
        compiler_params=_params("parallel", "parallel", "arbitrary"),
        name="mla_flash",
    )(qt, k, vt)


def _ret_kernel(q_ref, k_ref, v_ref, g_ref, o_ref, r_ref):
    c = RET_CHUNK
    hf = pl.program_id(1).astype(F32)

    @pl.when(pl.program_id(2) == 0)
    def _():
        r_ref[...] = jnp.zeros(r_ref.shape, F32)

    def log_gamma(shape):
        return jnp.log(1.0 - jnp.exp2(-5.0 - jnp.full(shape, hf, F32)))

    row = lax.broadcasted_iota(jnp.int32, (c, c), 0)
    col = lax.broadcasted_iota(jnp.int32, (c, c), 1)
    diff = (row - col).astype(F32)
    d_intra = jnp.where(diff >= 0, jnp.exp(jnp.where(diff >= 0, diff, 0.0) * log_gamma((c, c))), 0.0)
    idx = lax.broadcasted_iota(jnp.int32, (c, RET_DK), 0).astype(F32)
    lg_k = log_gamma((c, RET_DK))
    q_dec = jnp.exp((idx + 1.0) * lg_k)
    k_dec = jnp.exp((c - 1.0 - idx) * lg_k)
    c_dec = jnp.exp(c * log_gamma((1, RET_DV)))

    for ci in range(q_ref.shape[0] // c):
        rows = slice(ci * c, (ci + 1) * c)
        q, k, v = q_ref[rows, :], k_ref[rows, :], v_ref[rows, :]
        a = lax.dot_general(q, k, NT, preferred_element_type=F32) * d_intra
        r = r_ref[...]
        out = _dot(a.astype(BF16), v) + _dot((q.astype(F32) * q_dec).astype(BF16), r.astype(BF16))
        r_ref[...] = r * c_dec + lax.dot_general((k.astype(F32) * k_dec).astype(BF16), v, TN,
                                                 preferred_element_type=F32)
        g = g_ref[rows, :].astype(F32)
        o_ref[rows, :] = (_rms(out) * (g * jax.nn.sigmoid(g))).astype(o_ref.dtype)


def retention(z, batch, seq, cb):
    t = z.shape[0]
    cb = _tile(seq, cb)
    assert cb % RET_CHUNK == 0
    nblk = seq // cb
    nqk = RET_HEADS * RET_DK // RET_DK
    nv = (2 * RET_HEADS * RET_DK) // RET_DV
    return pl.pallas_call(
        _ret_kernel,
        grid=(batch, RET_HEADS, nblk),
        in_specs=[pl.BlockSpec((cb, RET_DK), lambda b, h, s: (b * nblk + s, h)),
                  pl.BlockSpec((cb, RET_DK), lambda b, h, s: (b * nblk + s, nqk + h)),
                  pl.BlockSpec((cb, RET_DV), lambda b, h, s: (b * nblk + s, nv + h)),
                  pl.BlockSpec((cb, RET_DV), lambda b, h, s: (b * nblk + s, nv + RET_HEADS + h))],
        out_specs=pl.BlockSpec((cb, RET_DV), lambda b, h, s: (b * nblk + s, h)),
        out_shape=jax.ShapeDtypeStruct((t, RET_HEADS * RET_DV), BF16),
        scratch_shapes=[pltpu.VMEM((RET_DK, RET_DV), F32)],
        compiler_params=_params("parallel", "parallel", "arbitrary"),
        name="retention",
    )(z, z, z, z)


def _rope_pad(a, axis):
    x1, x2 = jnp.split(a, 2, axis=axis)
    z = jnp.zeros_like(x1)
    return jnp.concatenate([x1, z, x2, z], axis=axis)


def _pad_qk_gain(g):
    return jnp.concatenate([g[:MLA_NOPE], _rope_pad(g[MLA_NOPE:], 0)])[None, :]


def kernel(x, p, positions, norm_mix, norm_ffn, norm_ple, e_w_in, e_lb_logits, e_q_a_norm, e_kv_a_norm, e_w_uq, e_w_ukv, e_q_norm, e_k_norm, e_hg_onorm, e_w_out, o_w_in, o_w_out, ffn_w_gate, ffn_w_up, ffn_conv_w, ffn_conv_b, ffn_w_down, ple_w_proj, ple_w_gate):
    batch, seq, d = x.shape
    t = batch * seq
    h = x.reshape(t, d)
    pos_col = positions.reshape(t, 1)

    ones = jnp.ones((1, 128), F32)
    f_ret = (ROPE_BASE ** (-jnp.arange(RET_DK // 2, dtype=F32) / (RET_DK // 2)))[None, :]
    f_mla = ROPE_BASE ** (-jnp.arange(MLA_ROPE // 2, dtype=F32) / (MLA_ROPE // 2))
    f_mla = _rope_pad(jnp.concatenate([f_mla, f_mla]), 0)[None, :]
    sgn_mla = _rope_pad(jnp.concatenate([-jnp.ones(32, F32), jnp.ones(32, F32)]), 0)[None, :]
    cos_ret, sin_ret = rope_tables(pos_col, f_ret, ones, "rope_tab_ret")
    cos_mla, sin_mla = rope_tables(pos_col, f_mla, sgn_mla, "rope_tab_mla")

    w_in = e_w_in[0]
    w_hg = w_in[:, :4 * HG_W].astype(BF16)
    off = 4 * HG_W + MLA_Q_RANK + MLA_KV_RANK
    w_mla = jnp.concatenate([w_in[:, 4 * HG_W:off], _rope_pad(w_in[:, off:], 1)], axis=1).astype(BF16)
    wuq = e_w_uq[0].reshape(MLA_Q_RANK, MLA_HEADS, MLA_QK)
    wuq = jnp.concatenate([wuq[..., :MLA_NOPE], _rope_pad(wuq[..., MLA_NOPE:], 2)], axis=-1)
    wuq = wuq.reshape(MLA_Q_RANK, MLA_HEADS * MLA_PAD).astype(BF16)
    wukv = e_w_ukv[0].astype(BF16)
    g_mix0 = norm_mix[0][None, :]

    z_hg = norm_matmul(h, g_mix0, w_hg, F32, 1024, 1024, "in_proj_hgrn")
    z_mla = norm_matmul(h, g_mix0, w_mla, F32, 1024, w_mla.shape[1], "in_proj_mla")
    o_a = hgrn(z_hg, e_lb_logits, e_hg_onorm[0][None, :], batch, seq, 512)
    qt, k, vt = mla_prep(z_mla, e_q_a_norm[0][None, :], e_kv_a_norm[0][None, :], wuq, wukv,
                       _pad_qk_gain(e_q_norm[0]), _pad_qk_gain(e_k_norm[0]), cos_mla, sin_mla, 512)
    o_b = flash_attention(qt, k, vt, batch, seq, 1024, 512)
    w_out = e_w_out[0].astype(BF16)
    h = res_matmul(h, [o_a, o_b], [w_out[:HG_W], w_out[HG_W:]], 1024, 1024, "out_proj_even")

    def channel_mix(h, i):
        h = conv_ffn(h, norm_ffn[i][None, :], ffn_w_gate[i].astype(BF16), ffn_w_up[i].astype(BF16),
                     ffn_conv_w[i], ffn_conv_b[i][None, :], ffn_w_down[i].astype(BF16), seq, 512, 512)
        return ple(h, norm_ple[i][None, :], ple_w_gate[i].astype(BF16), p[i].reshape(t, PLE_DIM),
                   ple_w_proj[i].astype(BF16), 1024, 1024)

    h = channel_mix(h, 0)

    z_ret = ret_in_proj(h, norm_mix[1][None, :], o_w_in[0].astype(BF16), cos_ret, sin_ret, 1024, 1024)
    o_r = retention(z_ret, batch, seq, 512)
    h = res_matmul(h, [o_r], [o_w_out[0].astype(BF16)], 1024, 1024, "out_proj_odd")
    h = channel_mix(h, 1)
    return h.reshape(batch, seq, d)
```

```python
import functools

import jax
import jax.numpy as jnp
from jax import lax
from jax.experimental import pallas as pl
from jax.experimental.pallas import tpu as pltpu

F32 = jnp.float32
BF16 = jnp.bfloat16

D_MODEL = 2048
PLE_DIM = 256
HG_HEADS = 8
HG_DK = 128
HG_DV = 128
HG_W = HG_HEADS * HG_DK
HG_CHUNK = 64
MLA_HEADS = 8
MLA_Q_RANK = 512
MLA_KV_RANK = 512
MLA_NOPE = 128
MLA_ROPE = 64
MLA_V = 128
MLA_QK = MLA_NOPE + MLA_ROPE
MLA_PAD = 256
VT_ROWS = MLA_V + 16
LOG2E = 1.4426950408889634
RET_HEADS = 8
RET_DK = 256
RET_DV = 512
RET_CHUNK = 256
D_FF = 5632
FFN_STRIP = 256
ROPE_BASE = 10000.0
EPS = 1e-6

VMEM_LIMIT = 56 * 2**20

NT = (((1,), (1,)), ((), ()))
TN = (((0,), (0,)), ((), ()))


def _params(*sem):
    return pltpu.CompilerParams(dimension_semantics=sem, vmem_limit_bytes=VMEM_LIMIT)


def _tile(n, pref):
    t = min(n, pref)
    assert n % t == 0, (n, pref)
    return t


def _dot(a, b):
    return jnp.dot(a, b, preferred_element_type=F32)


def _rms(x, g=None, n=None):
    n = x.shape[-1] if n is None else n
    y = x * lax.rsqrt(jnp.sum(x * x, axis=-1, keepdims=True) * (1.0 / n) + EPS)
    return y if g is None else y * g


def _norm_into(x_ref, g_ref, dst_ref, rows):
    g = g_ref[...]

    def body(c, carry):
        r0 = pl.multiple_of(c * rows, rows)
        dst_ref[pl.ds(r0, rows), :] = _rms(x_ref[pl.ds(r0, rows), :], g).astype(dst_ref.dtype)
        return carry

    lax.fori_loop(0, x_ref.shape[0] // rows, body, 0)


def _rope_tab_kernel(pos_ref, f_ref, sgn_ref, c_ref, s_ref):
    ang = pos_ref[...].astype(F32) * f_ref[...]
    c_ref[...] = jnp.cos(ang)
    s_ref[...] = jnp.sin(ang) * sgn_ref[...]


def rope_tables(pos_col, freq, sgn, name):
    t = pos_col.shape[0]
    tm = _tile(t, 1024)
    return pl.pallas_call(
        _rope_tab_kernel,
        grid=(t // tm,),
        in_specs=[pl.BlockSpec((tm, 1), lambda i: (i, 0)),
                  pl.BlockSpec((1, 128), lambda i: (0, 0)),
                  pl.BlockSpec((1, 128), lambda i: (0, 0))],
        out_specs=[pl.BlockSpec((tm, 128), lambda i: (i, 0)),
                   pl.BlockSpec((tm, 128), lambda i: (i, 0))],
        out_shape=[jax.ShapeDtypeStruct((t, 128), F32)] * 2,
        compiler_params=_params("parallel"),
        name=name,
    )(pos_col, freq, sgn)


def _norm_matmul_kernel(x_ref, g_ref, w_ref, o_ref, xn_ref):
    @pl.when(pl.program_id(1) == 0)
    def _():
        _norm_into(x_ref, g_ref, xn_ref, 256)

    o_ref[...] = _dot(xn_ref[...], w_ref[...]).astype(o_ref.dtype)


def norm_matmul(x, g, w, out_dtype, tm, tn, name):
    t, k = x.shape
    n = w.shape[1]
    tm, tn = _tile(t, tm), _tile(n, tn)
    return pl.pallas_call(
        _norm_matmul_kernel,
        grid=(t // tm, n // tn),
        in_specs=[pl.BlockSpec((tm, k), lambda i, j: (i, 0)),
                  pl.BlockSpec((1, k), lambda i, j: (0, 0)),
                  pl.BlockSpec((k, tn), lambda i, j: (0, j))],
        out_specs=pl.BlockSpec((tm, tn), lambda i, j: (i, j)),
        out_shape=jax.ShapeDtypeStruct((t, n), out_dtype),
        scratch_shapes=[pltpu.VMEM((tm, k), BF16)],
        compiler_params=_params("parallel", "arbitrary"),
        name=name,
    )(x, g, w)


def _ret_in_kernel(x_ref, g_ref, w_ref, cos_ref, sin_ref, o_ref, xn_ref, *, n_q_tiles, n_rope_tiles):
    j = pl.program_id(1)

    @pl.when(j == 0)
    def _():
        _norm_into(x_ref, g_ref, xn_ref, 256)

    acc = _dot(xn_ref[...], w_ref[...])
    is_rope = j < n_rope_tiles
    scale = jnp.where(j >= n_q_tiles, RET_DK ** -0.5, 1.0).astype(F32)
    cos = cos_ref[...] * scale
    sin = sin_ref[...] * scale
    half = RET_DK // 2
    for hh in range(acc.shape[1] // RET_DK):
        x1 = acc[:, hh * RET_DK:hh * RET_DK + half]
        x2 = acc[:, hh * RET_DK + half:(hh + 1) * RET_DK]
        o_ref[:, hh * RET_DK:hh * RET_DK + half] = jnp.where(
            is_rope, x1 * cos - x2 * sin, x1).astype(o_ref.dtype)
        o_ref[:, hh * RET_DK + half:(hh + 1) * RET_DK] = jnp.where(
            is_rope, x2 * cos + x1 * sin, x2).astype(o_ref.dtype)


def ret_in_proj(x, g, w, cos, sin, tm, tn):
    t, k = x.shape
    n = w.shape[1]
    tm, tn = _tile(t, tm), _tile(n, tn)
    qw = RET_HEADS * RET_DK
    kern = functools.partial(_ret_in_kernel, n_q_tiles=qw // tn, n_rope_tiles=2 * qw // tn)
    return pl.pallas_call(
        kern,
        grid=(t // tm, n // tn),
        in_specs=[pl.BlockSpec((tm, k), lambda i, j: (i, 0)),
                  pl.BlockSpec((1, k), lambda i, j: (0, 0)),
                  pl.BlockSpec((k, tn), lambda i, j: (0, j)),
                  pl.BlockSpec((tm, 128), lambda i, j: (i, 0)),
                  pl.BlockSpec((tm, 128), lambda i, j: (i, 0))],
        out_specs=pl.BlockSpec((tm, tn), lambda i, j: (i, j)),
        out_shape=jax.ShapeDtypeStruct((t, n), BF16),
        scratch_shapes=[pltpu.VMEM((tm, k), BF16)],
        compiler_params=_params("parallel", "arbitrary"),
        name="ret_in_proj",
    )(x, g, w, cos, sin)


def _res_matmul_kernel(*refs, n_lhs):
    h_ref, o_ref = refs[0], refs[-1]
    acc = h_ref[...]
    for a_ref, w_ref in zip(refs[1:1 + n_lhs], refs[1 + n_lhs:1 + 2 * n_lhs]):
        acc = acc + _dot(a_ref[...], w_ref[...])
    o_ref[...] = acc


def res_matmul(h, lhs, ws, tm, tn, name):
    t, n = h.shape
    tm, tn = _tile(t, tm), _tile(n, tn)
    in_specs = [pl.BlockSpec((tm, tn), lambda i, j: (i, j))]
    in_specs += [pl.BlockSpec((tm, a.shape[1]), lambda i, j: (i, 0)) for a in lhs]
    in_specs += [pl.BlockSpec((w.shape[0], tn), lambda i, j: (0, j)) for w in ws]
    return pl.pallas_call(
        functools.partial(_res_matmul_kernel, n_lhs=len(lhs)),
        grid=(t // tm, n // tn),
        in_specs=in_specs,
        out_specs=pl.BlockSpec((tm, tn), lambda i, j: (i, j)),
        out_shape=jax.ShapeDtypeStruct((t, n), F32),
        compiler_params=_params("parallel", "arbitrary"),
        name=name,
    )(h, *lhs, *ws)


def _ffn_kernel(h_ref, g_ref, wg_ref, wu_ref, cw_ref, cb_ref, wd_ref, o_ref,
                u_ref, gs_ref, tail_ref, *, blocks_per_seq):
    i, j = pl.program_id(0), pl.program_id(1)
    tm = h_ref.shape[0]

    @pl.when(j == 0)
    def _():
        _norm_into(h_ref, g_ref, u_ref, 256)
        o_ref[...] = h_ref[...]

    @pl.when(jnp.logical_and(i == 0, j == 0))
    def _():
        tail_ref[...] = jnp.zeros(tail_ref.shape, F32)

    u = u_ref[...]
    seq_start = (i % blocks_per_seq) == 0
    prev = jnp.where(seq_start, 0.0, tail_ref[j])
    cw, cb = cw_ref[...], cb_ref[...]
    contrib = None
    for c0 in range(0, wg_ref.shape[1], FFN_STRIP):
        cols = slice(c0, c0 + FFN_STRIP)
        a = _dot(u, wg_ref[:, cols])
        up = _dot(u, wu_ref[:, cols])
        gs_ref[0:8, cols] = prev[:, cols]
        gs_ref[8:8 + tm, cols] = a
        tail_ref[j, :, cols] = a[tm - 8:tm, :]
        c = (cb[:, cols] + cw[0:1, cols] * gs_ref[6:6 + tm, cols] + cw[1:2, cols] * gs_ref[7:7 + tm, cols]
             + cw[2:3, cols] * a)
        act = (c * jax.nn.sigmoid(c) * up).astype(BF16)
        part = _dot(act, wd_ref[cols, :])
        contrib = part if contrib is None else contrib + part
    o_ref[...] += contrib


def conv_ffn(h, g, wg, wu, cw, cb, wd, seq, tm, tf):
    t, d = h.shape
    f = wg.shape[1]
    tm, tf = _tile(seq, tm), _tile(f, tf)
    kern = functools.partial(_ffn_kernel, blocks_per_seq=seq // tm)
    return pl.pallas_call(
        kern,
        grid=(t // tm, f // tf),
        in_specs=[pl.BlockSpec((tm, d), lambda i, j: (i, 0)),
                  pl.BlockSpec((1, d), lambda i, j: (0, 0)),
                  pl.BlockSpec((d, tf), lambda i, j: (0, j)),
                  pl.BlockSpec((d, tf), lambda i, j: (0, j)),
                  pl.BlockSpec((3, tf), lambda i, j: (0, j)),
                  pl.BlockSpec((1, tf), lambda i, j: (0, j)),
                  pl.BlockSpec((tf, d), lambda i, j: (j, 0))],
        out_specs=pl.BlockSpec((tm, d), lambda i, j: (i, 0)),
        out_shape=jax.ShapeDtypeStruct((t, d), F32),
        scratch_shapes=[pltpu.VMEM((tm, d), BF16),
                        pltpu.VMEM((tm + 8, tf), F32),
                        pltpu.VMEM((f // tf, 8, tf), F32)],
        compiler_params=_params("arbitrary", "arbitrary"),
        name="conv_ffn",
    )(h, g, wg, wu, cw, cb, wd)


def _ple_kernel(h_ref, g_ref, wgate_ref, p_ref, wproj_ref, o_ref, xn_ref, pb_ref):
    j = pl.program_id(1)
    tn = o_ref.shape[1]

    @pl.when(j == 0)
    def _():
        _norm_into(h_ref, g_ref, xn_ref, 256)
        pb_ref[...] = p_ref[...].astype(BF16)

    gate = jax.nn.sigmoid(_dot(xn_ref[...], wgate_ref[...]))
    proj = _dot(pb_ref[...], wproj_ref[...])
    c0 = pl.multiple_of(j * tn, tn)
    o_ref[...] = h_ref[:, pl.ds(c0, tn)] + proj * gate


def ple(h, g, wgate, p, wproj, tm, tn):
    t, d = h.shape
    tm, tn = _tile(t, tm), _tile(d, tn)
    return pl.pallas_call(
        _ple_kernel,
        grid=(t // tm, d // tn),
        in_specs=[pl.BlockSpec((tm, d), lambda i, j: (i, 0)),
                  pl.BlockSpec((1, d), lambda i, j: (0, 0)),
                  pl.BlockSpec((d, tn), lambda i, j: (0, j)),
                  pl.BlockSpec((tm, PLE_DIM), lambda i, j: (i, 0)),
                  pl.BlockSpec((PLE_DIM, tn), lambda i, j: (0, j))],
        out_specs=pl.BlockSpec((tm, tn), lambda i, j: (i, j)),
        out_shape=jax.ShapeDtypeStruct((t, d), F32),
        scratch_shapes=[pltpu.VMEM((tm, d), BF16), pltpu.VMEM((tm, PLE_DIM), BF16)],
        compiler_params=_params("parallel", "arbitrary"),
        name="ple",
    )(h, g, wgate, p, wproj)


def _hgrn_kernel(q_ref, f_ref, i_ref, g_ref, lbl_ref, on_ref, o_ref, st_ref):
    c = HG_CHUNK

    @pl.when(pl.program_id(1) == 0)
    def _():
        st_ref[...] = jnp.zeros(st_ref.shape, F32)

    lg = lbl_ref[...]
    e = jnp.exp(lg - jnp.max(lg, axis=0, keepdims=True))
    lb = e[0:1, :] / jnp.sum(e, axis=0, keepdims=True)
    onorm = on_ref[...]
    row = lax.broadcasted_iota(jnp.int32, (c, c), 0)
    col = lax.broadcasted_iota(jnp.int32, (c, c), 1)
    causal = row >= col
    tril = jnp.where(causal, 1.0, 0.0).astype(BF16)

    def chunk(ci, carry):
        r0 = pl.multiple_of(ci * c, c)
        rows = pl.ds(r0, c)
        for h in range(HG_HEADS):
            cols = slice(h * HG_DK, (h + 1) * HG_DK)
            q = q_ref[rows, cols]
            v = i_ref[rows, cols].astype(BF16)
            lbh = lb[:, cols]
            f = lbh + (1.0 - lbh) * jax.nn.sigmoid(f_ref[rows, cols])
            k = 1.0 - f
            lf = jnp.log(f)
            hi = lf.astype(BF16)
            r1 = lf - hi.astype(F32)
            mid = r1.astype(BF16)
            lo = (r1 - mid.astype(F32)).astype(BF16)
            bc = _dot(tril, hi) + _dot(tril, mid) + _dot(tril, lo)
            ref = bc[c // 2 - 1:c // 2, :]
            b_last = bc[c - 1:c, :]
            q_rel = (q * jnp.exp(bc - ref)).astype(BF16)
            k_rel = (k * jnp.exp(ref - bc)).astype(BF16)
            a = lax.dot_general(q_rel, k_rel, NT, preferred_element_type=F32)
            a = jnp.where(causal, a, 0.0).astype(BF16)
            st = st_ref[h]
            out = _dot(a, v) + lax.dot_general((q * jnp.exp(bc)).astype(BF16), st.astype(BF16), NT,
                                               preferred_element_type=F32)
            k_dec = (k * jnp.exp(b_last - bc)).astype(BF16)
            st_ref[h] = st * jnp.exp(b_last) + lax.dot_general(v, k_dec, TN, preferred_element_type=F32)
            g = g_ref[rows, cols]
            o_ref[rows, cols] = (_rms(out, onorm) * (g * jax.nn.sigmoid(g))).astype(o_ref.dtype)
        return carry

    lax.fori_loop(0, q_ref.shape[0] // c, chunk, 0, unroll=2)


def hgrn(z, lb_logits, onorm, batch, seq, cb):
    t = z.shape[0]
    cb = _tile(seq, cb)
    nblk = seq // cb
    spec = lambda part: pl.BlockSpec((cb, HG_W), lambda b, s, part=part: (b * nblk + s, part))
    return pl.pallas_call(
        _hgrn_kernel,
        grid=(batch, nblk),
        in_specs=[spec(0), spec(1), spec(2), spec(3),
                  pl.BlockSpec(lb_logits.shape, lambda b, s: (0, 0)),
                  pl.BlockSpec((1, HG_DV), lambda b, s: (0, 0))],
        out_specs=pl.BlockSpec((cb, HG_W), lambda b, s: (b * nblk + s, 0)),
        out_shape=jax.ShapeDtypeStruct((t, HG_W), BF16),
        scratch_shapes=[pltpu.VMEM((HG_HEADS, HG_DV, HG_DK), F32)],
        compiler_params=_params("parallel", "arbitrary"),
        name="hgrn2",
    )(z, z, z, z, lb_logits, onorm)


def _mla_prep_kernel(z_ref, qa_ref, kva_ref, wuq_ref, wukv_ref, qn_ref, kn_ref, cos_ref, sin_ref,
                     qt_out, k_out, vt_out):
    cq = _rms(z_ref[:, 0:MLA_Q_RANK], qa_ref[...]).astype(BF16)
    ckv = _rms(z_ref[:, MLA_Q_RANK:MLA_Q_RANK + MLA_KV_RANK], kva_ref[...]).astype(BF16)
    kpe = z_ref[:, MLA_Q_RANK + MLA_KV_RANK:]
    qf = _dot(cq, wuq_ref[...])
    kvf = _dot(ckv, wukv_ref[...])
    cos, sin = cos_ref[...], sin_ref[...]
    qn, kn = qn_ref[...], kn_ref[...]
    scale = MLA_QK ** -0.5 * LOG2E
    tm = z_ref.shape[0]

    def rope(x):
        return x * cos + pltpu.roll(x, 64, 1) * sin

    kpe_ss = jnp.sum(kpe * kpe, axis=-1, keepdims=True)
    k_rope = rope(kpe * kn[:, MLA_NOPE:])
    ones_rows = jnp.where(lax.broadcasted_iota(jnp.int32, (VT_ROWS - MLA_V, tm), 0) == 0, 1.0, 0.0)
    for h in range(MLA_HEADS):
        c0 = h * MLA_PAD
        qh = qf[:, c0:c0 + MLA_PAD]
        rq = lax.rsqrt(jnp.sum(qh * qh, axis=-1, keepdims=True) * (1.0 / MLA_QK) + EPS) * scale
        qh = qh * rq * qn
        qt_out[h, 0:MLA_NOPE, :] = qh[:, :MLA_NOPE].T.astype(BF16)
        qt_out[h, MLA_NOPE:MLA_PAD, :] = rope(qh[:, MLA_NOPE:]).T.astype(BF16)
        kh = kvf[:, c0:c0 + MLA_NOPE]
        rk = lax.rsqrt((jnp.sum(kh * kh, axis=-1, keepdims=True) + kpe_ss) * (1.0 / MLA_QK) + EPS)
        k_out[:, c0:c0 + MLA_NOPE] = (kh * rk * kn[:, :MLA_NOPE]).astype(BF16)
        k_out[:, c0 + MLA_NOPE:c0 + MLA_PAD] = (k_rope * rk).astype(BF16)
        vt_out[h, 0:MLA_V, :] = kvf[:, c0 + MLA_NOPE:c0 + MLA_PAD].T.astype(BF16)
        vt_out[h, MLA_V:VT_ROWS, :] = ones_rows.astype(BF16)


def mla_prep(z, qa, kva, wuq, wukv, qn, kn, cos, sin, tm):
    t, zw = z.shape
    tm = _tile(t, tm)
    full = lambda a: pl.BlockSpec(a.shape, lambda i: (0, 0))
    rows = lambda w: pl.BlockSpec((tm, w), lambda i: (i, 0))
    cols = lambda r: pl.BlockSpec((MLA_HEADS, r, tm), lambda i: (0, 0, i))
    return pl.pallas_call(
        _mla_prep_kernel,
        grid=(t // tm,),
        in_specs=[rows(zw), full(qa), full(kva), full(wuq), full(wukv), full(qn), full(kn),
                  rows(128), rows(128)],
        out_specs=[cols(MLA_PAD), rows(MLA_HEADS * MLA_PAD), cols(VT_ROWS)],
        out_shape=[jax.ShapeDtypeStruct((MLA_HEADS, MLA_PAD, t), BF16),
                   jax.ShapeDtypeStruct((t, MLA_HEADS * MLA_PAD), BF16),
                   jax.ShapeDtypeStruct((MLA_HEADS, VT_ROWS, t), BF16)],
        compiler_params=_params("parallel"),
        name="mla_prep",
    )(z, qa, kva, wuq, wukv, qn, kn, cos, sin)


def _flash_kernel(qt_ref, k_ref, vt_ref, o_ref, acc_ref, *, tq, tk):
    qi = pl.program_id(2)
    qt = qt_ref[0]
    acc_ref[...] = jnp.zeros(acc_ref.shape, F32)

    def step(j, m_all, q_lo, diagonal):
        k0 = pl.multiple_of(j * tk, tk)
        s = _dot(k_ref[pl.ds(k0, tk), :], qt[:, q_lo:])
        if diagonal:
            row = lax.broadcasted_iota(jnp.int32, s.shape, 0)
            col = lax.broadcasted_iota(jnp.int32, s.shape, 1)
            s = jnp.where(row <= col, s, -jnp.inf)
        m_prev = m_all[:, q_lo:]
        m_new = jnp.maximum(m_prev, jnp.max(s, axis=0, keepdims=True))
        alpha = jnp.exp2(m_prev - m_new)
        p = jnp.exp2(s - m_new).astype(BF16)
        acc_ref[:, q_lo:] = alpha * acc_ref[:, q_lo:] + _dot(vt_ref[0, :, pl.ds(k0, tk)], p)
        return m_new if q_lo == 0 else jnp.concatenate([m_all[:, :q_lo], m_new], axis=1)

    n_full = qi * (tq // tk)
    m = lax.fori_loop(0, n_full, lambda j, m: step(j, m, 0, False), jnp.full((1, tq), -jnp.inf, F32))
    for d in range(tq // tk):
        m = step(n_full + d, m, d * tk, True)
    acc = acc_ref[...]
    o_ref[...] = (acc[0:MLA_V, :] / acc[MLA_V:MLA_V + 1, :]).T.astype(o_ref.dtype)


def flash_attention(qt, k, vt, batch, seq, tq, tk):
    t = k.shape[0]
    tq = _tile(seq, tq)
    tk = _tile(tq, tk)
    nq = seq // tq
    return pl.pallas_call(
        functools.partial(_flash_kernel, tq=tq, tk=tk),
        grid=(batch, MLA_HEADS, nq),
        in_specs=[pl.BlockSpec((1, MLA_PAD, tq), lambda b, h, i: (h, 0, b * nq + i)),
                  pl.BlockSpec((seq, MLA_PAD), lambda b, h, i: (b, h)),
                  pl.BlockSpec((1, VT_ROWS, seq), lambda b, h, i: (h, 0, b))],
        out_specs=pl.BlockSpec((tq, MLA_V), lambda b, h, i: (b * nq + i, h)),
        out_shape=jax.ShapeDtypeStruct((t, MLA_HEADS * MLA_V), BF16),
        scratch_shapes=[pltpu.VMEM((VT_ROWS, tq), F32)],
        compiler_params=_params("parallel", "parallel", "arbitrary"),
        name="mla_flash",
    )(qt, k, vt)


def _ret_kernel(q_ref, k_ref, v_ref, g_ref, o_ref, r_ref):
    c = RET_CHUNK
    hf = pl.program_id(1).astype(F32)

    @pl.when(pl.program_id(2) == 0)
    def _():
        r_ref[...] = jnp.zeros(r_ref.shape, F32)

    def log_gamma(shape):
        return jnp.log(1.0 - jnp.exp2(-5.0 - jnp.full(shape, hf, F32)))

    row = lax.broadcasted_iota(jnp.int32, (c, c), 0)
    col = lax.broadcasted_iota(jnp.int32, (c, c), 1)
    diff = (row - col).astype(F32)
    d_intra = jnp.where(diff >= 0, jnp.exp(jnp.where(diff >= 0, diff, 0.0) * log_gamma((c, c))), 0.0)
    idx = lax.broadcasted_iota(jnp.int32, (c, RET_DK), 0).astype(F32)
    lg_k = log_gamma((c, RET_DK))
    q_dec = jnp.exp((idx + 1.0) * lg_k)
    k_dec = jnp.exp((c - 1.0 - idx) * lg_k)
    c_dec = jnp.exp(c * log_gamma((1, RET_DV)))

    for ci in range(q_ref.shape[0] // c):
        rows = slice(ci * c, (ci + 1) * c)
        q, k, v = q_ref[rows, :], k_ref[rows, :], v_ref[rows, :]
        a = lax.dot_general(q, k, NT, preferred_element_type=F32) * d_intra
        r = r_ref[...]
        out = _dot(a.astype(BF16), v) + _dot((q.astype(F32) * q_dec).astype(BF16), r.astype(BF16))
        r_ref[...] = r * c_dec + lax.dot_general((k.astype(F32) * k_dec).astype(BF16), v, TN,
                                                 preferred_element_type=F32)
        g = g_ref[rows, :].astype(F32)
        o_ref[rows, :] = (_rms(out) * (g * jax.nn.sigmoid(g))).astype(o_ref.dtype)


def retention(z, batch, seq, cb):
    t = z.shape[0]
    cb = _tile(seq, cb)
    assert cb % RET_CHUNK == 0
    nblk = seq // cb
    nqk = RET_HEADS * RET_DK // RET_DK
    nv = (2 * RET_HEADS * RET_DK) // RET_DV
    return pl.pallas_call(
        _ret_kernel,
        grid=(batch, RET_HEADS, nblk),
        in_specs=[pl.BlockSpec((cb, RET_DK), lambda b, h, s: (b * nblk + s, h)),
                  pl.BlockSpec((cb, RET_DK), lambda b, h, s: (b * nblk + s, nqk + h)),
                  pl.BlockSpec((cb, RET_DV), lambda b, h, s: (b * nblk + s, nv + h)),
                  pl.BlockSpec((cb, RET_DV), lambda b, h, s: (b * nblk + s, nv + RET_HEADS + h))],
        out_specs=pl.BlockSpec((cb, RET_DV), lambda b, h, s: (b * nblk + s, h)),
        out_shape=jax.ShapeDtypeStruct((t, RET_HEADS * RET_DV), BF16),
        scratch_shapes=[pltpu.VMEM((RET_DK, RET_DV), F32)],
        compiler_params=_params("parallel", "parallel", "arbitrary"),
        name="retention",
    )(z, z, z, z)


def _rope_pad(a, axis):
    x1, x2 = jnp.split(a, 2, axis=axis)
    z = jnp.zeros_like(x1)
    return jnp.concatenate([x1, z, x2, z], axis=axis)


def _pad_qk_gain(g):
    return jnp.concatenate([g[:MLA_NOPE], _rope_pad(g[MLA_NOPE:], 0)])[None, :]


def kernel(x, p, positions, norm_mix, norm_ffn, norm_ple, e_w_in, e_lb_logits, e_q_a_norm, e_kv_a_norm, e_w_uq, e_w_ukv, e_q_norm, e_k_norm, e_hg_onorm, e_w_out, o_w_in, o_w_out, ffn_w_gate, ffn_w_up, ffn_conv_w, ffn_conv_b, ffn_w_down, ple_w_proj, ple_w_gate):
    batch, seq, d = x.shape
    t = batch * seq
    h = x.reshape(t, d)
    pos_col = positions.reshape(t, 1)

    ones = jnp.ones((1, 128), F32)
    f_ret = (ROPE_BASE ** (-jnp.arange(RET_DK // 2, dtype=F32) / (RET_DK // 2)))[None, :]
    f_mla = ROPE_BASE ** (-jnp.arange(MLA_ROPE // 2, dtype=F32) / (MLA_ROPE // 2))
    f_mla = _rope_pad(jnp.concatenate([f_mla, f_mla]), 0)[None, :]
    sgn_mla = _rope_pad(jnp.concatenate([-jnp.ones(32, F32), jnp.ones(32, F32)]), 0)[None, :]
    cos_ret, sin_ret = rope_tables(pos_col, f_ret, ones, "rope_tab_ret")
    cos_mla, sin_mla = rope_tables(pos_col, f_mla, sgn_mla, "rope_tab_mla")

    w_in = e_w_in[0]
    w_hg = w_in[:, :4 * HG_W].astype(BF16)
    off = 4 * HG_W + MLA_Q_RANK + MLA_KV_RANK
    w_mla = jnp.concatenate([w_in[:, 4 * HG_W:off], _rope_pad(w_in[:, off:], 1)], axis=1).astype(BF16)
    wuq = e_w_uq[0].reshape(MLA_Q_RANK, MLA_HEADS, MLA_QK)
    wuq = jnp.concatenate([wuq[..., :MLA_NOPE], _rope_pad(wuq[..., MLA_NOPE:], 2)], axis=-1)
    wuq = wuq.reshape(MLA_Q_RANK, MLA_HEADS * MLA_PAD).astype(BF16)
    wukv = e_w_ukv[0].astype(BF16)
    g_mix0 = norm_mix[0][None, :]

    z_hg = norm_matmul(h, g_mix0, w_hg, F32, 1024, 1024, "in_proj_hgrn")
    z_mla = norm_matmul(h, g_mix0, w_mla, F32, 1024, w_mla.shape[1], "in_proj_mla")
    o_a = hgrn(z_hg, e_lb_logits, e_hg_onorm[0][None, :], batch, seq, 512)
    qt, k, vt = mla_prep(z_mla, e_q_a_norm[0][None, :], e_kv_a_norm[0][None, :], wuq, wukv,
                       _pad_qk_gain(e_q_norm[0]), _pad_qk_gain(e_k_norm[0]), cos_mla, sin_mla, 512)
    o_b = flash_attention(qt, k, vt, batch, seq, 2048, 512)
    w_out = e_w_out[0].astype(BF16)
    h = res_matmul(h, [o_a, o_b], [w_out[:HG_W], w_out[HG_W:]], 1024, 1024, "out_proj_even")

    def channel_mix(h, i):
        h = conv_ffn(h, norm_ffn[i][None, :], ffn_w_gate[i].astype(BF16), ffn_w_up[i].astype(BF16),
                     ffn_conv_w[i], ffn_conv_b[i][None, :], ffn_w_down[i].astype(BF16), seq, 512, 512)
        return ple(h, norm_ple[i][None, :], ple_w_gate[i].astype(BF16), p[i].reshape(t, PLE_DIM),
                   ple_w_proj[i].astype(BF16), 1024, 1024)

    h = channel_mix(h, 0)

    z_ret = ret_in_proj(h, norm_mix[1][None, :], o_w_in[0].astype(BF16), cos_ret, sin_ret, 1024, 1024)
    o_r = retention(z_ret, batch, seq, 512)
    h = res_matmul(h, [o_r], [o_w_out[0].astype(BF16)], 1024, 1024, "out_proj_odd")
    h = channel_mix(h, 1)
    return h.reshape(batch, seq, d)
```

```python
import functools

import jax
import jax.numpy as jnp
from jax import lax
from jax.experimental import pallas as pl
from jax.experimental.pallas import tpu as pltpu

F32 = jnp.float32
BF16 = jnp.bfloat16

D_MODEL = 2048
PLE_DIM = 256
HG_HEADS = 8
HG_DK = 128
HG_DV = 128
HG_W = HG_HEADS * HG_DK
HG_CHUNK = 64
MLA_HEADS = 8
MLA_Q_RANK = 512
MLA_KV_RANK = 512
MLA_NOPE = 128
MLA_ROPE = 64
MLA_V = 128
MLA_QK = MLA_NOPE + MLA_ROPE
MLA_PAD = 256
VT_ROWS = MLA_V + 16
LOG2E = 1.4426950408889634
RET_HEADS = 8
RET_DK = 256
RET_DV = 512
RET_CHUNK = 256
D_FF = 5632
FFN_STRIP = 256
ROPE_BASE = 10000.0
EPS = 1e-6

VMEM_LIMIT = 56 * 2**20

NT = (((1,), (1,)), ((), ()))
TN = (((0,), (0,)), ((), ()))


def _params(*sem):
    return pltpu.CompilerParams(dimension_semantics=sem, vmem_limit_bytes=VMEM_LIMIT)


def _tile(n, pref):
    t = min(n, pref)
    assert n % t == 0, (n, pref)
    return t


def _dot(a, b):
    return jnp.dot(a, b, preferred_element_type=F32)


def _rms(x, g=None, n=None):
    n = x.shape[-1] if n is None else n
    y = x * lax.rsqrt(jnp.sum(x * x, axis=-1, keepdims=True) * (1.0 / n) + EPS)
    return y if g is None else y * g


def _norm_into(x_ref, g_ref, dst_ref, rows):
    g = g_ref[...]

    def body(c, carry):
        r0 = pl.multiple_of(c * rows, rows)
        dst_ref[pl.ds(r0, rows), :] = _rms(x_ref[pl.ds(r0, rows), :], g).astype(dst_ref.dtype)
        return carry

    lax.fori_loop(0, x_ref.shape[0] // rows, body, 0)


def _rope_tab_kernel(pos_ref, f_ref, sgn_ref, c_ref, s_ref):
    ang = pos_ref[...].astype(F32) * f_ref[...]
    c_ref[...] = jnp.cos(ang)
    s_ref[...] = jnp.sin(ang) * sgn_ref[...]


def rope_tables(pos_col, freq, sgn, name):
    t = pos_col.shape[0]
    tm = _tile(t, 1024)
    return pl.pallas_call(
        _rope_tab_kernel,
        grid=(t // tm,),
        in_specs=[pl.BlockSpec((tm, 1), lambda i: (i, 0)),
                  pl.BlockSpec((1, 128), lambda i: (0, 0)),
                  pl.BlockSpec((1, 128), lambda i: (0, 0))],
        out_specs=[pl.BlockSpec((tm, 128), lambda i: (i, 0)),
                   pl.BlockSpec((tm, 128), lambda i: (i, 0))],
        out_shape=[jax.ShapeDtypeStruct((t, 128), F32)] * 2,
        compiler_params=_params("parallel"),
        name=name,
    )(pos_col, freq, sgn)


def _norm_matmul_kernel(x_ref, g_ref, w_ref, o_ref, xn_ref):
    @pl.when(pl.program_id(1) == 0)
    def _():
        _norm_into(x_ref, g_ref, xn_ref, 256)

    o_ref[...] = _dot(xn_ref[...], w_ref[...]).astype(o_ref.dtype)


def norm_matmul(x, g, w, out_dtype, tm, tn, name):
    t, k = x.shape
    n = w.shape[1]
    tm, tn = _tile(t, tm), _tile(n, tn)
    return pl.pallas_call(
        _norm_matmul_kernel,
        grid=(t // tm, n // tn),
        in_specs=[pl.BlockSpec((tm, k), lambda i, j: (i, 0)),
                  pl.BlockSpec((1, k), lambda i, j: (0, 0)),
                  pl.BlockSpec((k, tn), lambda i, j: (0, j))],
        out_specs=pl.BlockSpec((tm, tn), lambda i, j: (i, j)),
        out_shape=jax.ShapeDtypeStruct((t, n), out_dtype),
        scratch_shapes=[pltpu.VMEM((tm, k), BF16)],
        compiler_params=_params("parallel", "arbitrary"),
        name=name,
    )(x, g, w)


def _ret_in_kernel(x_ref, g_ref, w_ref, cos_ref, sin_ref, o_ref, xn_ref, *, n_q_tiles, n_rope_tiles):
    j = pl.program_id(1)

    @pl.when(j == 0)
    def _():
        _norm_into(x_ref, g_ref, xn_ref, 256)

    acc = _dot(xn_ref[...], w_ref[...])
    is_rope = j < n_rope_tiles
    scale = jnp.where(j >= n_q_tiles, RET_DK ** -0.5, 1.0).astype(F32)
    cos = cos_ref[...] * scale
    sin = sin_ref[...] * scale
    half = RET_DK // 2
    for hh in range(acc.shape[1] // RET_DK):
        x1 = acc[:, hh * RET_DK:hh * RET_DK + half]
        x2 = acc[:, hh * RET_DK + half:(hh + 1) * RET_DK]
        o_ref[:, hh * RET_DK:hh * RET_DK + half] = jnp.where(
            is_rope, x1 * cos - x2 * sin, x1).astype(o_ref.dtype)
        o_ref[:, hh * RET_DK + half:(hh + 1) * RET_DK] = jnp.where(
            is_rope, x2 * cos + x1 * sin, x2).astype(o_ref.dtype)


def ret_in_proj(x, g, w, cos, sin, tm, tn):
    t, k = x.shape
    n = w.shape[1]
    tm, tn = _tile(t, tm), _tile(n, tn)
    qw = RET_HEADS * RET_DK
    kern = functools.partial(_ret_in_kernel, n_q_tiles=qw // tn, n_rope_tiles=2 * qw // tn)
    return pl.pallas_call(
        kern,
        grid=(t // tm, n // tn),
        in_specs=[pl.BlockSpec((tm, k), lambda i, j: (i, 0)),
                  pl.BlockSpec((1, k), lambda i, j: (0, 0)),
                  pl.BlockSpec((k, tn), lambda i, j: (0, j)),
                  pl.BlockSpec((tm, 128), lambda i, j: (i, 0)),
                  pl.BlockSpec((tm, 128), lambda i, j: (i, 0))],
        out_specs=pl.BlockSpec((tm, tn), lambda i, j: (i, j)),
        out_shape=jax.ShapeDtypeStruct((t, n), BF16),
        scratch_shapes=[pltpu.VMEM((tm, k), BF16)],
        compiler_params=_params("parallel", "arbitrary"),
        name="ret_in_proj",
    )(x, g, w, cos, sin)


def _res_matmul_kernel(*refs, n_lhs):
    h_ref, o_ref = refs[0], refs[-1]
    acc = h_ref[...]
    for a_ref, w_ref in zip(refs[1:1 + n_lhs], refs[1 + n_lhs:1 + 2 * n_lhs]):
        acc = acc + _dot(a_ref[...], w_ref[...])
    o_ref[...] = acc


def res_matmul(h, lhs, ws, tm, tn, name):
    t, n = h.shape
    tm, tn = _tile(t, tm), _tile(n, tn)
    in_specs = [pl.BlockSpec((tm, tn), lambda i, j: (i, j))]
    in_specs += [pl.BlockSpec((tm, a.shape[1]), lambda i, j: (i, 0)) for a in lhs]
    in_specs += [pl.BlockSpec((w.shape[0], tn), lambda i, j: (0, j)) for w in ws]
    return pl.pallas_call(
        functools.partial(_res_matmul_kernel, n_lhs=len(lhs)),
        grid=(t // tm, n // tn),
        in_specs=in_specs,
        out_specs=pl.BlockSpec((tm, tn), lambda i, j: (i, j)),
        out_shape=jax.ShapeDtypeStruct((t, n), F32),
        compiler_params=_params("parallel", "arbitrary"),
        name=name,
    )(h, *lhs, *ws)


def _ffn_kernel(h_ref, g_ref, wg_ref, wu_ref, cw_ref, cb_ref, wd_ref, o_ref,
                u_ref, gs_ref, tail_ref, *, blocks_per_seq):
    i, j = pl.program_id(0), pl.program_id(1)
    tm = h_ref.shape[0]

    @pl.when(j == 0)
    def _():
        _norm_into(h_ref, g_ref, u_ref, 256)
        o_ref[...] = h_ref[...]

    @pl.when(jnp.logical_and(i == 0, j == 0))
    def _():
        tail_ref[...] = jnp.zeros(tail_ref.shape, F32)

    u = u_ref[...]
    seq_start = (i % blocks_per_seq) == 0
    prev = jnp.where(seq_start, 0.0, tail_ref[j])
    cw, cb = cw_ref[...], cb_ref[...]
    contrib = None
    for c0 in range(0, wg_ref.shape[1], FFN_STRIP):
        cols = slice(c0, c0 + FFN_STRIP)
        a = _dot(u, wg_ref[:, cols])
        up = _dot(u, wu_ref[:, cols])
        gs_ref[0:8, cols] = prev[:, cols]
        gs_ref[8:8 + tm, cols] = a
        tail_ref[j, :, cols] = a[tm - 8:tm, :]
        c = (cb[:, cols] + cw[0:1, cols] * gs_ref[6:6 + tm, cols] + cw[1:2, cols] * gs_ref[7:7 + tm, cols]
             + cw[2:3, cols] * a)
        act = (c * jax.nn.sigmoid(c) * up).astype(BF16)
        part = _dot(act, wd_ref[cols, :])
        contrib = part if contrib is None else contrib + part
    o_ref[...] += contrib


def conv_ffn(h, g, wg, wu, cw, cb, wd, seq, tm, tf):
    t, d = h.shape
    f = wg.shape[1]
    tm, tf = _tile(seq, tm), _tile(f, tf)
    kern = functools.partial(_ffn_kernel, blocks_per_seq=seq // tm)
    return pl.pallas_call(
        kern,
        grid=(t // tm, f // tf),
        in_specs=[pl.BlockSpec((tm, d), lambda i, j: (i, 0), pipeline_mode=pl.Buffered(1)),
                  pl.BlockSpec((1, d), lambda i, j: (0, 0)),
                  pl.BlockSpec((d, tf), lambda i, j: (0, j)),
                  pl.BlockSpec((d, tf), lambda i, j: (0, j)),
                  pl.BlockSpec((3, tf), lambda i, j: (0, j)),
                  pl.BlockSpec((1, tf), lambda i, j: (0, j)),
                  pl.BlockSpec((tf, d), lambda i, j: (j, 0))],
        out_specs=pl.BlockSpec((tm, d), lambda i, j: (i, 0)),
        out_shape=jax.ShapeDtypeStruct((t, d), F32),
        scratch_shapes=[pltpu.VMEM((tm, d), BF16),
                        pltpu.VMEM((tm + 8, tf), F32),
                        pltpu.VMEM((f // tf, 8, tf), F32)],
        compiler_params=_params("arbitrary", "arbitrary"),
        name="conv_ffn",
    )(h, g, wg, wu, cw, cb, wd)


def _ple_kernel(h_ref, g_ref, wgate_ref, p_ref, wproj_ref, o_ref, xn_ref, pb_ref):
    j = pl.program_id(1)
    tn = o_ref.shape[1]

    @pl.when(j == 0)
    def _():
        _norm_into(h_ref, g_ref, xn_ref, 256)
        pb_ref[...] = p_ref[...].astype(BF16)

    gate = jax.nn.sigmoid(_dot(xn_ref[...], wgate_ref[...]))
    proj = _dot(pb_ref[...], wproj_ref[...])
    c0 = pl.multiple_of(j * tn, tn)
    o_ref[...] = h_ref[:, pl.ds(c0, tn)] + proj * gate


def ple(h, g, wgate, p, wproj, tm, tn):
    t, d = h.shape
    tm, tn = _tile(t, tm), _tile(d, tn)
    return pl.pallas_call(
        _ple_kernel,
        grid=(t // tm, d // tn),
        in_specs=[pl.BlockSpec((tm, d), lambda i, j: (i, 0)),
                  pl.BlockSpec((1, d), lambda i, j: (0, 0)),
                  pl.BlockSpec((d, tn), lambda i, j: (0, j)),
                  pl.BlockSpec((tm, PLE_DIM), lambda i, j: (i, 0)),
                  pl.BlockSpec((PLE_DIM, tn), lambda i, j: (0, j))],
        out_specs=pl.BlockSpec((tm, tn), lambda i, j: (i, j)),
        out_shape=jax.ShapeDtypeStruct((t, d), F32),
        scratch_shapes=[pltpu.VMEM((tm, d), BF16), pltpu.VMEM((tm, PLE_DIM), BF16)],
        compiler_params=_params("parallel", "arbitrary"),
        name="ple",
    )(h, g, wgate, p, wproj)


def _hgrn_kernel(q_ref, f_ref, i_ref, g_ref, lbl_ref, on_ref, o_ref, st_ref):
    c = HG_CHUNK

    @pl.when(pl.program_id(1) == 0)
    def _():
        st_ref[...] = jnp.zeros(st_ref.shape, F32)

    lg = lbl_ref[...]
    e = jnp.exp(lg - jnp.max(lg, axis=0, keepdims=True))
    lb = e[0:1, :] / jnp.sum(e, axis=0, keepdims=True)
    onorm = on_ref[...]
    pw = 2 * HG_DK
    row = lax.broadcasted_iota(jnp.int32, (c, c), 0)
    col = lax.broadcasted_iota(jnp.int32, (c, c), 1)
    tril = jnp.where(row >= col, 1.0, 0.0).astype(BF16)
    row2 = lax.broadcasted_iota(jnp.int32, (c, 2 * c), 0)
    col2 = lax.broadcasted_iota(jnp.int32, (c, 2 * c), 1)
    causal2 = row2 >= jnp.where(col2 >= c, col2 - c, col2)
    left = lax.broadcasted_iota(jnp.int32, (c, pw), 1) < HG_DK
    same_head = ((lax.broadcasted_iota(jnp.int32, (pw, pw), 0) < HG_DV)
                 == (lax.broadcasted_iota(jnp.int32, (pw, pw), 1) < HG_DK))

    def block_diag(x):
        zero = jnp.zeros_like(x)
        return jnp.concatenate([jnp.where(left, x, zero), jnp.where(left, zero, x)], axis=0)

    def chunk(ci, carry):
        rows = pl.ds(pl.multiple_of(ci * c, c), c)
        q = q_ref[rows, :]
        v = i_ref[rows, :].astype(BF16)
        f = lb + (1.0 - lb) * jax.nn.sigmoid(f_ref[rows, :])
        k = 1.0 - f
        lf = jnp.log(f)
        hi = lf.astype(BF16)
        r1 = lf - hi.astype(F32)
        mid = r1.astype(BF16)
        lo = (r1 - mid.astype(F32)).astype(BF16)
        bc = _dot(tril, hi) + _dot(tril, mid) + _dot(tril, lo)
        ref = bc[c // 2 - 1:c // 2, :]
        b_last = bc[c - 1:c, :]
        q_rel = (q * jnp.exp(bc - ref)).astype(BF16)
        k_rel = (k * jnp.exp(ref - bc)).astype(BF16)
        q_dec = (q * jnp.exp(bc)).astype(BF16)
        k_dec = (k * jnp.exp(b_last - bc)).astype(BF16)
        decay = jnp.exp(b_last)
        for p in range(HG_HEADS // 2):
            cols = slice(p * pw, (p + 1) * pw)
            a = lax.dot_general(q_rel[:, cols], block_diag(k_rel[:, cols]), NT, preferred_element_type=F32)
            a = jnp.where(causal2, a, 0.0).astype(BF16)
            st = st_ref[p]
            out = _dot(a, block_diag(v[:, cols])) + lax.dot_general(q_dec[:, cols], st.astype(BF16), NT,
                                                                    preferred_element_type=F32)
            upd = lax.dot_general(v[:, cols], k_dec[:, cols], TN, preferred_element_type=F32)
            st_ref[p] = st * decay[:, cols] + jnp.where(same_head, upd, 0.0)
            for hh in range(2):
                hc = slice(p * pw + hh * HG_DV, p * pw + (hh + 1) * HG_DV)
                g = g_ref[rows, hc]
                o_ref[rows, hc] = (_rms(out[:, hh * HG_DV:(hh + 1) * HG_DV], onorm)
                                   * (g * jax.nn.sigmoid(g))).astype(o_ref.dtype)
        return carry

    lax.fori_loop(0, q_ref.shape[0] // c, chunk, 0, unroll=2)


def hgrn(z, lb_logits, onorm, batch, seq, cb):
    t = z.shape[0]
    cb = _tile(seq, cb)
    nblk = seq // cb
    spec = lambda part: pl.BlockSpec((cb, HG_W), lambda b, s, part=part: (b * nblk + s, part))
    return pl.pallas_call(
        _hgrn_kernel,
        grid=(batch, nblk),
        in_specs=[spec(0), spec(1), spec(2), spec(3),
                  pl.BlockSpec(lb_logits.shape, lambda b, s: (0, 0)),
                  pl.BlockSpec((1, HG_DV), lambda b, s: (0, 0))],
        out_specs=pl.BlockSpec((cb, HG_W), lambda b, s: (b * nblk + s, 0)),
        out_shape=jax.ShapeDtypeStruct((t, HG_W), BF16),
        scratch_shapes=[pltpu.VMEM((HG_HEADS // 2, 2 * HG_DV, 2 * HG_DK), F32)],
        compiler_params=_params("parallel", "arbitrary"),
        name="hgrn2",
    )(z, z, z, z, lb_logits, onorm)


def _mla_prep_kernel(z_ref, qa_ref, kva_ref, wuq_ref, wukv_ref, qn_ref, kn_ref, cos_ref, sin_ref,
                     qt_out, k_out, vt_out):
    cq = _rms(z_ref[:, 0:MLA_Q_RANK], qa_ref[...]).astype(BF16)
    ckv = _rms(z_ref[:, MLA_Q_RANK:MLA_Q_RANK + MLA_KV_RANK], kva_ref[...]).astype(BF16)
    kpe = z_ref[:, MLA_Q_RANK + MLA_KV_RANK:]
    qf = _dot(cq, wuq_ref[...])
    kvf = _dot(ckv, wukv_ref[...])
    cos, sin = cos_ref[...], sin_ref[...]
    qn, kn = qn_ref[...], kn_ref[...]
    scale = MLA_QK ** -0.5 * LOG2E
    tm = z_ref.shape[0]

    def rope(x):
        return x * cos + pltpu.roll(x, 64, 1) * sin

    kpe_ss = jnp.sum(kpe * kpe, axis=-1, keepdims=True)
    k_rope = rope(kpe * kn[:, MLA_NOPE:])
    ones_rows = jnp.where(lax.broadcasted_iota(jnp.int32, (VT_ROWS - MLA_V, tm), 0) == 0, 1.0, 0.0)
    for h in range(MLA_HEADS):
        c0 = h * MLA_PAD
        qh = qf[:, c0:c0 + MLA_PAD]
        rq = lax.rsqrt(jnp.sum(qh * qh, axis=-1, keepdims=True) * (1.0 / MLA_QK) + EPS) * scale
        qh = qh * rq * qn
        qt_out[h, 0:MLA_NOPE, :] = qh[:, :MLA_NOPE].T.astype(BF16)
        qt_out[h, MLA_NOPE:MLA_PAD, :] = rope(qh[:, MLA_NOPE:]).T.astype(BF16)
        kh = kvf[:, c0:c0 + MLA_NOPE]
        rk = lax.rsqrt((jnp.sum(kh * kh, axis=-1, keepdims=True) + kpe_ss) * (1.0 / MLA_QK) + EPS)
        k_out[:, c0:c0 + MLA_NOPE] = (kh * rk * kn[:, :MLA_NOPE]).astype(BF16)
        k_out[:, c0 + MLA_NOPE:c0 + MLA_PAD] = (k_rope * rk).astype(BF16)
        vt_out[h, 0:MLA_V, :] = kvf[:, c0 + MLA_NOPE:c0 + MLA_PAD].T.astype(BF16)
        vt_out[h, MLA_V:VT_ROWS, :] = ones_rows.astype(BF16)


def mla_prep(z, qa, kva, wuq, wukv, qn, kn, cos, sin, tm):
    t, zw = z.shape
    tm = _tile(t, tm)
    full = lambda a: pl.BlockSpec(a.shape, lambda i: (0, 0))
    rows = lambda w: pl.BlockSpec((tm, w), lambda i: (i, 0))
    cols = lambda r: pl.BlockSpec((MLA_HEADS, r, tm), lambda i: (0, 0, i))
    return pl.pallas_call(
        _mla_prep_kernel,
        grid=(t // tm,),
        in_specs=[rows(zw), full(qa), full(kva), full(wuq), full(wukv), full(qn), full(kn),
                  rows(128), rows(128)],
        out_specs=[cols(MLA_PAD), rows(MLA_HEADS * MLA_PAD), cols(VT_ROWS)],
        out_shape=[jax.ShapeDtypeStruct((MLA_HEADS, MLA_PAD, t), BF16),
                   jax.ShapeDtypeStruct((t, MLA_HEADS * MLA_PAD), BF16),
                   jax.ShapeDtypeStruct((MLA_HEADS, VT_ROWS, t), BF16)],
        compiler_params=_params("parallel"),
        name="mla_prep",
    )(z, qa, kva, wuq, wukv, qn, kn, cos, sin)


def _flash_kernel(qt_ref, k_ref, vt_ref, o_ref, acc_ref, *, tq, tk):
    qi = pl.program_id(2)
    qt = qt_ref[0]
    acc_ref[...] = jnp.zeros(acc_ref.shape, F32)

    def step(j, m_all, q_lo, diagonal):
        k0 = pl.multiple_of(j * tk, tk)
        s = _dot(k_ref[pl.ds(k0, tk), :], qt[:, q_lo:])
        if diagonal:
            row = lax.broadcasted_iota(jnp.int32, s.shape, 0)
            col = lax.broadcasted_iota(jnp.int32, s.shape, 1)
            s = jnp.where(row <= col, s, -jnp.inf)
        m_prev = m_all[:, q_lo:]
        m_new = jnp.maximum(m_prev, jnp.max(s, axis=0, keepdims=True))
        alpha = jnp.exp2(m_prev - m_new)
        p = jnp.exp2(s - m_new).astype(BF16)
        acc_ref[:, q_lo:] = alpha * acc_ref[:, q_lo:] + _dot(vt_ref[0, :, pl.ds(k0, tk)], p)
        return m_new if q_lo == 0 else jnp.concatenate([m_all[:, :q_lo], m_new], axis=1)

    n_full = qi * (tq // tk)
    m = lax.fori_loop(0, n_full, lambda j, m: step(j, m, 0, False), jnp.full((1, tq), -jnp.inf, F32))
    for d in range(tq // tk):
        m = step(n_full + d, m, d * tk, True)
    acc = acc_ref[...]
    o_ref[...] = (acc[0:MLA_V, :] / acc[MLA_V:MLA_V + 1, :]).T.astype(o_ref.dtype)


def flash_attention(qt, k, vt, batch, seq, tq, tk):
    t = k.shape[0]
    tq = _tile(seq, tq)
    tk = _tile(tq, tk)
    nq = seq // tq
    return pl.pallas_call(
        functools.partial(_flash_kernel, tq=tq, tk=tk),
        grid=(batch, MLA_HEADS, nq),
        in_specs=[pl.BlockSpec((1, MLA_PAD, tq), lambda b, h, i: (h, 0, b * nq + i)),
                  pl.BlockSpec((seq, MLA_PAD), lambda b, h, i: (b, h)),
                  pl.BlockSpec((1, VT_ROWS, seq), lambda b, h, i: (h, 0, b))],
        out_specs=pl.BlockSpec((tq, MLA_V), lambda b, h, i: (b * nq + i, h)),
        out_shape=jax.ShapeDtypeStruct((t, MLA_HEADS * MLA_V), BF16),
        scratch_shapes=[pltpu.VMEM((VT_ROWS, tq), F32)],
        compiler_params=_params("parallel", "parallel", "arbitrary"),
        name="mla_flash",
    )(qt, k, vt)


def _ret_kernel(q_ref, k_ref, v_ref, g_ref, o_ref, r_ref):
    c = RET_CHUNK
    hf = pl.program_id(1).astype(F32)

    @pl.when(pl.program_id(2) == 0)
    def _():
        r_ref[...] = jnp.zeros(r_ref.shape, F32)

    def log_gamma(shape):
        return jnp.log(1.0 - jnp.exp2(-5.0 - jnp.full(shape, hf, F32)))

    row = lax.broadcasted_iota(jnp.int32, (c, c), 0)
    col = lax.broadcasted_iota(jnp.int32, (c, c), 1)
    diff = (row - col).astype(F32)
    d_intra = jnp.where(diff >= 0, jnp.exp(jnp.where(diff >= 0, diff, 0.0) * log_gamma((c, c))), 0.0)
    idx = lax.broadcasted_iota(jnp.int32, (c, RET_DK), 0).astype(F32)
    lg_k = log_gamma((c, RET_DK))
    q_dec = jnp.exp((idx + 1.0) * lg_k).astype(BF16)
    k_dec = jnp.exp((c - 1.0 - idx) * lg_k).astype(BF16)
    c_dec = jnp.exp(c * log_gamma((1, RET_DV)))

    def chunk(ci, carry):
        rows = pl.ds(pl.multiple_of(ci * c, c), c)
        q, k, v = q_ref[rows, :], k_ref[rows, :], v_ref[rows, :]
        a = lax.dot_general(q, k, NT, preferred_element_type=F32) * d_intra
        r = r_ref[...]
        out = _dot(a.astype(BF16), v) + _dot(q * q_dec, r.astype(BF16))
        r_ref[...] = r * c_dec + lax.dot_general(k * k_dec, v, TN, preferred_element_type=F32)
        g = g_ref[rows, :].astype(F32)
        o_ref[rows, :] = (_rms(out) * (g * jax.nn.sigmoid(g))).astype(o_ref.dtype)
        return carry

    lax.fori_loop(0, q_ref.shape[0] // c, chunk, 0, unroll=2)


def retention(z, batch, seq, cb):
    t = z.shape[0]
    cb = _tile(seq, cb)
    assert cb % RET_CHUNK == 0
    nblk = seq // cb
    nqk = RET_HEADS * RET_DK // RET_DK
    nv = (2 * RET_HEADS * RET_DK) // RET_DV
    return pl.pallas_call(
        _ret_kernel,
        grid=(batch, RET_HEADS, nblk),
        in_specs=[pl.BlockSpec((cb, RET_DK), lambda b, h, s: (b * nblk + s, h)),
                  pl.BlockSpec((cb, RET_DK), lambda b, h, s: (b * nblk + s, nqk + h)),
                  pl.BlockSpec((cb, RET_DV), lambda b, h, s: (b * nblk + s, nv + h)),
                  pl.BlockSpec((cb, RET_DV), lambda b, h, s: (b * nblk + s, nv + RET_HEADS + h))],
        out_specs=pl.BlockSpec((cb, RET_DV), lambda b, h, s: (b * nblk + s, h)),
        out_shape=jax.ShapeDtypeStruct((t, RET_HEADS * RET_DV), BF16),
        scratch_shapes=[pltpu.VMEM((RET_DK, RET_DV), F32)],
        compiler_params=_params("parallel", "parallel", "arbitrary"),
        name="retention",
    )(z, z, z, z)


def _rope_pad(a, axis):
    x1, x2 = jnp.split(a, 2, axis=axis)
    z = jnp.zeros_like(x1)
    return jnp.concatenate([x1, z, x2, z], axis=axis)


def _pad_qk_gain(g):
    return jnp.concatenate([g[:MLA_NOPE], _rope_pad(g[MLA_NOPE:], 0)])[None, :]


def kernel(x, p, positions, norm_mix, norm_ffn, norm_ple, e_w_in, e_lb_logits, e_q_a_norm, e_kv_a_norm, e_w_uq, e_w_ukv, e_q_norm, e_k_norm, e_hg_onorm, e_w_out, o_w_in, o_w_out, ffn_w_gate, ffn_w_up, ffn_conv_w, ffn_conv_b, ffn_w_down, ple_w_proj, ple_w_gate):
    batch, seq, d = x.shape
    t = batch * seq
    h = x.reshape(t, d)
    pos_col = positions.reshape(t, 1)

    ones = jnp.ones((1, 128), F32)
    f_ret = (ROPE_BASE ** (-jnp.arange(RET_DK // 2, dtype=F32) / (RET_DK // 2)))[None, :]
    f_mla = ROPE_BASE ** (-jnp.arange(MLA_ROPE // 2, dtype=F32) / (MLA_ROPE // 2))
    f_mla = _rope_pad(jnp.concatenate([f_mla, f_mla]), 0)[None, :]
    sgn_mla = _rope_pad(jnp.concatenate([-jnp.ones(32, F32), jnp.ones(32, F32)]), 0)[None, :]
    cos_ret, sin_ret = rope_tables(pos_col, f_ret, ones, "rope_tab_ret")
    cos_mla, sin_mla = rope_tables(pos_col, f_mla, sgn_mla, "rope_tab_mla")

    w_in = e_w_in[0]
    w_hg = w_in[:, :4 * HG_W].astype(BF16)
    off = 4 * HG_W + MLA_Q_RANK + MLA_KV_RANK
    w_mla = jnp.concatenate([w_in[:, 4 * HG_W:off], _rope_pad(w_in[:, off:], 1)], axis=1).astype(BF16)
    wuq = e_w_uq[0].reshape(MLA_Q_RANK, MLA_HEADS, MLA_QK)
    wuq = jnp.concatenate([wuq[..., :MLA_NOPE], _rope_pad(wuq[..., MLA_NOPE:], 2)], axis=-1)
    wuq = wuq.reshape(MLA_Q_RANK, MLA_HEADS * MLA_PAD).astype(BF16)
    wukv = e_w_ukv[0].astype(BF16)
    g_mix0 = norm_mix[0][None, :]

    z_hg = norm_matmul(h, g_mix0, w_hg, F32, 1024, 1024, "in_proj_hgrn")
    z_mla = norm_matmul(h, g_mix0, w_mla, F32, 1024, w_mla.shape[1], "in_proj_mla")
    o_a = hgrn(z_hg, e_lb_logits, e_hg_onorm[0][None, :], batch, seq, 512)
    qt, k, vt = mla_prep(z_mla, e_q_a_norm[0][None, :], e_kv_a_norm[0][None, :], wuq, wukv,
                       _pad_qk_gain(e_q_norm[0]), _pad_qk_gain(e_k_norm[0]), cos_mla, sin_mla, 512)
    o_b = flash_attention(qt, k, vt, batch, seq, 2048, 512)
    w_out = e_w_out[0].astype(BF16)
    h = res_matmul(h, [o_a, o_b], [w_out[:HG_W], w_out[HG_W:]], 1024, 1024, "out_proj_even")

    def channel_mix(h, i):
        h = conv_ffn(h, norm_ffn[i][None, :], ffn_w_gate[i].astype(BF16), ffn_w_up[i].astype(BF16),
                     ffn_conv_w[i], ffn_conv_b[i][None, :], ffn_w_down[i].astype(BF16), seq, 1024, 512)
        return ple(h, norm_ple[i][None, :], ple_w_gate[i].astype(BF16), p[i].reshape(t, PLE_DIM),
                   ple_w_proj[i].astype(BF16), 1024, 1024)

    h = channel_mix(h, 0)

    z_ret = ret_in_proj(h, norm_mix[1][None, :], o_w_in[0].astype(BF16), cos_ret, sin_ret, 1024, 1024)
    o_r = retention(z_ret, batch, seq, 2048)
    h = res_matmul(h, [o_r], [o_w_out[0].astype(BF16)], 1024, 1024, "out_proj_odd")
    h = channel_mix(h, 1)
    return h.reshape(batch, seq, d)
```

```python
import functools

import jax
import jax.numpy as jnp
from jax import lax
from jax.experimental import pallas as pl
from jax.experimental.pallas import tpu as pltpu

F32 = jnp.float32
BF16 = jnp.bfloat16

D_MODEL = 2048
PLE_DIM = 256
HG_HEADS = 8
HG_DK = 128
HG_DV = 128
HG_W = HG_HEADS * HG_DK
HG_CHUNK = 64
MLA_HEADS = 8
MLA_Q_RANK = 512
MLA_KV_RANK = 512
MLA_NOPE = 128
MLA_ROPE = 64
MLA_V = 128
MLA_QK = MLA_NOPE + MLA_ROPE
MLA_PAD = 256
VT_ROWS = MLA_V + 16
LOG2E = 1.4426950408889634
RET_HEADS = 8
RET_DK = 256
RET_DV = 512
RET_CHUNK = 256
D_FF = 5632
FFN_STRIP = 256
ROPE_BASE = 10000.0
EPS = 1e-6

VMEM_LIMIT = 56 * 2**20

NT = (((1,), (1,)), ((), ()))
TN = (((0,), (0,)), ((), ()))


def _params(*sem):
    return pltpu.CompilerParams(dimension_semantics=sem, vmem_limit_bytes=VMEM_LIMIT)


def _tile(n, pref):
    t = min(n, pref)
    assert n % t == 0, (n, pref)
    return t


def _dot(a, b):
    return jnp.dot(a, b, preferred_element_type=F32)


def _rms(x, g=None, n=None):
    n = x.shape[-1] if n is None else n
    y = x * lax.rsqrt(jnp.sum(x * x, axis=-1, keepdims=True) * (1.0 / n) + EPS)
    return y if g is None else y * g


def _norm_into(x_ref, g_ref, dst_ref, rows):
    g = g_ref[...]

    def body(c, carry):
        r0 = pl.multiple_of(c * rows, rows)
        dst_ref[pl.ds(r0, rows), :] = _rms(x_ref[pl.ds(r0, rows), :], g).astype(dst_ref.dtype)
        return carry

    lax.fori_loop(0, x_ref.shape[0] // rows, body, 0)


def _rope_tab_kernel(pos_ref, f_ref, sgn_ref, c_ref, s_ref):
    ang = pos_ref[...].astype(F32) * f_ref[...]
    c_ref[...] = jnp.cos(ang)
    s_ref[...] = jnp.sin(ang) * sgn_ref[...]


def rope_tables(pos_col, freq, sgn, name):
    t = pos_col.shape[0]
    tm = _tile(t, 1024)
    return pl.pallas_call(
        _rope_tab_kernel,
        grid=(t // tm,),
        in_specs=[pl.BlockSpec((tm, 1), lambda i: (i, 0)),
                  pl.BlockSpec((1, 128), lambda i: (0, 0)),
                  pl.BlockSpec((1, 128), lambda i: (0, 0))],
        out_specs=[pl.BlockSpec((tm, 128), lambda i: (i, 0)),
                   pl.BlockSpec((tm, 128), lambda i: (i, 0))],
        out_shape=[jax.ShapeDtypeStruct((t, 128), F32)] * 2,
        compiler_params=_params("parallel"),
        name=name,
    )(pos_col, freq, sgn)


def _norm_matmul_kernel(x_ref, g_ref, w_ref, o_ref, xn_ref):
    @pl.when(pl.program_id(1) == 0)
    def _():
        _norm_into(x_ref, g_ref, xn_ref, 256)

    o_ref[...] = _dot(xn_ref[...], w_ref[...]).astype(o_ref.dtype)


def norm_matmul(x, g, w, out_dtype, tm, tn, name):
    t, k = x.shape
    n = w.shape[1]
    tm, tn = _tile(t, tm), _tile(n, tn)
    return pl.pallas_call(
        _norm_matmul_kernel,
        grid=(t // tm, n // tn),
        in_specs=[pl.BlockSpec((tm, k), lambda i, j: (i, 0)),
                  pl.BlockSpec((1, k), lambda i, j: (0, 0)),
                  pl.BlockSpec((k, tn), lambda i, j: (0, j))],
        out_specs=pl.BlockSpec((tm, tn), lambda i, j: (i, j)),
        out_shape=jax.ShapeDtypeStruct((t, n), out_dtype),
        scratch_shapes=[pltpu.VMEM((tm, k), BF16)],
        compiler_params=_params("parallel", "arbitrary"),
        name=name,
    )(x, g, w)


def _ret_in_kernel(x_ref, g_ref, w_ref, cos_ref, sin_ref, o_ref, xn_ref, *, n_q_tiles, n_rope_tiles):
    j = pl.program_id(1)

    @pl.when(j == 0)
    def _():
        _norm_into(x_ref, g_ref, xn_ref, 256)

    acc = _dot(xn_ref[...], w_ref[...])
    is_rope = j < n_rope_tiles
    scale = jnp.where(j >= n_q_tiles, RET_DK ** -0.5, 1.0).astype(F32)
    cos = cos_ref[...] * scale
    sin = sin_ref[...] * scale
    half = RET_DK // 2
    for hh in range(acc.shape[1] // RET_DK):
        x1 = acc[:, hh * RET_DK:hh * RET_DK + half]
        x2 = acc[:, hh * RET_DK + half:(hh + 1) * RET_DK]
        o_ref[:, hh * RET_DK:hh * RET_DK + half] = jnp.where(
            is_rope, x1 * cos - x2 * sin, x1).astype(o_ref.dtype)
        o_ref[:, hh * RET_DK + half:(hh + 1) * RET_DK] = jnp.where(
            is_rope, x2 * cos + x1 * sin, x2).astype(o_ref.dtype)


def ret_in_proj(x, g, w, cos, sin, tm, tn):
    t, k = x.shape
    n = w.shape[1]
    tm, tn = _tile(t, tm), _tile(n, tn)
    qw = RET_HEADS * RET_DK
    kern = functools.partial(_ret_in_kernel, n_q_tiles=qw // tn, n_rope_tiles=2 * qw // tn)
    return pl.pallas_call(
        kern,
        grid=(t // tm, n // tn),
        in_specs=[pl.BlockSpec((tm, k), lambda i, j: (i, 0), pipeline_mode=pl.Buffered(1)),
                  pl.BlockSpec((1, k), lambda i, j: (0, 0)),
                  pl.BlockSpec((k, tn), lambda i, j: (0, j)),
                  pl.BlockSpec((tm, 128), lambda i, j: (i, 0)),
                  pl.BlockSpec((tm, 128), lambda i, j: (i, 0))],
        out_specs=pl.BlockSpec((tm, tn), lambda i, j: (i, j)),
        out_shape=jax.ShapeDtypeStruct((t, n), BF16),
        scratch_shapes=[pltpu.VMEM((tm, k), BF16)],
        compiler_params=_params("parallel", "arbitrary"),
        name="ret_in_proj",
    )(x, g, w, cos, sin)


def _res_matmul_kernel(*refs, n_lhs):
    h_ref, o_ref = refs[0], refs[-1]
    acc = h_ref[...]
    for a_ref, w_ref in zip(refs[1:1 + n_lhs], refs[1 + n_lhs:1 + 2 * n_lhs]):
        acc = acc + _dot(a_ref[...], w_ref[...])
    o_ref[...] = acc


def res_matmul(h, lhs, ws, tm, tn, name):
    t, n = h.shape
    tm, tn = _tile(t, tm), _tile(n, tn)
    in_specs = [pl.BlockSpec((tm, tn), lambda i, j: (i, j))]
    in_specs += [pl.BlockSpec((tm, a.shape[1]), lambda i, j: (i, 0)) for a in lhs]
    in_specs += [pl.BlockSpec((w.shape[0], tn), lambda i, j: (0, j)) for w in ws]
    return pl.pallas_call(
        functools.partial(_res_matmul_kernel, n_lhs=len(lhs)),
        grid=(t // tm, n // tn),
        in_specs=in_specs,
        out_specs=pl.BlockSpec((tm, tn), lambda i, j: (i, j)),
        out_shape=jax.ShapeDtypeStruct((t, n), F32),
        compiler_params=_params("parallel", "arbitrary"),
        name=name,
    )(h, *lhs, *ws)


def _ffn_kernel(h_ref, g_ref, wg_ref, wu_ref, cw_ref, cb_ref, wd_ref, o_ref,
                u_ref, gs_ref, tail_ref, *, blocks_per_seq):
    i, j = pl.program_id(0), pl.program_id(1)
    tm = h_ref.shape[0]

    @pl.when(j == 0)
    def _():
        _norm_into(h_ref, g_ref, u_ref, 256)
        o_ref[...] = h_ref[...]

    @pl.when(jnp.logical_and(i == 0, j == 0))
    def _():
        tail_ref[...] = jnp.zeros(tail_ref.shape, F32)

    u = u_ref[...]
    seq_start = (i % blocks_per_seq) == 0
    prev = jnp.where(seq_start, 0.0, tail_ref[j])
    cw, cb = cw_ref[...], cb_ref[...]
    contrib = None
    for c0 in range(0, wg_ref.shape[1], FFN_STRIP):
        cols = slice(c0, c0 + FFN_STRIP)
        a = _dot(u, wg_ref[:, cols])
        up = _dot(u, wu_ref[:, cols])
        gs_ref[0:8, cols] = prev[:, cols]
        gs_ref[8:8 + tm, cols] = a
        tail_ref[j, :, cols] = a[tm - 8:tm, :]
        c = (cb[:, cols] + cw[0:1, cols] * gs_ref[6:6 + tm, cols] + cw[1:2, cols] * gs_ref[7:7 + tm, cols]
             + cw[2:3, cols] * a)
        act = (c * jax.nn.sigmoid(c) * up).astype(BF16)
        part = _dot(act, wd_ref[cols, :])
        contrib = part if contrib is None else contrib + part
    o_ref[...] += contrib


def conv_ffn(h, g, wg, wu, cw, cb, wd, seq, tm, tf):
    t, d = h.shape
    f = wg.shape[1]
    tm, tf = _tile(seq, tm), _tile(f, tf)
    kern = functools.partial(_ffn_kernel, blocks_per_seq=seq // tm)
    return pl.pallas_call(
        kern,
        grid=(t // tm, f // tf),
        in_specs=[pl.BlockSpec((tm, d), lambda i, j: (i, 0), pipeline_mode=pl.Buffered(1)),
                  pl.BlockSpec((1, d), lambda i, j: (0, 0)),
                  pl.BlockSpec((d, tf), lambda i, j: (0, j)),
                  pl.BlockSpec((d, tf), lambda i, j: (0, j)),
                  pl.BlockSpec((3, tf), lambda i, j: (0, j)),
                  pl.BlockSpec((1, tf), lambda i, j: (0, j)),
                  pl.BlockSpec((tf, d), lambda i, j: (j, 0))],
        out_specs=pl.BlockSpec((tm, d), lambda i, j: (i, 0)),
        out_shape=jax.ShapeDtypeStruct((t, d), F32),
        scratch_shapes=[pltpu.VMEM((tm, d), BF16),
                        pltpu.VMEM((tm + 8, tf), F32),
                        pltpu.VMEM((f // tf, 8, tf), F32)],
        compiler_params=_params("arbitrary", "arbitrary"),
        name="conv_ffn",
    )(h, g, wg, wu, cw, cb, wd)


def _ple_kernel(h_ref, g_ref, wgate_ref, p_ref, wproj_ref, o_ref, xn_ref, pb_ref):
    j = pl.program_id(1)
    tn = o_ref.shape[1]

    @pl.when(j == 0)
    def _():
        _norm_into(h_ref, g_ref, xn_ref, 256)
        pb_ref[...] = p_ref[...].astype(BF16)

    gate = jax.nn.sigmoid(_dot(xn_ref[...], wgate_ref[...]))
    proj = _dot(pb_ref[...], wproj_ref[...])
    c0 = pl.multiple_of(j * tn, tn)
    o_ref[...] = h_ref[:, pl.ds(c0, tn)] + proj * gate


def ple(h, g, wgate, p, wproj, tm, tn):
    t, d = h.shape
    tm, tn = _tile(t, tm), _tile(d, tn)
    return pl.pallas_call(
        _ple_kernel,
        grid=(t // tm, d // tn),
        in_specs=[pl.BlockSpec((tm, d), lambda i, j: (i, 0)),
                  pl.BlockSpec((1, d), lambda i, j: (0, 0)),
                  pl.BlockSpec((d, tn), lambda i, j: (0, j)),
                  pl.BlockSpec((tm, PLE_DIM), lambda i, j: (i, 0)),
                  pl.BlockSpec((PLE_DIM, tn), lambda i, j: (0, j))],
        out_specs=pl.BlockSpec((tm, tn), lambda i, j: (i, j)),
        out_shape=jax.ShapeDtypeStruct((t, d), F32),
        scratch_shapes=[pltpu.VMEM((tm, d), BF16), pltpu.VMEM((tm, PLE_DIM), BF16)],
        compiler_params=_params("parallel", "arbitrary"),
        name="ple",
    )(h, g, wgate, p, wproj)


def _hgrn_kernel(q_ref, f_ref, i_ref, g_ref, lbl_ref, on_ref, o_ref, st_ref):
    c = HG_CHUNK

    @pl.when(pl.program_id(1) == 0)
    def _():
        st_ref[...] = jnp.zeros(st_ref.shape, F32)

    lg = lbl_ref[...]
    e = jnp.exp(lg - jnp.max(lg, axis=0, keepdims=True))
    lb = e[0:1, :] / jnp.sum(e, axis=0, keepdims=True)
    onorm = on_ref[...]
    pw = 2 * HG_DK
    row = lax.broadcasted_iota(jnp.int32, (c, c), 0)
    col = lax.broadcasted_iota(jnp.int32, (c, c), 1)
    tril = jnp.where(row >= col, 1.0, 0.0).astype(BF16)
    row2 = lax.broadcasted_iota(jnp.int32, (c, 2 * c), 0)
    col2 = lax.broadcasted_iota(jnp.int32, (c, 2 * c), 1)
    causal2 = row2 >= jnp.where(col2 >= c, col2 - c, col2)
    left = lax.broadcasted_iota(jnp.int32, (c, pw), 1) < HG_DK
    same_head = ((lax.broadcasted_iota(jnp.int32, (pw, pw), 0) < HG_DV)
                 == (lax.broadcasted_iota(jnp.int32, (pw, pw), 1) < HG_DK))

    def block_diag(x):
        zero = jnp.zeros_like(x)
        return jnp.concatenate([jnp.where(left, x, zero), jnp.where(left, zero, x)], axis=0)

    def chunk(ci, carry):
        rows = pl.ds(pl.multiple_of(ci * c, c), c)
        q = q_ref[rows, :]
        v = i_ref[rows, :].astype(BF16)
        f = lb + (1.0 - lb) * jax.nn.sigmoid(f_ref[rows, :])
        k = 1.0 - f
        lf = jnp.log(f)
        hi = lf.astype(BF16)
        r1 = lf - hi.astype(F32)
        mid = r1.astype(BF16)
        lo = (r1 - mid.astype(F32)).astype(BF16)
        bc = _dot(tril, hi) + _dot(tril, mid) + _dot(tril, lo)
        ref = bc[c // 2 - 1:c // 2, :]
        b_last = bc[c - 1:c, :]
        q_rel = (q * jnp.exp(bc - ref)).astype(BF16)
        k_rel = (k * jnp.exp(ref - bc)).astype(BF16)
        q_dec = (q * jnp.exp(bc)).astype(BF16)
        k_dec = (k * jnp.exp(b_last - bc)).astype(BF16)
        decay = jnp.exp(b_last)
        for p in range(HG_HEADS // 2):
            cols = slice(p * pw, (p + 1) * pw)
            a = lax.dot_general(q_rel[:, cols], block_diag(k_rel[:, cols]), NT, preferred_element_type=F32)
            a = jnp.where(causal2, a, 0.0).astype(BF16)
            st = st_ref[p]
            out = _dot(a, block_diag(v[:, cols])) + lax.dot_general(q_dec[:, cols], st.astype(BF16), NT,
                                                                    preferred_element_type=F32)
            upd = lax.dot_general(v[:, cols], k_dec[:, cols], TN, preferred_element_type=F32)
            st_ref[p] = st * decay[:, cols] + jnp.where(same_head, upd, 0.0)
            for hh in range(2):
                hc = slice(p * pw + hh * HG_DV, p * pw + (hh + 1) * HG_DV)
                g = g_ref[rows, hc]
                o_ref[rows, hc] = (_rms(out[:, hh * HG_DV:(hh + 1) * HG_DV], onorm)
                                   * (g * jax.nn.sigmoid(g))).astype(o_ref.dtype)
        return carry

    lax.fori_loop(0, q_ref.shape[0] // c, chunk, 0, unroll=2)


def hgrn(z, lb_logits, onorm, batch, seq, cb):
    t = z.shape[0]
    cb = _tile(seq, cb)
    nblk = seq // cb
    spec = lambda part: pl.BlockSpec((cb, HG_W), lambda b, s, part=part: (b * nblk + s, part))
    return pl.pallas_call(
        _hgrn_kernel,
        grid=(batch, nblk),
        in_specs=[spec(0), spec(1), spec(2), spec(3),
                  pl.BlockSpec(lb_logits.shape, lambda b, s: (0, 0)),
                  pl.BlockSpec((1, HG_DV), lambda b, s: (0, 0))],
        out_specs=pl.BlockSpec((cb, HG_W), lambda b, s: (b * nblk + s, 0)),
        out_shape=jax.ShapeDtypeStruct((t, HG_W), BF16),
        scratch_shapes=[pltpu.VMEM((HG_HEADS // 2, 2 * HG_DV, 2 * HG_DK), F32)],
        compiler_params=_params("parallel", "arbitrary"),
        name="hgrn2",
    )(z, z, z, z, lb_logits, onorm)


def _mla_prep_kernel(z_ref, qa_ref, kva_ref, wuq_ref, wukv_ref, qn_ref, kn_ref, cos_ref, sin_ref,
                     qt_out, k_out, vt_out):
    cq = _rms(z_ref[:, 0:MLA_Q_RANK], qa_ref[...]).astype(BF16)
    ckv = _rms(z_ref[:, MLA_Q_RANK:MLA_Q_RANK + MLA_KV_RANK], kva_ref[...]).astype(BF16)
    kpe = z_ref[:, MLA_Q_RANK + MLA_KV_RANK:]
    qf = _dot(cq, wuq_ref[...])
    kvf = _dot(ckv, wukv_ref[...])
    cos, sin = cos_ref[...], sin_ref[...]
    qn, kn = qn_ref[...], kn_ref[...]
    scale = MLA_QK ** -0.5 * LOG2E
    tm = z_ref.shape[0]

    def rope(x):
        return x * cos + pltpu.roll(x, 64, 1) * sin

    kpe_ss = jnp.sum(kpe * kpe, axis=-1, keepdims=True)
    k_rope = rope(kpe * kn[:, MLA_NOPE:])
    ones_rows = jnp.where(lax.broadcasted_iota(jnp.int32, (VT_ROWS - MLA_V, tm), 0) == 0, 1.0, 0.0)
    for h in range(MLA_HEADS):
        c0 = h * MLA_PAD
        qh = qf[:, c0:c0 + MLA_PAD]
        rq = lax.rsqrt(jnp.sum(qh * qh, axis=-1, keepdims=True) * (1.0 / MLA_QK) + EPS) * scale
        qh = qh * rq * qn
        qt_out[h, 0:MLA_NOPE, :] = qh[:, :MLA_NOPE].T.astype(BF16)
        qt_out[h, MLA_NOPE:MLA_PAD, :] = rope(qh[:, MLA_NOPE:]).T.astype(BF16)
        kh = kvf[:, c0:c0 + MLA_NOPE]
        rk = lax.rsqrt((jnp.sum(kh * kh, axis=-1, keepdims=True) + kpe_ss) * (1.0 / MLA_QK) + EPS)
        k_out[:, c0:c0 + MLA_NOPE] = (kh * rk * kn[:, :MLA_NOPE]).astype(BF16)
        k_out[:, c0 + MLA_NOPE:c0 + MLA_PAD] = (k_rope * rk).astype(BF16)
        vt_out[h, 0:MLA_V, :] = kvf[:, c0 + MLA_NOPE:c0 + MLA_PAD].T.astype(BF16)
        vt_out[h, MLA_V:VT_ROWS, :] = ones_rows.astype(BF16)


def mla_prep(z, qa, kva, wuq, wukv, qn, kn, cos, sin, tm):
    t, zw = z.shape
    tm = _tile(t, tm)
    full = lambda a: pl.BlockSpec(a.shape, lambda i: (0, 0))
    rows = lambda w: pl.BlockSpec((tm, w), lambda i: (i, 0))
    cols = lambda r: pl.BlockSpec((MLA_HEADS, r, tm), lambda i: (0, 0, i))
    return pl.pallas_call(
        _mla_prep_kernel,
        grid=(t // tm,),
        in_specs=[rows(zw), full(qa), full(kva), full(wuq), full(wukv), full(qn), full(kn),
                  rows(128), rows(128)],
        out_specs=[cols(MLA_PAD), rows(MLA_HEADS * MLA_PAD), cols(VT_ROWS)],
        out_shape=[jax.ShapeDtypeStruct((MLA_HEADS, MLA_PAD, t), BF16),
                   jax.ShapeDtypeStruct((t, MLA_HEADS * MLA_PAD), BF16),
                   jax.ShapeDtypeStruct((MLA_HEADS, VT_ROWS, t), BF16)],
        compiler_params=_params("parallel"),
        name="mla_prep",
    )(z, qa, kva, wuq, wukv, qn, kn, cos, sin)


def _flash_kernel(qt_ref, k_ref, vt_ref, o_ref, acc_ref, *, tq, tk, tk_main):
    qi = pl.program_id(2)
    qt = qt_ref[0]
    acc_ref[...] = jnp.zeros(acc_ref.shape, F32)

    def step(k0, kw, m_all, q_lo, diagonal):
        s = _dot(k_ref[pl.ds(k0, kw), :], qt[:, q_lo:])
        if diagonal:
            row = lax.broadcasted_iota(jnp.int32, s.shape, 0)
            col = lax.broadcasted_iota(jnp.int32, s.shape, 1)
            s = jnp.where(row <= col, s, -jnp.inf)
        m_prev = m_all[:, q_lo:]
        m_new = jnp.maximum(m_prev, jnp.max(s, axis=0, keepdims=True))
        alpha = jnp.exp2(m_prev - m_new)
        p = jnp.exp2(s - m_new).astype(BF16)
        acc_ref[:, q_lo:] = alpha * acc_ref[:, q_lo:] + _dot(vt_ref[0, :, pl.ds(k0, kw)], p)
        return m_new if q_lo == 0 else jnp.concatenate([m_all[:, :q_lo], m_new], axis=1)

    m = lax.fori_loop(0, qi * (tq // tk_main),
                      lambda j, m: step(pl.multiple_of(j * tk_main, tk_main), tk_main, m, 0, False),
                      jnp.full((1, tq), -jnp.inf, F32))
    for d in range(tq // tk):
        m = step(pl.multiple_of(qi * tq + d * tk, tk), tk, m, d * tk, True)
    acc = acc_ref[...]
    o_ref[...] = (acc[0:MLA_V, :] / acc[MLA_V:MLA_V + 1, :]).T.astype(o_ref.dtype)


def flash_attention(qt, k, vt, batch, seq, tq, tk, tk_main):
    t = k.shape[0]
    tq = _tile(seq, tq)
    tk = _tile(tq, tk)
    tk_main = _tile(tq, tk_main)
    nq = seq // tq
    return pl.pallas_call(
        functools.partial(_flash_kernel, tq=tq, tk=tk, tk_main=tk_main),
        grid=(batch, MLA_HEADS, nq),
        in_specs=[pl.BlockSpec((1, MLA_PAD, tq), lambda b, h, i: (h, 0, b * nq + i)),
                  pl.BlockSpec((seq, MLA_PAD), lambda b, h, i: (b, h)),
                  pl.BlockSpec((1, VT_ROWS, seq), lambda b, h, i: (h, 0, b))],
        out_specs=pl.BlockSpec((tq, MLA_V), lambda b, h, i: (b * nq + i, h)),
        out_shape=jax.ShapeDtypeStruct((t, MLA_HEADS * MLA_V), BF16),
        scratch_shapes=[pltpu.VMEM((VT_ROWS, tq), F32)],
        compiler_params=_params("parallel", "parallel", "arbitrary"),
        name="mla_flash",
    )(qt, k, vt)


def _ret_kernel(q_ref, k_ref, v_ref, g_ref, o_ref, r_ref):
    c = RET_CHUNK
    hf = pl.program_id(1).astype(F32)

    @pl.when(pl.program_id(2) == 0)
    def _():
        r_ref[...] = jnp.zeros(r_ref.shape, F32)

    def log_gamma(shape):
        return jnp.log(1.0 - jnp.exp2(-5.0 - jnp.full(shape, hf, F32)))

    row = lax.broadcasted_iota(jnp.int32, (c, c), 0)
    col = lax.broadcasted_iota(jnp.int32, (c, c), 1)
    diff = (row - col).astype(F32)
    d_intra = jnp.where(diff >= 0, jnp.exp(jnp.where(diff >= 0, diff, 0.0) * log_gamma((c, c))), 0.0)
    idx = lax.broadcasted_iota(jnp.int32, (c, RET_DK), 0).astype(F32)
    lg_k = log_gamma((c, RET_DK))
    q_dec = jnp.exp((idx + 1.0) * lg_k).astype(BF16)
    k_dec = jnp.exp((c - 1.0 - idx) * lg_k).astype(BF16)
    c_dec = jnp.exp(c * log_gamma((1, RET_DV)))

    def chunk(ci, carry):
        rows = pl.ds(pl.multiple_of(ci * c, c), c)
        q, k, v = q_ref[rows, :], k_ref[rows, :], v_ref[rows, :]
        a = lax.dot_general(q, k, NT, preferred_element_type=F32) * d_intra
        r = r_ref[...]
        out = _dot(a.astype(BF16), v) + _dot(q * q_dec, r.astype(BF16))
        r_ref[...] = r * c_dec + lax.dot_general(k * k_dec, v, TN, preferred_element_type=F32)
        g = g_ref[rows, :].astype(F32)
        o_ref[rows, :] = (_rms(out) * (g * jax.nn.sigmoid(g))).astype(o_ref.dtype)
        return carry

    lax.fori_loop(0, q_ref.shape[0] // c, chunk, 0, unroll=2)


def retention(z, batch, seq, cb):
    t = z.shape[0]
    cb = _tile(seq, cb)
    assert cb % RET_CHUNK == 0
    nblk = seq // cb
    nqk = RET_HEADS * RET_DK // RET_DK
    nv = (2 * RET_HEADS * RET_DK) // RET_DV
    return pl.pallas_call(
        _ret_kernel,
        grid=(batch, RET_HEADS, nblk),
        in_specs=[pl.BlockSpec((cb, RET_DK), lambda b, h, s: (b * nblk + s, h)),
                  pl.BlockSpec((cb, RET_DK), lambda b, h, s: (b * nblk + s, nqk + h)),
                  pl.BlockSpec((cb, RET_DV), lambda b, h, s: (b * nblk + s, nv + h)),
                  pl.BlockSpec((cb, RET_DV), lambda b, h, s: (b * nblk + s, nv + RET_HEADS + h))],
        out_specs=pl.BlockSpec((cb, RET_DV), lambda b, h, s: (b * nblk + s, h)),
        out_shape=jax.ShapeDtypeStruct((t, RET_HEADS * RET_DV), BF16),
        scratch_shapes=[pltpu.VMEM((RET_DK, RET_DV), F32)],
        compiler_params=_params("parallel", "parallel", "arbitrary"),
        name="retention",
    )(z, z, z, z)


def _rope_pad(a, axis):
    x1, x2 = jnp.split(a, 2, axis=axis)
    z = jnp.zeros_like(x1)
    return jnp.concatenate([x1, z, x2, z], axis=axis)


def _pad_qk_gain(g):
    return jnp.concatenate([g[:MLA_NOPE], _rope_pad(g[MLA_NOPE:], 0)])[None, :]


def kernel(x, p, positions, norm_mix, norm_ffn, norm_ple, e_w_in, e_lb_logits, e_q_a_norm, e_kv_a_norm, e_w_uq, e_w_ukv, e_q_norm, e_k_norm, e_hg_onorm, e_w_out, o_w_in, o_w_out, ffn_w_gate, ffn_w_up, ffn_conv_w, ffn_conv_b, ffn_w_down, ple_w_proj, ple_w_gate):
    batch, seq, d = x.shape
    t = batch * seq
    h = x.reshape(t, d)
    pos_col = positions.reshape(t, 1)

    ones = jnp.ones((1, 128), F32)
    f_ret = (ROPE_BASE ** (-jnp.arange(RET_DK // 2, dtype=F32) / (RET_DK // 2)))[None, :]
    f_mla = ROPE_BASE ** (-jnp.arange(MLA_ROPE // 2, dtype=F32) / (MLA_ROPE // 2))
    f_mla = _rope_pad(jnp.concatenate([f_mla, f_mla]), 0)[None, :]
    sgn_mla = _rope_pad(jnp.concatenate([-jnp.ones(32, F32), jnp.ones(32, F32)]), 0)[None, :]
    cos_ret, sin_ret = rope_tables(pos_col, f_ret, ones, "rope_tab_ret")
    cos_mla, sin_mla = rope_tables(pos_col, f_mla, sgn_mla, "rope_tab_mla")

    w_in = e_w_in[0]
    w_hg = w_in[:, :4 * HG_W].astype(BF16)
    off = 4 * HG_W + MLA_Q_RANK + MLA_KV_RANK
    w_mla = jnp.concatenate([w_in[:, 4 * HG_W:off], _rope_pad(w_in[:, off:], 1)], axis=1).astype(BF16)
    wuq = e_w_uq[0].reshape(MLA_Q_RANK, MLA_HEADS, MLA_QK)
    wuq = jnp.concatenate([wuq[..., :MLA_NOPE], _rope_pad(wuq[..., MLA_NOPE:], 2)], axis=-1)
    wuq = wuq.reshape(MLA_Q_RANK, MLA_HEADS * MLA_PAD).astype(BF16)
    wukv = e_w_ukv[0].astype(BF16)
    g_mix0 = norm_mix[0][None, :]

    z_hg = norm_matmul(h, g_mix0, w_hg, F32, 1024, 1024, "in_proj_hgrn")
    z_mla = norm_matmul(h, g_mix0, w_mla, F32, 1024, w_mla.shape[1], "in_proj_mla")
    o_a = hgrn(z_hg, e_lb_logits, e_hg_onorm[0][None, :], batch, seq, 512)
    qt, k, vt = mla_prep(z_mla, e_q_a_norm[0][None, :], e_kv_a_norm[0][None, :], wuq, wukv,
                       _pad_qk_gain(e_q_norm[0]), _pad_qk_gain(e_k_norm[0]), cos_mla, sin_mla, 512)
    o_b = flash_attention(qt, k, vt, batch, seq, 2048, 512, 1024)
    w_out = e_w_out[0].astype(BF16)
    h = res_matmul(h, [o_a, o_b], [w_out[:HG_W], w_out[HG_W:]], 1024, 1024, "out_proj_even")

    def channel_mix(h, i):
        h = conv_ffn(h, norm_ffn[i][None, :], ffn_w_gate[i].astype(BF16), ffn_w_up[i].astype(BF16),
                     ffn_conv_w[i], ffn_conv_b[i][None, :], ffn_w_down[i].astype(BF16), seq, 1024, 512)
        return ple(h, norm_ple[i][None, :], ple_w_gate[i].astype(BF16), p[i].reshape(t, PLE_DIM),
                   ple_w_proj[i].astype(BF16), 1024, 1024)

    h = channel_mix(h, 0)

    z_ret = ret_in_proj(h, norm_mix[1][None, :], o_w_in[0].astype(BF16), cos_ret, sin_ret, 1024, 2048)
    o_r = retention(z_ret, batch, seq, 2048)
    h = res_matmul(h, [o_r], [o_w_out[0].astype(BF16)], 1024, 1024, "out_proj_odd")
    h = channel_mix(h, 1)
    return h.reshape(batch, seq, d)
```

```python
import functools

import jax
import jax.numpy as jnp
from jax import lax
from jax.experimental import pallas as pl
from jax.experimental.pallas import tpu as pltpu

F32 = jnp.float32
BF16 = jnp.bfloat16

D_MODEL = 2048
PLE_DIM = 256
HG_HEADS = 8
HG_DK = 128
HG_DV = 128
HG_W = HG_HEADS * HG_DK
HG_CHUNK = 64
MLA_HEADS = 8
MLA_Q_RANK = 512
MLA_KV_RANK = 512
MLA_NOPE = 128
MLA_ROPE = 64
MLA_V = 128
MLA_QK = MLA_NOPE + MLA_ROPE
MLA_PAD = 256
VT_ROWS = MLA_V + 16
LOG2E = 1.4426950408889634
RET_HEADS = 8
RET_DK = 256
RET_DV = 512
RET_CHUNK = 256
D_FF = 5632
FFN_STRIP = 256
ROPE_BASE = 10000.0
EPS = 1e-6

VMEM_LIMIT = 60 * 2**20
CAST_BLOCK_BYTES = 8 * 2**20

NT = (((1,), (1,)), ((), ()))
TN = (((0,), (0,)), ((), ()))


def _params(*sem):
    return pltpu.CompilerParams(dimension_semantics=sem, vmem_limit_bytes=VMEM_LIMIT)


def _tile(n, pref):
    t = min(n, pref)
    assert n % t == 0, (n, pref)
    return t


def _dot(a, b):
    return jnp.dot(a, b, preferred_element_type=F32)


def _rms(x, g=None, n=None):
    n = x.shape[-1] if n is None else n
    y = x * lax.rsqrt(jnp.sum(x * x, axis=-1, keepdims=True) * (1.0 / n) + EPS)
    return y if g is None else y * g


def _norm_into(x_ref, g_ref, dst_ref, rows):
    g = g_ref[...]

    def body(c, carry):
        r0 = pl.multiple_of(c * rows, rows)
        dst_ref[pl.ds(r0, rows), :] = _rms(x_ref[pl.ds(r0, rows), :], g).astype(dst_ref.dtype)
        return carry

    lax.fori_loop(0, x_ref.shape[0] // rows, body, 0)


def _cast_kernel(w_ref, o_ref):
    o_ref[...] = w_ref[...].astype(o_ref.dtype)


def cast_bf16(w, layer, ncols=None, name="cast_bf16"):
    _, r, c = w.shape
    ncols = c if ncols is None else ncols
    tr = r
    while tr % 16 == 0 and tr * ncols * 4 > CAST_BLOCK_BYTES:
        tr //= 2
    return pl.pallas_call(
        _cast_kernel,
        grid=(r // tr,),
        in_specs=[pl.BlockSpec((None, tr, ncols), lambda i: (layer, i, 0))],
        out_specs=pl.BlockSpec((tr, ncols), lambda i: (i, 0)),
        out_shape=jax.ShapeDtypeStruct((r, ncols), BF16),
        compiler_params=_params("parallel"),
        name=name,
    )(w)


def _rope_tab_kernel(pos_ref, f_ref, sgn_ref, c_ref, s_ref):
    ang = pos_ref[...].astype(F32) * f_ref[...]
    c_ref[...] = jnp.cos(ang)
    s_ref[...] = jnp.sin(ang) * sgn_ref[...]


def rope_tables(pos_col, freq, sgn, name):
    t = pos_col.shape[0]
    tm = _tile(t, 1024)
    return pl.pallas_call(
        _rope_tab_kernel,
        grid=(t // tm,),
        in_specs=[pl.BlockSpec((tm, 1), lambda i: (i, 0)),
                  pl.BlockSpec((1, 128), lambda i: (0, 0)),
                  pl.BlockSpec((1, 128), lambda i: (0, 0))],
        out_specs=[pl.BlockSpec((tm, 128), lambda i: (i, 0)),
                   pl.BlockSpec((tm, 128), lambda i: (i, 0))],
        out_shape=[jax.ShapeDtypeStruct((t, 128), F32)] * 2,
        compiler_params=_params("parallel"),
        name=name,
    )(pos_col, freq, sgn)


def _norm_matmul_kernel(x_ref, g_ref, w_ref, o_ref, xn_ref):
    @pl.when(pl.program_id(1) == 0)
    def _():
        _norm_into(x_ref, g_ref, xn_ref, 256)

    o_ref[...] = _dot(xn_ref[...], w_ref[...]).astype(o_ref.dtype)


def norm_matmul(x, g, w, out_dtype, tm, tn, name):
    t, k = x.shape
    n = w.shape[1]
    tm, tn = _tile(t, tm), _tile(n, tn)
    return pl.pallas_call(
        _norm_matmul_kernel,
        grid=(t // tm, n // tn),
        in_specs=[pl.BlockSpec((tm, k), lambda i, j: (i, 0)),
                  pl.BlockSpec((1, k), lambda i, j: (0, 0)),
                  pl.BlockSpec((k, tn), lambda i, j: (0, j))],
        out_specs=pl.BlockSpec((tm, tn), lambda i, j: (i, j)),
        out_shape=jax.ShapeDtypeStruct((t, n), out_dtype),
        scratch_shapes=[pltpu.VMEM((tm, k), BF16)],
        compiler_params=_params("parallel", "arbitrary"),
        name=name,
    )(x, g, w)


def _ret_in_kernel(x_ref, g_ref, w_ref, cos_ref, sin_ref, o_ref, xn_ref, *, n_q_tiles, n_rope_tiles):
    j = pl.program_id(1)

    @pl.when(j == 0)
    def _():
        _norm_into(x_ref, g_ref, xn_ref, 256)

    acc = _dot(xn_ref[...], w_ref[...])
    is_rope = j < n_rope_tiles
    scale = jnp.where(j >= n_q_tiles, RET_DK ** -0.5, 1.0).astype(F32)
    cos = cos_ref[...] * scale
    sin = sin_ref[...] * scale
    half = RET_DK // 2
    for hh in range(acc.shape[1] // RET_DK):
        x1 = acc[:, hh * RET_DK:hh * RET_DK + half]
        x2 = acc[:, hh * RET_DK + half:(hh + 1) * RET_DK]
        o_ref[:, hh * RET_DK:hh * RET_DK + half] = jnp.where(
            is_rope, x1 * cos - x2 * sin, x1).astype(o_ref.dtype)
        o_ref[:, hh * RET_DK + half:(hh + 1) * RET_DK] = jnp.where(
            is_rope, x2 * cos + x1 * sin, x2).astype(o_ref.dtype)


def ret_in_proj(x, g, w, cos, sin, tm, tn):
    t, k = x.shape
    n = w.shape[1]
    tm, tn = _tile(t, tm), _tile(n, tn)
    qw = RET_HEADS * RET_DK
    kern = functools.partial(_ret_in_kernel, n_q_tiles=qw // tn, n_rope_tiles=2 * qw // tn)
    return pl.pallas_call(
        kern,
        grid=(t // tm, n // tn),
        in_specs=[pl.BlockSpec((tm, k), lambda i, j: (i, 0)),
                  pl.BlockSpec((1, k), lambda i, j: (0, 0)),
                  pl.BlockSpec((k, tn), lambda i, j: (0, j)),
                  pl.BlockSpec((tm, 128), lambda i, j: (i, 0)),
                  pl.BlockSpec((tm, 128), lambda i, j: (i, 0))],
        out_specs=pl.BlockSpec((tm, tn), lambda i, j: (i, j)),
        out_shape=jax.ShapeDtypeStruct((t, n), BF16),
        scratch_shapes=[pltpu.VMEM((tm, k), BF16)],
        compiler_params=_params("parallel", "arbitrary"),
        name="ret_in_proj",
    )(x, g, w, cos, sin)


def _res_matmul_kernel(*refs, n_lhs):
    h_ref, o_ref = refs[0], refs[-1]
    acc = h_ref[...]
    for a_ref, w_ref in zip(refs[1:1 + n_lhs], refs[1 + n_lhs:1 + 2 * n_lhs]):
        acc = acc + _dot(a_ref[...], w_ref[...])
    o_ref[...] = acc


def res_matmul(h, lhs, ws, tm, tn, name):
    t, n = h.shape
    tm, tn = _tile(t, tm), _tile(n, tn)
    in_specs = [pl.BlockSpec((tm, tn), lambda i, j: (i, j))]
    in_specs += [pl.BlockSpec((tm, a.shape[1]), lambda i, j: (i, 0)) for a in lhs]
    in_specs += [pl.BlockSpec((w.shape[0], tn), lambda i, j: (0, j)) for w in ws]
    return pl.pallas_call(
        functools.partial(_res_matmul_kernel, n_lhs=len(lhs)),
        grid=(t // tm, n // tn),
        in_specs=in_specs,
        out_specs=pl.BlockSpec((tm, tn), lambda i, j: (i, j)),
        out_shape=jax.ShapeDtypeStruct((t, n), F32),
        compiler_params=_params("parallel", "arbitrary"),
        name=name,
    )(h, *lhs, *ws)


def _ffn_kernel(h_ref, g_ref, wg_ref, wu_ref, cw_ref, cb_ref, wd_ref, o_ref,
                u_ref, gs_ref, tail_ref, *, blocks_per_seq):
    i, j = pl.program_id(0), pl.program_id(1)
    tm = h_ref.shape[0]

    @pl.when(j == 0)
    def _():
        _norm_into(h_ref, g_ref, u_ref, 256)
        o_ref[...] = h_ref[...]

    @pl.when(jnp.logical_and(i == 0, j == 0))
    def _():
        tail_ref[...] = jnp.zeros(tail_ref.shape, F32)

    u = u_ref[...]
    seq_start = (i % blocks_per_seq) == 0
    prev = jnp.where(seq_start, 0.0, tail_ref[j])
    cw, cb = cw_ref[...], cb_ref[...]
    contrib = None
    for c0 in range(0, wg_ref.shape[1], FFN_STRIP):
        cols = slice(c0, c0 + FFN_STRIP)
        a = _dot(u, wg_ref[:, cols])
        up = _dot(u, wu_ref[:, cols])
        gs_ref[0:8, cols] = prev[:, cols]
        gs_ref[8:8 + tm, cols] = a
        tail_ref[j, :, cols] = a[tm - 8:tm, :]
        c = (cb[:, cols] + cw[0:1, cols] * gs_ref[6:6 + tm, cols] + cw[1:2, cols] * gs_ref[7:7 + tm, cols]
             + cw[2:3, cols] * a)
        act = (c * jax.nn.sigmoid(c) * up).astype(BF16)
        part = _dot(act, wd_ref[cols, :])
        contrib = part if contrib is None else contrib + part
    o_ref[...] += contrib


def conv_ffn(h, g, wg, wu, cw, cb, wd, seq, tm, tf):
    t, d = h.shape
    f = wg.shape[1]
    tm, tf = _tile(seq, tm), _tile(f, tf)
    kern = functools.partial(_ffn_kernel, blocks_per_seq=seq // tm)
    return pl.pallas_call(
        kern,
        grid=(t // tm, f // tf),
        in_specs=[pl.BlockSpec((tm, d), lambda i, j: (i, 0)),
                  pl.BlockSpec((1, d), lambda i, j: (0, 0)),
                  pl.BlockSpec((d, tf), lambda i, j: (0, j)),
                  pl.BlockSpec((d, tf), lambda i, j: (0, j)),
                  pl.BlockSpec((3, tf), lambda i, j: (0, j)),
                  pl.BlockSpec((1, tf), lambda i, j: (0, j)),
                  pl.BlockSpec((tf, d), lambda i, j: (j, 0))],
        out_specs=pl.BlockSpec((tm, d), lambda i, j: (i, 0)),
        out_shape=jax.ShapeDtypeStruct((t, d), F32),
        scratch_shapes=[pltpu.VMEM((tm, d), BF16),
                        pltpu.VMEM((tm + 8, tf), F32),
                        pltpu.VMEM((f // tf, 8, tf), F32)],
        compiler_params=_params("arbitrary", "arbitrary"),
        name="conv_ffn",
    )(h, g, wg, wu, cw, cb, wd)


def _ple_kernel(h_ref, g_ref, wgate_ref, p_ref, wproj_ref, o_ref, xn_ref, pb_ref):
    j = pl.program_id(1)
    tn = o_ref.shape[1]

    @pl.when(j == 0)
    def _():
        _norm_into(h_ref, g_ref, xn_ref, 256)
        pb_ref[...] = p_ref[...].astype(BF16)

    gate = jax.nn.sigmoid(_dot(xn_ref[...], wgate_ref[...]))
    proj = _dot(pb_ref[...], wproj_ref[...])
    c0 = pl.multiple_of(j * tn, tn)
    o_ref[...] = h_ref[:, pl.ds(c0, tn)] + proj * gate


def ple(h, g, wgate, p, wproj, tm, tn):
    t, d = h.shape
    tm, tn = _tile(t, tm), _tile(d, tn)
    return pl.pallas_call(
        _ple_kernel,
        grid=(t // tm, d // tn),
        in_specs=[pl.BlockSpec((tm, d), lambda i, j: (i, 0)),
                  pl.BlockSpec((1, d), lambda i, j: (0, 0)),
                  pl.BlockSpec((d, tn), lambda i, j: (0, j)),
                  pl.BlockSpec((tm, PLE_DIM), lambda i, j: (i, 0)),
                  pl.BlockSpec((PLE_DIM, tn), lambda i, j: (0, j))],
        out_specs=pl.BlockSpec((tm, tn), lambda i, j: (i, j)),
        out_shape=jax.ShapeDtypeStruct((t, d), F32),
        scratch_shapes=[pltpu.VMEM((tm, d), BF16), pltpu.VMEM((tm, PLE_DIM), BF16)],
        compiler_params=_params("parallel", "arbitrary"),
        name="ple",
    )(h, g, wgate, p, wproj)


def _hgrn_kernel(q_ref, f_ref, i_ref, g_ref, lbl_ref, on_ref, o_ref, st_ref):
    c = HG_CHUNK

    @pl.when(pl.program_id(1) == 0)
    def _():
        st_ref[...] = jnp.zeros(st_ref.shape, F32)

    lg = lbl_ref[...]
    e = jnp.exp(lg - jnp.max(lg, axis=0, keepdims=True))
    lb = e[0:1, :] / jnp.sum(e, axis=0, keepdims=True)
    onorm = on_ref[...]
    pw = 2 * HG_DK
    row = lax.broadcasted_iota(jnp.int32, (c, c), 0)
    col = lax.broadcasted_iota(jnp.int32, (c, c), 1)
    tril = jnp.where(row >= col, 1.0, 0.0).astype(BF16)
    row2 = lax.broadcasted_iota(jnp.int32, (c, 2 * c), 0)
    col2 = lax.broadcasted_iota(jnp.int32, (c, 2 * c), 1)
    causal2 = row2 >= jnp.where(col2 >= c, col2 - c, col2)
    left = lax.broadcasted_iota(jnp.int32, (c, pw), 1) < HG_DK
    same_head = ((lax.broadcasted_iota(jnp.int32, (pw, pw), 0) < HG_DV)
                 == (lax.broadcasted_iota(jnp.int32, (pw, pw), 1) < HG_DK))

    def block_diag(x):
        zero = jnp.zeros_like(x)
        return jnp.concatenate([jnp.where(left, x, zero), jnp.where(left, zero, x)], axis=0)

    def chunk(ci, carry):
        rows = pl.ds(pl.multiple_of(ci * c, c), c)
        q = q_ref[rows, :]
        v = i_ref[rows, :].astype(BF16)
        f = lb + (1.0 - lb) * jax.nn.sigmoid(f_ref[rows, :])
        k = 1.0 - f
        lf = jnp.log(f)
        hi = lf.astype(BF16)
        r1 = lf - hi.astype(F32)
        mid = r1.astype(BF16)
        lo = (r1 - mid.astype(F32)).astype(BF16)
        bc = _dot(tril, hi) + _dot(tril, mid) + _dot(tril, lo)
        ref = bc[c // 2 - 1:c // 2, :]
        b_last = bc[c - 1:c, :]
        q_rel = (q * jnp.exp(bc - ref)).astype(BF16)
        k_rel = (k * jnp.exp(ref - bc)).astype(BF16)
        q_dec = (q * jnp.exp(bc)).astype(BF16)
        k_dec = (k * jnp.exp(b_last - bc)).astype(BF16)
        decay = jnp.exp(b_last)
        for p in range(HG_HEADS // 2):
            cols = slice(p * pw, (p + 1) * pw)
            a = lax.dot_general(q_rel[:, cols], block_diag(k_rel[:, cols]), NT, preferred_element_type=F32)
            a = jnp.where(causal2, a, 0.0).astype(BF16)
            st = st_ref[p]
            out = _dot(a, block_diag(v[:, cols])) + lax.dot_general(q_dec[:, cols], st.astype(BF16), NT,
                                                                    preferred_element_type=F32)
            upd = lax.dot_general(v[:, cols], k_dec[:, cols], TN, preferred_element_type=F32)
            st_ref[p] = st * decay[:, cols] + jnp.where(same_head, upd, 0.0)
            for hh in range(2):
                hc = slice(p * pw + hh * HG_DV, p * pw + (hh + 1) * HG_DV)
                g = g_ref[rows, hc]
                o_ref[rows, hc] = (_rms(out[:, hh * HG_DV:(hh + 1) * HG_DV], onorm)
                                   * (g * jax.nn.sigmoid(g))).astype(o_ref.dtype)
        return carry

    lax.fori_loop(0, q_ref.shape[0] // c, chunk, 0, unroll=2)


def hgrn(z, lb_logits, onorm, batch, seq, cb):
    t = z.shape[0]
    cb = _tile(seq, cb)
    nblk = seq // cb
    spec = lambda part: pl.BlockSpec((cb, HG_W), lambda b, s, part=part: (b * nblk + s, part))
    return pl.pallas_call(
        _hgrn_kernel,
        grid=(batch, nblk),
        in_specs=[spec(0), spec(1), spec(2), spec(3),
                  pl.BlockSpec(lb_logits.shape, lambda b, s: (0, 0)),
                  pl.BlockSpec((1, HG_DV), lambda b, s: (0, 0))],
        out_specs=pl.BlockSpec((cb, HG_W), lambda b, s: (b * nblk + s, 0)),
        out_shape=jax.ShapeDtypeStruct((t, HG_W), BF16),
        scratch_shapes=[pltpu.VMEM((HG_HEADS // 2, 2 * HG_DV, 2 * HG_DK), F32)],
        compiler_params=_params("parallel", "arbitrary"),
        name="hgrn2",
    )(z, z, z, z, lb_logits, onorm)


def _mla_prep_kernel(z_ref, qa_ref, kva_ref, wuq_ref, wukv_ref, qn_ref, kn_ref, cos_ref, sin_ref,
                     qt_out, k_out, vt_out):
    cq = _rms(z_ref[:, 0:MLA_Q_RANK], qa_ref[...]).astype(BF16)
    ckv = _rms(z_ref[:, MLA_Q_RANK:MLA_Q_RANK + MLA_KV_RANK], kva_ref[...]).astype(BF16)
    kpe = z_ref[:, MLA_Q_RANK + MLA_KV_RANK:]
    qf = _dot(cq, wuq_ref[...])
    kvf = _dot(ckv, wukv_ref[...])
    cos, sin = cos_ref[...], sin_ref[...]
    qn, kn = qn_ref[...], kn_ref[...]
    scale = MLA_QK ** -0.5 * LOG2E
    tm = z_ref.shape[0]

    def rope(x):
        return x * cos + pltpu.roll(x, 64, 1) * sin

    kpe_ss = jnp.sum(kpe * kpe, axis=-1, keepdims=True)
    k_rope = rope(kpe * kn[:, MLA_NOPE:])
    ones_rows = jnp.where(lax.broadcasted_iota(jnp.int32, (VT_ROWS - MLA_V, tm), 0) == 0, 1.0, 0.0)
    for h in range(MLA_HEADS):
        c0 = h * MLA_PAD
        qh = qf[:, c0:c0 + MLA_PAD]
        rq = lax.rsqrt(jnp.sum(qh * qh, axis=-1, keepdims=True) * (1.0 / MLA_QK) + EPS) * scale
        qh = qh * rq * qn
        qt_out[h, 0:MLA_NOPE, :] = qh[:, :MLA_NOPE].T.astype(BF16)
        qt_out[h, MLA_NOPE:MLA_PAD, :] = rope(qh[:, MLA_NOPE:]).T.astype(BF16)
        kh = kvf[:, c0:c0 + MLA_NOPE]
        rk = lax.rsqrt((jnp.sum(kh * kh, axis=-1, keepdims=True) + kpe_ss) * (1.0 / MLA_QK) + EPS)
        k_out[:, c0:c0 + MLA_NOPE] = (kh * rk * kn[:, :MLA_NOPE]).astype(BF16)
        k_out[:, c0 + MLA_NOPE:c0 + MLA_PAD] = (k_rope * rk).astype(BF16)
        vt_out[h, 0:MLA_V, :] = kvf[:, c0 + MLA_NOPE:c0 + MLA_PAD].T.astype(BF16)
        vt_out[h, MLA_V:VT_ROWS, :] = ones_rows.astype(BF16)


def mla_prep(z, qa, kva, wuq, wukv, qn, kn, cos, sin, tm):
    t, zw = z.shape
    tm = _tile(t, tm)
    full = lambda a: pl.BlockSpec(a.shape, lambda i: (0, 0))
    rows = lambda w: pl.BlockSpec((tm, w), lambda i: (i, 0))
    cols = lambda r: pl.BlockSpec((MLA_HEADS, r, tm), lambda i: (0, 0, i))
    return pl.pallas_call(
        _mla_prep_kernel,
        grid=(t // tm,),
        in_specs=[rows(zw), full(qa), full(kva), full(wuq), full(wukv), full(qn), full(kn),
                  rows(128), rows(128)],
        out_specs=[cols(MLA_PAD), rows(MLA_HEADS * MLA_PAD), cols(VT_ROWS)],
        out_shape=[jax.ShapeDtypeStruct((MLA_HEADS, MLA_PAD, t), BF16),
                   jax.ShapeDtypeStruct((t, MLA_HEADS * MLA_PAD), BF16),
                   jax.ShapeDtypeStruct((MLA_HEADS, VT_ROWS, t), BF16)],
        compiler_params=_params("parallel"),
        name="mla_prep",
    )(z, qa, kva, wuq, wukv, qn, kn, cos, sin)


def _flash_kernel(qt_ref, k_ref, vt_ref, o_ref, acc_ref, *, tq, tk, tk_main):
    qi = pl.program_id(2)
    qt = qt_ref[0]
    acc_ref[...] = jnp.zeros(acc_ref.shape, F32)

    def step(k0, kw, m_all, q_lo, diagonal):
        s = _dot(k_ref[pl.ds(k0, kw), :], qt[:, q_lo:])
        if diagonal:
            row = lax.broadcasted_iota(jnp.int32, s.shape, 0)
            col = lax.broadcasted_iota(jnp.int32, s.shape, 1)
            s = jnp.where(row <= col, s, -jnp.inf)
        m_prev = m_all[:, q_lo:]
        m_new = jnp.maximum(m_prev, jnp.max(s, axis=0, keepdims=True))
        alpha = jnp.exp2(m_prev - m_new)
        p = jnp.exp2(s - m_new).astype(BF16)
        acc_ref[:, q_lo:] = alpha * acc_ref[:, q_lo:] + _dot(vt_ref[0, :, pl.ds(k0, kw)], p)
        return m_new if q_lo == 0 else jnp.concatenate([m_all[:, :q_lo], m_new], axis=1)

    m = lax.fori_loop(0, qi * (tq // tk_main),
                      lambda j, m: step(pl.multiple_of(j * tk_main, tk_main), tk_main, m, 0, False),
                      jnp.full((1, tq), -jnp.inf, F32))
    for d in range(tq // tk):
        m = step(pl.multiple_of(qi * tq + d * tk, tk), tk, m, d * tk, True)
    acc = acc_ref[...]
    o_ref[...] = (acc[0:MLA_V, :] / acc[MLA_V:MLA_V + 1, :]).T.astype(o_ref.dtype)


def flash_attention(qt, k, vt, batch, seq, tq, tk, tk_main):
    t = k.shape[0]
    tq = _tile(seq, tq)
    tk = _tile(tq, tk)
    tk_main = _tile(tq, tk_main)
    nq = seq // tq
    return pl.pallas_call(
        functools.partial(_flash_kernel, tq=tq, tk=tk, tk_main=tk_main),
        grid=(batch, MLA_HEADS, nq),
        in_specs=[pl.BlockSpec((1, MLA_PAD, tq), lambda b, h, i: (h, 0, b * nq + i)),
                  pl.BlockSpec((seq, MLA_PAD), lambda b, h, i: (b, h)),
                  pl.BlockSpec((1, VT_ROWS, seq), lambda b, h, i: (h, 0, b))],
        out_specs=pl.BlockSpec((tq, MLA_V), lambda b, h, i: (b * nq + i, h)),
        out_shape=jax.ShapeDtypeStruct((t, MLA_HEADS * MLA_V), BF16),
        scratch_shapes=[pltpu.VMEM((VT_ROWS, tq), F32)],
        compiler_params=_params("parallel", "parallel", "arbitrary"),
        name="mla_flash",
    )(qt, k, vt)


def _ret_kernel(q_ref, k_ref, v_ref, g_ref, o_ref, r_ref):
    c = RET_CHUNK
    hf = pl.program_id(1).astype(F32)

    @pl.when(pl.program_id(2) == 0)
    def _():
        r_ref[...] = jnp.zeros(r_ref.shape, F32)

    def log_gamma(shape):
        return jnp.log(1.0 - jnp.exp2(-5.0 - jnp.full(shape, hf, F32)))

    row = lax.broadcasted_iota(jnp.int32, (c, c), 0)
    col = lax.broadcasted_iota(jnp.int32, (c, c), 1)
    diff = (row - col).astype(F32)
    d_intra = jnp.where(diff >= 0, jnp.exp(jnp.where(diff >= 0, diff, 0.0) * log_gamma((c, c))), 0.0)
    idx = lax.broadcasted_iota(jnp.int32, (c, RET_DK), 0).astype(F32)
    lg_k = log_gamma((c, RET_DK))
    q_dec = jnp.exp((idx + 1.0) * lg_k).astype(BF16)
    k_dec = jnp.exp((c - 1.0 - idx) * lg_k).astype(BF16)
    c_dec = jnp.exp(c * log_gamma((1, RET_DV)))

    def chunk(ci, carry):
        rows = pl.ds(pl.multiple_of(ci * c, c), c)
        q, k, v = q_ref[rows, :], k_ref[rows, :], v_ref[rows, :]
        a = lax.dot_general(q, k, NT, preferred_element_type=F32) * d_intra
        r = r_ref[...]
        out = _dot(a.astype(BF16), v) + _dot(q * q_dec, r.astype(BF16))
        r_ref[...] = r * c_dec + lax.dot_general(k * k_dec, v, TN, preferred_element_type=F32)
        g = g_ref[rows, :].astype(F32)
        o_ref[rows, :] = (_rms(out) * (g * jax.nn.sigmoid(g))).astype(o_ref.dtype)
        return carry

    lax.fori_loop(0, q_ref.shape[0] // c, chunk, 0, unroll=2)


def retention(z, batch, seq, cb):
    t = z.shape[0]
    cb = _tile(seq, cb)
    assert cb % RET_CHUNK == 0
    nblk = seq // cb
    nqk = RET_HEADS * RET_DK // RET_DK
    nv = (2 * RET_HEADS * RET_DK) // RET_DV
    return pl.pallas_call(
        _ret_kernel,
        grid=(batch, RET_HEADS, nblk),
        in_specs=[pl.BlockSpec((cb, RET_DK), lambda b, h, s: (b * nblk + s, h)),
                  pl.BlockSpec((cb, RET_DK), lambda b, h, s: (b * nblk + s, nqk + h)),
                  pl.BlockSpec((cb, RET_DV), lambda b, h, s: (b * nblk + s, nv + h)),
                  pl.BlockSpec((cb, RET_DV), lambda b, h, s: (b * nblk + s, nv + RET_HEADS + h))],
        out_specs=pl.BlockSpec((cb, RET_DV), lambda b, h, s: (b * nblk + s, h)),
        out_shape=jax.ShapeDtypeStruct((t, RET_HEADS * RET_DV), BF16),
        scratch_shapes=[pltpu.VMEM((RET_DK, RET_DV), F32)],
        compiler_params=_params("parallel", "parallel", "arbitrary"),
        name="retention",
    )(z, z, z, z)


def _rope_pad(a, axis):
    x1, x2 = jnp.split(a, 2, axis=axis)
    z = jnp.zeros_like(x1)
    return jnp.concatenate([x1, z, x2, z], axis=axis)


def _pad_qk_gain(g):
    return jnp.concatenate([g[:MLA_NOPE], _rope_pad(g[MLA_NOPE:], 0)])[None, :]


def kernel(x, p, positions, norm_mix, norm_ffn, norm_ple, e_w_in, e_lb_logits, e_q_a_norm, e_kv_a_norm, e_w_uq, e_w_ukv, e_q_norm, e_k_norm, e_hg_onorm, e_w_out, o_w_in, o_w_out, ffn_w_gate, ffn_w_up, ffn_conv_w, ffn_conv_b, ffn_w_down, ple_w_proj, ple_w_gate):
    batch, seq, d = x.shape
    t = batch * seq
    h = x.reshape(t, d)
    pos_col = positions.reshape(t, 1)

    ones = jnp.ones((1, 128), F32)
    f_ret = (ROPE_BASE ** (-jnp.arange(RET_DK // 2, dtype=F32) / (RET_DK // 2)))[None, :]
    f_mla = ROPE_BASE ** (-jnp.arange(MLA_ROPE // 2, dtype=F32) / (MLA_ROPE // 2))
    f_mla = _rope_pad(jnp.concatenate([f_mla, f_mla]), 0)[None, :]
    sgn_mla = _rope_pad(jnp.concatenate([-jnp.ones(32, F32), jnp.ones(32, F32)]), 0)[None, :]
    cos_ret, sin_ret = rope_tables(pos_col, f_ret, ones, "rope_tab_ret")
    cos_mla, sin_mla = rope_tables(pos_col, f_mla, sgn_mla, "rope_tab_mla")

    w_in = e_w_in[0]
    w_hg = cast_bf16(e_w_in, 0, 4 * HG_W, "cast_w_hgrn")
    off = 4 * HG_W + MLA_Q_RANK + MLA_KV_RANK
    w_mla = jnp.concatenate([w_in[:, 4 * HG_W:off], _rope_pad(w_in[:, off:], 1)], axis=1).astype(BF16)
    wuq = e_w_uq[0].reshape(MLA_Q_RANK, MLA_HEADS, MLA_QK)
    wuq = jnp.concatenate([wuq[..., :MLA_NOPE], _rope_pad(wuq[..., MLA_NOPE:], 2)], axis=-1)
    wuq = wuq.reshape(MLA_Q_RANK, MLA_HEADS * MLA_PAD).astype(BF16)
    wukv = e_w_ukv[0].astype(BF16)
    g_mix0 = norm_mix[0][None, :]

    z_hg = norm_matmul(h, g_mix0, w_hg, F32, 1024, 1024, "in_proj_hgrn")
    z_mla = norm_matmul(h, g_mix0, w_mla, F32, 1024, w_mla.shape[1], "in_proj_mla")
    o_a = hgrn(z_hg, e_lb_logits, e_hg_onorm[0][None, :], batch, seq, 512)
    qt, k, vt = mla_prep(z_mla, e_q_a_norm[0][None, :], e_kv_a_norm[0][None, :], wuq, wukv,
                       _pad_qk_gain(e_q_norm[0]), _pad_qk_gain(e_k_norm[0]), cos_mla, sin_mla, 512)
    o_b = flash_attention(qt, k, vt, batch, seq, 2048, 512, 1024)
    w_out = e_w_out[0].astype(BF16)
    h = res_matmul(h, [o_a, o_b], [w_out[:HG_W], w_out[HG_W:]], 1024, 1024, "out_proj_even")

    def channel_mix(h, i):
        h = conv_ffn(h, norm_ffn[i][None, :], cast_bf16(ffn_w_gate, i, name="cast_w_gate"),
                     cast_bf16(ffn_w_up, i, name="cast_w_up"), ffn_conv_w[i], ffn_conv_b[i][None, :],
                     cast_bf16(ffn_w_down, i, name="cast_w_down"), seq, 1024, 512)
        return ple(h, norm_ple[i][None, :], cast_bf16(ple_w_gate, i, name="cast_w_ple"),
                   p[i].reshape(t, PLE_DIM), ple_w_proj[i].astype(BF16), 1024, 1024)

    h = channel_mix(h, 0)

    z_ret = ret_in_proj(h, norm_mix[1][None, :], cast_bf16(o_w_in, 0, name="cast_w_ret_in"),
                        cos_ret, sin_ret, 1024, 2048)
    o_r = retention(z_ret, batch, seq, 2048)
    h = res_matmul(h, [o_r], [cast_bf16(o_w_out, 0, name="cast_w_ret_out")], 1024, 1024, "out_proj_odd")
    h = channel_mix(h, 1)
    return h.reshape(batch, seq, d)
```

```python
import functools

import jax
import jax.numpy as jnp
from jax import lax
from jax.experimental import pallas as pl
from jax.experimental.pallas import tpu as pltpu

F32 = jnp.float32
BF16 = jnp.bfloat16

D_MODEL = 2048
PLE_DIM = 256
HG_HEADS = 8
HG_DK = 128
HG_DV = 128
HG_W = HG_HEADS * HG_DK
HG_CHUNK = 64
MLA_HEADS = 8
MLA_Q_RANK = 512
MLA_KV_RANK = 512
MLA_NOPE = 128
MLA_ROPE = 64
MLA_V = 128
MLA_QK = MLA_NOPE + MLA_ROPE
MLA_PAD = 256
VT_ROWS = MLA_V + 16
LOG2E = 1.4426950408889634
RET_HEADS = 8
RET_DK = 256
RET_DV = 512
RET_CHUNK = 256
D_FF = 5632
FFN_STRIP = 256
ROPE_BASE = 10000.0
EPS = 1e-6

VMEM_LIMIT = 60 * 2**20
CAST_BLOCK_BYTES = 8 * 2**20

NT = (((1,), (1,)), ((), ()))
TN = (((0,), (0,)), ((), ()))


def _params(*sem):
    return pltpu.CompilerParams(dimension_semantics=sem, vmem_limit_bytes=VMEM_LIMIT)


def _tile(n, pref):
    t = min(n, pref)
    assert n % t == 0, (n, pref)
    return t


def _dot(a, b):
    return jnp.dot(a, b, preferred_element_type=F32)


def _rms(x, g=None, n=None):
    n = x.shape[-1] if n is None else n
    y = x * lax.rsqrt(jnp.sum(x * x, axis=-1, keepdims=True) * (1.0 / n) + EPS)
    return y if g is None else y * g


def _norm_into(x_ref, g_ref, dst_ref, rows):
    g = g_ref[...]

    def body(c, carry):
        r0 = pl.multiple_of(c * rows, rows)
        dst_ref[pl.ds(r0, rows), :] = _rms(x_ref[pl.ds(r0, rows), :], g).astype(dst_ref.dtype)
        return carry

    lax.fori_loop(0, x_ref.shape[0] // rows, body, 0)


def _cast_kernel(w_ref, o_ref):
    o_ref[...] = w_ref[...].astype(o_ref.dtype)


def cast_bf16(w, layer, ncols=None, name="cast_bf16"):
    _, r, c = w.shape
    ncols = c if ncols is None else ncols
    tr = r
    while tr % 16 == 0 and tr * ncols * 4 > CAST_BLOCK_BYTES:
        tr //= 2
    return pl.pallas_call(
        _cast_kernel,
        grid=(r // tr,),
        in_specs=[pl.BlockSpec((None, tr, ncols), lambda i: (layer, i, 0))],
        out_specs=pl.BlockSpec((tr, ncols), lambda i: (i, 0)),
        out_shape=jax.ShapeDtypeStruct((r, ncols), BF16),
        compiler_params=_params("parallel"),
        name=name,
    )(w)


def _rope_tab_kernel(pos_ref, f_ref, sgn_ref, c_ref, s_ref):
    ang = pos_ref[...].astype(F32) * f_ref[...]
    c_ref[...] = jnp.cos(ang)
    s_ref[...] = jnp.sin(ang) * sgn_ref[...]


def rope_tables(pos_col, freq, sgn, name):
    t = pos_col.shape[0]
    tm = _tile(t, 1024)
    return pl.pallas_call(
        _rope_tab_kernel,
        grid=(t // tm,),
        in_specs=[pl.BlockSpec((tm, 1), lambda i: (i, 0)),
                  pl.BlockSpec((1, 128), lambda i: (0, 0)),
                  pl.BlockSpec((1, 128), lambda i: (0, 0))],
        out_specs=[pl.BlockSpec((tm, 128), lambda i: (i, 0)),
                   pl.BlockSpec((tm, 128), lambda i: (i, 0))],
        out_shape=[jax.ShapeDtypeStruct((t, 128), F32)] * 2,
        compiler_params=_params("parallel"),
        name=name,
    )(pos_col, freq, sgn)


def _norm_matmul_kernel(x_ref, g_ref, w_ref, o_ref, xn_ref):
    @pl.when(pl.program_id(1) == 0)
    def _():
        _norm_into(x_ref, g_ref, xn_ref, 256)

    o_ref[...] = _dot(xn_ref[...], w_ref[...]).astype(o_ref.dtype)


def norm_matmul(x, g, w, out_dtype, tm, tn, name):
    t, k = x.shape
    n = w.shape[1]
    tm, tn = _tile(t, tm), _tile(n, tn)
    return pl.pallas_call(
        _norm_matmul_kernel,
        grid=(t // tm, n // tn),
        in_specs=[pl.BlockSpec((tm, k), lambda i, j: (i, 0)),
                  pl.BlockSpec((1, k), lambda i, j: (0, 0)),
                  pl.BlockSpec((k, tn), lambda i, j: (0, j))],
        out_specs=pl.BlockSpec((tm, tn), lambda i, j: (i, j)),
        out_shape=jax.ShapeDtypeStruct((t, n), out_dtype),
        scratch_shapes=[pltpu.VMEM((tm, k), BF16)],
        compiler_params=_params("parallel", "arbitrary"),
        name=name,
    )(x, g, w)


def _ret_in_kernel(x_ref, g_ref, w_ref, cos_ref, sin_ref, o_ref, xn_ref, *, n_q_tiles, n_rope_tiles):
    j = pl.program_id(1)

    @pl.when(j == 0)
    def _():
        _norm_into(x_ref, g_ref, xn_ref, 256)

    acc = _dot(xn_ref[...], w_ref[...])
    is_rope = j < n_rope_tiles
    scale = jnp.where(j >= n_q_tiles, RET_DK ** -0.5, 1.0).astype(F32)
    cos = cos_ref[...] * scale
    sin = sin_ref[...] * scale
    half = RET_DK // 2
    for hh in range(acc.shape[1] // RET_DK):
        x1 = acc[:, hh * RET_DK:hh * RET_DK + half]
        x2 = acc[:, hh * RET_DK + half:(hh + 1) * RET_DK]
        o_ref[:, hh * RET_DK:hh * RET_DK + half] = jnp.where(
            is_rope, x1 * cos - x2 * sin, x1).astype(o_ref.dtype)
        o_ref[:, hh * RET_DK + half:(hh + 1) * RET_DK] = jnp.where(
            is_rope, x2 * cos + x1 * sin, x2).astype(o_ref.dtype)


def ret_in_proj(x, g, w, cos, sin, tm, tn):
    t, k = x.shape
    n = w.shape[1]
    tm, tn = _tile(t, tm), _tile(n, tn)
    qw = RET_HEADS * RET_DK
    kern = functools.partial(_ret_in_kernel, n_q_tiles=qw // tn, n_rope_tiles=2 * qw // tn)
    return pl.pallas_call(
        kern,
        grid=(t // tm, n // tn),
        in_specs=[pl.BlockSpec((tm, k), lambda i, j: (i, 0)),
                  pl.BlockSpec((1, k), lambda i, j: (0, 0)),
                  pl.BlockSpec((k, tn), lambda i, j: (0, j)),
                  pl.BlockSpec((tm, 128), lambda i, j: (i, 0)),
                  pl.BlockSpec((tm, 128), lambda i, j: (i, 0))],
        out_specs=pl.BlockSpec((tm, tn), lambda i, j: (i, j)),
        out_shape=jax.ShapeDtypeStruct((t, n), BF16),
        scratch_shapes=[pltpu.VMEM((tm, k), BF16)],
        compiler_params=_params("parallel", "arbitrary"),
        name="ret_in_proj",
    )(x, g, w, cos, sin)


def _res_matmul_kernel(*refs, n_lhs):
    h_ref, o_ref = refs[0], refs[-1]
    acc = h_ref[...]
    for a_ref, w_ref in zip(refs[1:1 + n_lhs], refs[1 + n_lhs:1 + 2 * n_lhs]):
        acc = acc + _dot(a_ref[...], w_ref[...])
    o_ref[...] = acc


def res_matmul(h, lhs, ws, tm, tn, name):
    t, n = h.shape
    tm, tn = _tile(t, tm), _tile(n, tn)
    in_specs = [pl.BlockSpec((tm, tn), lambda i, j: (i, j))]
    in_specs += [pl.BlockSpec((tm, a.shape[1]), lambda i, j: (i, 0)) for a in lhs]
    in_specs += [pl.BlockSpec((w.shape[0], tn), lambda i, j: (0, j)) for w in ws]
    return pl.pallas_call(
        functools.partial(_res_matmul_kernel, n_lhs=len(lhs)),
        grid=(t // tm, n // tn),
        in_specs=in_specs,
        out_specs=pl.BlockSpec((tm, tn), lambda i, j: (i, j)),
        out_shape=jax.ShapeDtypeStruct((t, n), F32),
        compiler_params=_params("parallel", "arbitrary"),
        name=name,
    )(h, *lhs, *ws)


def _ffn_kernel(h_ref, g_ref, wg_ref, wu_ref, cw_ref, cb_ref, wd_ref, o_ref,
                u_ref, gs_ref, tail_ref, *, blocks_per_seq):
    i, j = pl.program_id(0), pl.program_id(1)
    tm = h_ref.shape[0]

    @pl.when(j == 0)
    def _():
        _norm_into(h_ref, g_ref, u_ref, 256)
        o_ref[...] = h_ref[...]

    @pl.when(jnp.logical_and(i == 0, j == 0))
    def _():
        tail_ref[...] = jnp.zeros(tail_ref.shape, F32)

    u = u_ref[...]
    seq_start = (i % blocks_per_seq) == 0
    prev = jnp.where(seq_start, 0.0, tail_ref[j])
    cw, cb = cw_ref[...], cb_ref[...]
    contrib = None
    for c0 in range(0, wg_ref.shape[1], FFN_STRIP):
        cols = slice(c0, c0 + FFN_STRIP)
        a = _dot(u, wg_ref[:, cols])
        up = _dot(u, wu_ref[:, cols])
        gs_ref[0:8, cols] = prev[:, cols]
        gs_ref[8:8 + tm, cols] = a
        tail_ref[j, :, cols] = a[tm - 8:tm, :]
        c = (cb[:, cols] + cw[0:1, cols] * gs_ref[6:6 + tm, cols] + cw[1:2, cols] * gs_ref[7:7 + tm, cols]
             + cw[2:3, cols] * a)
        act = (c * jax.nn.sigmoid(c) * up).astype(BF16)
        part = _dot(act, wd_ref[cols, :])
        contrib = part if contrib is None else contrib + part
    o_ref[...] += contrib


def conv_ffn(h, g, wg, wu, cw, cb, wd, seq, tm, tf):
    t, d = h.shape
    f = wg.shape[1]
    tm, tf = _tile(seq, tm), _tile(f, tf)
    kern = functools.partial(_ffn_kernel, blocks_per_seq=seq // tm)
    return pl.pallas_call(
        kern,
        grid=(t // tm, f // tf),
        in_specs=[pl.BlockSpec((tm, d), lambda i, j: (i, 0)),
                  pl.BlockSpec((1, d), lambda i, j: (0, 0)),
                  pl.BlockSpec((d, tf), lambda i, j: (0, j)),
                  pl.BlockSpec((d, tf), lambda i, j: (0, j)),
                  pl.BlockSpec((3, tf), lambda i, j: (0, j)),
                  pl.BlockSpec((1, tf), lambda i, j: (0, j)),
                  pl.BlockSpec((tf, d), lambda i, j: (j, 0))],
        out_specs=pl.BlockSpec((tm, d), lambda i, j: (i, 0)),
        out_shape=jax.ShapeDtypeStruct((t, d), F32),
        scratch_shapes=[pltpu.VMEM((tm, d), BF16),
                        pltpu.VMEM((tm + 8, tf), F32),
                        pltpu.VMEM((f // tf, 8, tf), F32)],
        compiler_params=_params("arbitrary", "arbitrary"),
        name="conv_ffn",
    )(h, g, wg, wu, cw, cb, wd)


def _ple_kernel(h_ref, g_ref, wgate_ref, p_ref, wproj_ref, o_ref, xn_ref, pb_ref):
    j = pl.program_id(1)
    tn = o_ref.shape[1]

    @pl.when(j == 0)
    def _():
        _norm_into(h_ref, g_ref, xn_ref, 256)
        pb_ref[...] = p_ref[...].astype(BF16)

    gate = jax.nn.sigmoid(_dot(xn_ref[...], wgate_ref[...]))
    proj = _dot(pb_ref[...], wproj_ref[...])
    c0 = pl.multiple_of(j * tn, tn)
    o_ref[...] = h_ref[:, pl.ds(c0, tn)] + proj * gate


def ple(h, g, wgate, p, layer, wproj, tm, tn):
    t, d = h.shape
    tm, tn = _tile(t, tm), _tile(d, tn)
    return pl.pallas_call(
        _ple_kernel,
        grid=(t // tm, d // tn),
        in_specs=[pl.BlockSpec((tm, d), lambda i, j: (i, 0)),
                  pl.BlockSpec((1, d), lambda i, j: (0, 0)),
                  pl.BlockSpec((d, tn), lambda i, j: (0, j)),
                  pl.BlockSpec((None, tm, PLE_DIM), lambda i, j: (layer, i, 0)),
                  pl.BlockSpec((PLE_DIM, tn), lambda i, j: (0, j))],
        out_specs=pl.BlockSpec((tm, tn), lambda i, j: (i, j)),
        out_shape=jax.ShapeDtypeStruct((t, d), F32),
        scratch_shapes=[pltpu.VMEM((tm, d), BF16), pltpu.VMEM((tm, PLE_DIM), BF16)],
        compiler_params=_params("parallel", "arbitrary"),
        name="ple",
    )(h, g, wgate, p, wproj)


def _hgrn_kernel(q_ref, f_ref, i_ref, g_ref, lbl_ref, on_ref, o_ref, st_ref):
    c = HG_CHUNK

    @pl.when(pl.program_id(1) == 0)
    def _():
        st_ref[...] = jnp.zeros(st_ref.shape, F32)

    lg = lbl_ref[...]
    e = jnp.exp(lg - jnp.max(lg, axis=0, keepdims=True))
    lb = e[0:1, :] / jnp.sum(e, axis=0, keepdims=True)
    onorm = on_ref[...]
    pw = 2 * HG_DK
    row = lax.broadcasted_iota(jnp.int32, (c, c), 0)
    col = lax.broadcasted_iota(jnp.int32, (c, c), 1)
    tril = jnp.where(row >= col, 1.0, 0.0).astype(BF16)
    row2 = lax.broadcasted_iota(jnp.int32, (c, 2 * c), 0)
    col2 = lax.broadcasted_iota(jnp.int32, (c, 2 * c), 1)
    causal2 = row2 >= jnp.where(col2 >= c, col2 - c, col2)
    left = lax.broadcasted_iota(jnp.int32, (c, pw), 1) < HG_DK
    same_head = ((lax.broadcasted_iota(jnp.int32, (pw, pw), 0) < HG_DV)
                 == (lax.broadcasted_iota(jnp.int32, (pw, pw), 1) < HG_DK))

    def block_diag(x):
        zero = jnp.zeros_like(x)
        return jnp.concatenate([jnp.where(left, x, zero), jnp.where(left, zero, x)], axis=0)

    def chunk(ci, carry):
        rows = pl.ds(pl.multiple_of(ci * c, c), c)
        q = q_ref[rows, :]
        v = i_ref[rows, :].astype(BF16)
        f = lb + (1.0 - lb) * jax.nn.sigmoid(f_ref[rows, :])
        k = 1.0 - f
        lf = jnp.log(f)
        hi = lf.astype(BF16)
        r1 = lf - hi.astype(F32)
        mid = r1.astype(BF16)
        lo = (r1 - mid.astype(F32)).astype(BF16)
        bc = _dot(tril, hi) + _dot(tril, mid) + _dot(tril, lo)
        ref = bc[c // 2 - 1:c // 2, :]
        b_last = bc[c - 1:c, :]
        q_rel = (q * jnp.exp(bc - ref)).astype(BF16)
        k_rel = (k * jnp.exp(ref - bc)).astype(BF16)
        q_dec = (q * jnp.exp(bc)).astype(BF16)
        k_dec = (k * jnp.exp(b_last - bc)).astype(BF16)
        decay = jnp.exp(b_last)
        for p in range(HG_HEADS // 2):
            cols = slice(p * pw, (p + 1) * pw)
            a = lax.dot_general(q_rel[:, cols], block_diag(k_rel[:, cols]), NT, preferred_element_type=F32)
            a = jnp.where(causal2, a, 0.0).astype(BF16)
            st = st_ref[p]
            out = _dot(a, block_diag(v[:, cols])) + lax.dot_general(q_dec[:, cols], st.astype(BF16), NT,
                                                                    preferred_element_type=F32)
            upd = lax.dot_general(v[:, cols], k_dec[:, cols], TN, preferred_element_type=F32)
            st_ref[p] = st * decay[:, cols] + jnp.where(same_head, upd, 0.0)
            for hh in range(2):
                hc = slice(p * pw + hh * HG_DV, p * pw + (hh + 1) * HG_DV)
                g = g_ref[rows, hc]
                o_ref[rows, hc] = (_rms(out[:, hh * HG_DV:(hh + 1) * HG_DV], onorm)
                                   * (g * jax.nn.sigmoid(g))).astype(o_ref.dtype)
        return carry

    lax.fori_loop(0, q_ref.shape[0] // c, chunk, 0, unroll=2)


def hgrn(z, lb_logits, onorm, batch, seq, cb):
    t = z.shape[0]
    cb = _tile(seq, cb)
    nblk = seq // cb
    spec = lambda part: pl.BlockSpec((cb, HG_W), lambda b, s, part=part: (b * nblk + s, part))
    return pl.pallas_call(
        _hgrn_kernel,
        grid=(batch, nblk),
        in_specs=[spec(0), spec(1), spec(2), spec(3),
                  pl.BlockSpec(lb_logits.shape, lambda b, s: (0, 0)),
                  pl.BlockSpec((1, HG_DV), lambda b, s: (0, 0))],
        out_specs=pl.BlockSpec((cb, HG_W), lambda b, s: (b * nblk + s, 0)),
        out_shape=jax.ShapeDtypeStruct((t, HG_W), BF16),
        scratch_shapes=[pltpu.VMEM((HG_HEADS // 2, 2 * HG_DV, 2 * HG_DK), F32)],
        compiler_params=_params("parallel", "arbitrary"),
        name="hgrn2",
    )(z, z, z, z, lb_logits, onorm)


def _mla_prep_kernel(z_ref, qa_ref, kva_ref, wuq_ref, wukv_ref, qn_ref, kn_ref, cos_ref, sin_ref,
                     qt_out, k_out, vt_out):
    cq = _rms(z_ref[:, 0:MLA_Q_RANK], qa_ref[...]).astype(BF16)
    ckv = _rms(z_ref[:, MLA_Q_RANK:MLA_Q_RANK + MLA_KV_RANK], kva_ref[...]).astype(BF16)
    kpe = z_ref[:, MLA_Q_RANK + MLA_KV_RANK:]
    qf = _dot(cq, wuq_ref[...])
    kvf = _dot(ckv, wukv_ref[...])
    cos, sin = cos_ref[...], sin_ref[...]
    qn, kn = qn_ref[...], kn_ref[...]
    scale = MLA_QK ** -0.5 * LOG2E
    tm = z_ref.shape[0]

    def rope(x):
        return x * cos + pltpu.roll(x, 64, 1) * sin

    kpe_ss = jnp.sum(kpe * kpe, axis=-1, keepdims=True)
    k_rope = rope(kpe * kn[:, MLA_NOPE:])
    ones_rows = jnp.where(lax.broadcasted_iota(jnp.int32, (VT_ROWS - MLA_V, tm), 0) == 0, 1.0, 0.0)
    for h in range(MLA_HEADS):
        c0 = h * MLA_PAD
        qh = qf[:, c0:c0 + MLA_PAD]
        rq = lax.rsqrt(jnp.sum(qh * qh, axis=-1, keepdims=True) * (1.0 / MLA_QK) + EPS) * scale
        qh = qh * rq * qn
        qt_out[h, 0:MLA_NOPE, :] = qh[:, :MLA_NOPE].T.astype(BF16)
        qt_out[h, MLA_NOPE:MLA_PAD, :] = rope(qh[:, MLA_NOPE:]).T.astype(BF16)
        kh = kvf[:, c0:c0 + MLA_NOPE]
        rk = lax.rsqrt((jnp.sum(kh * kh, axis=-1, keepdims=True) + kpe_ss) * (1.0 / MLA_QK) + EPS)
        k_out[:, c0:c0 + MLA_NOPE] = (kh * rk * kn[:, :MLA_NOPE]).astype(BF16)
        k_out[:, c0 + MLA_NOPE:c0 + MLA_PAD] = (k_rope * rk).astype(BF16)
        vt_out[h, 0:MLA_V, :] = kvf[:, c0 + MLA_NOPE:c0 + MLA_PAD].T.astype(BF16)
        vt_out[h, MLA_V:VT_ROWS, :] = ones_rows.astype(BF16)


def mla_prep(z, qa, kva, wuq, wukv, qn, kn, cos, sin, tm):
    t, zw = z.shape
    tm = _tile(t, tm)
    full = lambda a: pl.BlockSpec(a.shape, lambda i: (0, 0))
    rows = lambda w: pl.BlockSpec((tm, w), lambda i: (i, 0))
    cols = lambda r: pl.BlockSpec((MLA_HEADS, r, tm), lambda i: (0, 0, i))
    return pl.pallas_call(
        _mla_prep_kernel,
        grid=(t // tm,),
        in_specs=[rows(zw), full(qa), full(kva), full(wuq), full(wukv), full(qn), full(kn),
                  rows(128), rows(128)],
        out_specs=[cols(MLA_PAD), rows(MLA_HEADS * MLA_PAD), cols(VT_ROWS)],
        out_shape=[jax.ShapeDtypeStruct((MLA_HEADS, MLA_PAD, t), BF16),
                   jax.ShapeDtypeStruct((t, MLA_HEADS * MLA_PAD), BF16),
                   jax.ShapeDtypeStruct((MLA_HEADS, VT_ROWS, t), BF16)],
        compiler_params=_params("parallel"),
        name="mla_prep",
    )(z, qa, kva, wuq, wukv, qn, kn, cos, sin)


def _flash_kernel(qt_ref, k_ref, vt_ref, o_ref, acc_ref, s_ref, *, tq, tk, tk_main):
    qi = pl.program_id(2)
    qt = qt_ref[0]
    acc_ref[...] = jnp.zeros(acc_ref.shape, F32)

    def scores(k0, kw, q_lo):
        return _dot(k_ref[pl.ds(k0, kw), :], qt[:, q_lo:])

    def update(s, k0, kw, m_all, q_lo):
        m_prev = m_all[:, q_lo:]
        m_new = jnp.maximum(m_prev, jnp.max(s, axis=0, keepdims=True))
        alpha = jnp.exp2(m_prev - m_new)
        p = jnp.exp2(s - m_new).astype(BF16)
        acc_ref[:, q_lo:] = alpha * acc_ref[:, q_lo:] + _dot(vt_ref[0, :, pl.ds(k0, kw)], p)
        return m_new if q_lo == 0 else jnp.concatenate([m_all[:, :q_lo], m_new], axis=1)

    n_pairs = qi
    s_ref[0] = scores(0, tk_main, 0)

    def pair(t, m):
        k0 = pl.multiple_of(t * (2 * tk_main), 2 * tk_main)
        s_ref[1] = scores(k0 + tk_main, tk_main, 0)
        m = update(s_ref[0], k0, tk_main, m, 0)
        s_ref[0] = scores(k0 + 2 * tk_main, tk_main, 0)
        return update(s_ref[1], k0 + tk_main, tk_main, m, 0)

    m = lax.fori_loop(0, n_pairs, pair, jnp.full((1, tq), -jnp.inf, F32))
    base = pl.multiple_of(qi * tq, tq)
    s_ref[1, :, tk_main:] = scores(base + tk_main, tk_main, tk_main)
    for d in range(tq // tk):
        e, r0, q_lo = (d * tk) // tk_main, (d * tk) % tk_main, d * tk
        s = s_ref[e, r0:r0 + tk, q_lo:]
        row = lax.broadcasted_iota(jnp.int32, s.shape, 0)
        col = lax.broadcasted_iota(jnp.int32, s.shape, 1)
        m = update(jnp.where(row <= col, s, -jnp.inf), base + q_lo, tk, m, q_lo)
    acc = acc_ref[...]
    o_ref[...] = (acc[0:MLA_V, :] / acc[MLA_V:MLA_V + 1, :]).T.astype(o_ref.dtype)


def flash_attention(qt, k, vt, batch, seq, tq, tk, tk_main):
    t = k.shape[0]
    tq = _tile(seq, tq)
    tk = _tile(tq, tk)
    tk_main = _tile(tq, tk_main)
    assert tq == 2 * tk_main and tk_main % tk == 0
    nq = seq // tq
    return pl.pallas_call(
        functools.partial(_flash_kernel, tq=tq, tk=tk, tk_main=tk_main),
        grid=(batch, MLA_HEADS, nq),
        in_specs=[pl.BlockSpec((1, MLA_PAD, tq), lambda b, h, i: (h, 0, b * nq + i)),
                  pl.BlockSpec((seq, MLA_PAD), lambda b, h, i: (b, h)),
                  pl.BlockSpec((1, VT_ROWS, seq), lambda b, h, i: (h, 0, b))],
        out_specs=pl.BlockSpec((tq, MLA_V), lambda b, h, i: (b * nq + i, h)),
        out_shape=jax.ShapeDtypeStruct((t, MLA_HEADS * MLA_V), BF16),
        scratch_shapes=[pltpu.VMEM((VT_ROWS, tq), F32), pltpu.VMEM((2, tk_main, tq), F32)],
        compiler_params=_params("parallel", "parallel", "arbitrary"),
        name="mla_flash",
    )(qt, k, vt)


def _ret_kernel(q_ref, k_ref, v_ref, g_ref, o_ref, r_ref):
    c = RET_CHUNK
    hf = pl.program_id(1).astype(F32)

    @pl.when(pl.program_id(2) == 0)
    def _():
        r_ref[...] = jnp.zeros(r_ref.shape, F32)

    def log_gamma(shape):
        return jnp.log(1.0 - jnp.exp2(-5.0 - jnp.full(shape, hf, F32)))

    row = lax.broadcasted_iota(jnp.int32, (c, c), 0)
    col = lax.broadcasted_iota(jnp.int32, (c, c), 1)
    diff = (row - col).astype(F32)
    d_intra = jnp.where(diff >= 0, jnp.exp(jnp.where(diff >= 0, diff, 0.0) * log_gamma((c, c))), 0.0)
    idx = lax.broadcasted_iota(jnp.int32, (c, RET_DK), 0).astype(F32)
    lg_k = log_gamma((c, RET_DK))
    q_dec = jnp.exp((idx + 1.0) * lg_k).astype(BF16)
    k_dec = jnp.exp((c - 1.0 - idx) * lg_k).astype(BF16)
    c_dec = jnp.exp(c * log_gamma((1, RET_DV)))

    def chunk(ci, carry):
        rows = pl.ds(pl.multiple_of(ci * c, c), c)
        q, k, v = q_ref[rows, :], k_ref[rows, :], v_ref[rows, :]
        a = lax.dot_general(q, k, NT, preferred_element_type=F32) * d_intra
        r = r_ref[...]
        out = _dot(a.astype(BF16), v) + _dot(q * q_dec, r.astype(BF16))
        r_ref[...] = r * c_dec + lax.dot_general(k * k_dec, v, TN, preferred_element_type=F32)
        g = g_ref[rows, :].astype(F32)
        o_ref[rows, :] = (_rms(out) * (g * jax.nn.sigmoid(g))).astype(o_ref.dtype)
        return carry

    lax.fori_loop(0, q_ref.shape[0] // c, chunk, 0, unroll=2)


def retention(z, batch, seq, cb):
    t = z.shape[0]
    cb = _tile(seq, cb)
    assert cb % RET_CHUNK == 0
    nblk = seq // cb
    nqk = RET_HEADS * RET_DK // RET_DK
    nv = (2 * RET_HEADS * RET_DK) // RET_DV
    return pl.pallas_call(
        _ret_kernel,
        grid=(batch, RET_HEADS, nblk),
        in_specs=[pl.BlockSpec((cb, RET_DK), lambda b, h, s: (b * nblk + s, h)),
                  pl.BlockSpec((cb, RET_DK), lambda b, h, s: (b * nblk + s, nqk + h)),
                  pl.BlockSpec((cb, RET_DV), lambda b, h, s: (b * nblk + s, nv + h)),
                  pl.BlockSpec((cb, RET_DV), lambda b, h, s: (b * nblk + s, nv + RET_HEADS + h))],
        out_specs=pl.BlockSpec((cb, RET_DV), lambda b, h, s: (b * nblk + s, h)),
        out_shape=jax.ShapeDtypeStruct((t, RET_HEADS * RET_DV), BF16),
        scratch_shapes=[pltpu.VMEM((RET_DK, RET_DV), F32)],
        compiler_params=_params("parallel", "parallel", "arbitrary"),
        name="retention",
    )(z, z, z, z)


def _rope_pad(a, axis):
    x1, x2 = jnp.split(a, 2, axis=axis)
    z = jnp.zeros_like(x1)
    return jnp.concatenate([x1, z, x2, z], axis=axis)


def _pad_qk_gain(g):
    return jnp.concatenate([g[:MLA_NOPE], _rope_pad(g[MLA_NOPE:], 0)])[None, :]


def kernel(x, p, positions, norm_mix, norm_ffn, norm_ple, e_w_in, e_lb_logits, e_q_a_norm, e_kv_a_norm, e_w_uq, e_w_ukv, e_q_norm, e_k_norm, e_hg_onorm, e_w_out, o_w_in, o_w_out, ffn_w_gate, ffn_w_up, ffn_conv_w, ffn_conv_b, ffn_w_down, ple_w_proj, ple_w_gate):
    batch, seq, d = x.shape
    t = batch * seq
    h = x.reshape(t, d)
    pos_col = positions.reshape(t, 1)

    ones = jnp.ones((1, 128), F32)
    f_ret = (ROPE_BASE ** (-jnp.arange(RET_DK // 2, dtype=F32) / (RET_DK // 2)))[None, :]
    f_mla = ROPE_BASE ** (-jnp.arange(MLA_ROPE // 2, dtype=F32) / (MLA_ROPE // 2))
    f_mla = _rope_pad(jnp.concatenate([f_mla, f_mla]), 0)[None, :]
    sgn_mla = _rope_pad(jnp.concatenate([-jnp.ones(32, F32), jnp.ones(32, F32)]), 0)[None, :]
    cos_ret, sin_ret = rope_tables(pos_col, f_ret, ones, "rope_tab_ret")
    cos_mla, sin_mla = rope_tables(pos_col, f_mla, sgn_mla, "rope_tab_mla")

    w_in = e_w_in[0]
    w_hg = cast_bf16(e_w_in, 0, 4 * HG_W, "cast_w_hgrn")
    off = 4 * HG_W + MLA_Q_RANK + MLA_KV_RANK
    w_mla = jnp.concatenate([w_in[:, 4 * HG_W:off], _rope_pad(w_in[:, off:], 1)], axis=1).astype(BF16)
    wuq = e_w_uq[0].reshape(MLA_Q_RANK, MLA_HEADS, MLA_QK)
    wuq = jnp.concatenate([wuq[..., :MLA_NOPE], _rope_pad(wuq[..., MLA_NOPE:], 2)], axis=-1)
    wuq = wuq.reshape(MLA_Q_RANK, MLA_HEADS * MLA_PAD).astype(BF16)
    wukv = e_w_ukv[0].astype(BF16)
    g_mix0 = norm_mix[0][None, :]

    z_hg = norm_matmul(h, g_mix0, w_hg, F32, 1024, 1024, "in_proj_hgrn")
    z_mla = norm_matmul(h, g_mix0, w_mla, F32, 1024, w_mla.shape[1], "in_proj_mla")
    o_a = hgrn(z_hg, e_lb_logits, e_hg_onorm[0][None, :], batch, seq, 512)
    qt, k, vt = mla_prep(z_mla, e_q_a_norm[0][None, :], e_kv_a_norm[0][None, :], wuq, wukv,
                       _pad_qk_gain(e_q_norm[0]), _pad_qk_gain(e_k_norm[0]), cos_mla, sin_mla, 512)
    o_b = flash_attention(qt, k, vt, batch, seq, 2048, 512, 1024)
    w_out = e_w_out[0].astype(BF16)
    h = res_matmul(h, [o_a, o_b], [w_out[:HG_W], w_out[HG_W:]], 1024, 1024, "out_proj_even")

    def channel_mix(h, i):
        h = conv_ffn(h, norm_ffn[i][None, :], cast_bf16(ffn_w_gate, i, name="cast_w_gate"),
                     cast_bf16(ffn_w_up, i, name="cast_w_up"), ffn_conv_w[i], ffn_conv_b[i][None, :],
                     cast_bf16(ffn_w_down, i, name="cast_w_down"), seq, 1024, 512)
        return ple(h, norm_ple[i][None, :], cast_bf16(ple_w_gate, i, name="cast_w_ple"),
                   p.reshape(p.shape[0], t, PLE_DIM), i, ple_w_proj[i].astype(BF16), 1024, 1024)

    h = channel_mix(h, 0)

    z_ret = ret_in_proj(h, norm_mix[1][None, :], cast_bf16(o_w_in, 0, name="cast_w_ret_in"),
                        cos_ret, sin_ret, 1024, 2048)
    o_r = retention(z_ret, batch, seq, 2048)
    h = res_matmul(h, [o_r], [cast_bf16(o_w_out, 0, name="cast_w_ret_out")], 1024, 1024, "out_proj_odd")
    h = channel_mix(h, 1)
    return h.reshape(batch, seq, d)
```

```python
import functools

import jax
import jax.numpy as jnp
from jax import lax
from jax.experimental import pallas as pl
from jax.experimental.pallas import tpu as pltpu

F32 = jnp.float32
BF16 = jnp.bfloat16

D_MODEL = 2048
PLE_DIM = 256
HG_HEADS = 8
HG_DK = 128
HG_DV = 128
HG_W = HG_HEADS * HG_DK
HG_CHUNK = 64
MLA_HEADS = 8
MLA_Q_RANK = 512
MLA_KV_RANK = 512
MLA_NOPE = 128
MLA_ROPE = 64
MLA_V = 128
MLA_QK = MLA_NOPE + MLA_ROPE
MLA_PAD = 256
VT_ROWS = MLA_V + 16
LOG2E = 1.4426950408889634
RET_HEADS = 8
RET_DK = 256
RET_DV = 512
RET_CHUNK = 256
D_FF = 5632
FFN_STRIP = 256
ROPE_BASE = 10000.0
EPS = 1e-6

VMEM_LIMIT = 60 * 2**20
CAST_BLOCK_BYTES = 8 * 2**20

NT = (((1,), (1,)), ((), ()))
TN = (((0,), (0,)), ((), ()))


def _params(*sem):
    return pltpu.CompilerParams(dimension_semantics=sem, vmem_limit_bytes=VMEM_LIMIT)


def _tile(n, pref):
    t = min(n, pref)
    assert n % t == 0, (n, pref)
    return t


def _dot(a, b):
    return jnp.dot(a, b, preferred_element_type=F32)


def _rms(x, g=None, n=None):
    n = x.shape[-1] if n is None else n
    y = x * lax.rsqrt(jnp.sum(x * x, axis=-1, keepdims=True) * (1.0 / n) + EPS)
    return y if g is None else y * g


def _norm_into(x_ref, g_ref, dst_ref, rows):
    g = g_ref[...]

    def body(c, carry):
        r0 = pl.multiple_of(c * rows, rows)
        dst_ref[pl.ds(r0, rows), :] = _rms(x_ref[pl.ds(r0, rows), :], g).astype(dst_ref.dtype)
        return carry

    lax.fori_loop(0, x_ref.shape[0] // rows, body, 0)


def _cast_kernel(w_ref, o_ref):
    o_ref[...] = w_ref[...].astype(o_ref.dtype)


def cast_bf16(w, layer, ncols=None, name="cast_bf16"):
    _, r, c = w.shape
    ncols = c if ncols is None else ncols
    tr = r
    while tr % 16 == 0 and tr * ncols * 4 > CAST_BLOCK_BYTES:
        tr //= 2
    return pl.pallas_call(
        _cast_kernel,
        grid=(r // tr,),
        in_specs=[pl.BlockSpec((None, tr, ncols), lambda i: (layer, i, 0))],
        out_specs=pl.BlockSpec((tr, ncols), lambda i: (i, 0)),
        out_shape=jax.ShapeDtypeStruct((r, ncols), BF16),
        compiler_params=_params("parallel"),
        name=name,
    )(w)


def _select_lanes(x, sel):
    hi = x.astype(BF16)
    r1 = x - hi.astype(F32)
    mid = r1.astype(BF16)
    lo = (r1 - mid.astype(F32)).astype(BF16)
    return _dot(hi, sel) + _dot(mid, sel) + _dot(lo, sel)


def _rope_tab_kernel(pos_ref, f_ref, selc_ref, sels_ref, c_ref, s_ref, cm_ref, sm_ref):
    ang = pos_ref[...].astype(F32) * f_ref[...]
    c, s = jnp.cos(ang), jnp.sin(ang)
    c_ref[...] = c
    s_ref[...] = s
    cm_ref[...] = _select_lanes(c, selc_ref[...])
    sm_ref[...] = _select_lanes(s, sels_ref[...])


def rope_tables(pos_col, freq, sel_cos, sel_sin):
    t = pos_col.shape[0]
    tm = _tile(t, 1024)
    const = lambda a: pl.BlockSpec(a.shape, lambda i: (0, 0))
    return pl.pallas_call(
        _rope_tab_kernel,
        grid=(t // tm,),
        in_specs=[pl.BlockSpec((tm, 1), lambda i: (i, 0)), const(freq), const(sel_cos), const(sel_sin)],
        out_specs=[pl.BlockSpec((tm, 128), lambda i: (i, 0))] * 4,
        out_shape=[jax.ShapeDtypeStruct((t, 128), F32)] * 4,
        compiler_params=_params("parallel"),
        name="rope_tables",
    )(pos_col, freq, sel_cos, sel_sin)


def _norm_matmul_kernel(x_ref, g_ref, w_ref, o_ref, xn_ref):
    @pl.when(pl.program_id(1) == 0)
    def _():
        _norm_into(x_ref, g_ref, xn_ref, 256)

    o_ref[...] = _dot(xn_ref[...], w_ref[...]).astype(o_ref.dtype)


def norm_matmul(x, g, w, out_dtype, tm, tn, name):
    t, k = x.shape
    n = w.shape[1]
    tm, tn = _tile(t, tm), _tile(n, tn)
    return pl.pallas_call(
        _norm_matmul_kernel,
        grid=(t // tm, n // tn),
        in_specs=[pl.BlockSpec((tm, k), lambda i, j: (i, 0)),
                  pl.BlockSpec((1, k), lambda i, j: (0, 0)),
                  pl.BlockSpec((k, tn), lambda i, j: (0, j))],
        out_specs=pl.BlockSpec((tm, tn), lambda i, j: (i, j)),
        out_shape=jax.ShapeDtypeStruct((t, n), out_dtype),
        scratch_shapes=[pltpu.VMEM((tm, k), BF16)],
        compiler_params=_params("parallel", "arbitrary"),
        name=name,
    )(x, g, w)


def _ret_in_kernel(x_ref, g_ref, w_ref, cos_ref, sin_ref, o_ref, xn_ref, *, n_q_tiles, n_rope_tiles):
    j = pl.program_id(1)

    @pl.when(j == 0)
    def _():
        _norm_into(x_ref, g_ref, xn_ref, 256)

    acc = _dot(xn_ref[...], w_ref[...])
    is_rope = j < n_rope_tiles
    scale = jnp.where(j >= n_q_tiles, RET_DK ** -0.5, 1.0).astype(F32)
    cos = cos_ref[...] * scale
    sin = sin_ref[...] * scale
    half = RET_DK // 2
    for hh in range(acc.shape[1] // RET_DK):
        x1 = acc[:, hh * RET_DK:hh * RET_DK + half]
        x2 = acc[:, hh * RET_DK + half:(hh + 1) * RET_DK]
        o_ref[:, hh * RET_DK:hh * RET_DK + half] = jnp.where(
            is_rope, x1 * cos - x2 * sin, x1).astype(o_ref.dtype)
        o_ref[:, hh * RET_DK + half:(hh + 1) * RET_DK] = jnp.where(
            is_rope, x2 * cos + x1 * sin, x2).astype(o_ref.dtype)


def ret_in_proj(x, g, w, cos, sin, tm, tn):
    t, k = x.shape
    n = w.shape[1]
    tm, tn = _tile(t, tm), _tile(n, tn)
    qw = RET_HEADS * RET_DK
    kern = functools.partial(_ret_in_kernel, n_q_tiles=qw // tn, n_rope_tiles=2 * qw // tn)
    return pl.pallas_call(
        kern,
        grid=(t // tm, n // tn),
        in_specs=[pl.BlockSpec((tm, k), lambda i, j: (i, 0)),
                  pl.BlockSpec((1, k), lambda i, j: (0, 0)),
                  pl.BlockSpec((k, tn), lambda i, j: (0, j)),
                  pl.BlockSpec((tm, 128), lambda i, j: (i, 0)),
                  pl.BlockSpec((tm, 128), lambda i, j: (i, 0))],
        out_specs=pl.BlockSpec((tm, tn), lambda i, j: (i, j)),
        out_shape=jax.ShapeDtypeStruct((t, n), BF16),
        scratch_shapes=[pltpu.VMEM((tm, k), BF16)],
        compiler_params=_params("parallel", "arbitrary"),
        name="ret_in_proj",
    )(x, g, w, cos, sin)


def _res_matmul_kernel(*refs):
    h_ref, w_ref, o_ref = refs[0], refs[-2], refs[-1]
    lhs = [a_ref[...] for a_ref in refs[1:-2]]
    lhs = lhs[0] if len(lhs) == 1 else jnp.concatenate(lhs, axis=1)
    o_ref[...] = h_ref[...] + _dot(lhs, w_ref[...])


def res_matmul(h, lhs, w, tm, tn, name):
    t, n = h.shape
    tm, tn = _tile(t, tm), _tile(n, tn)
    assert sum(a.shape[1] for a in lhs) == w.shape[0]
    in_specs = [pl.BlockSpec((tm, tn), lambda i, j: (i, j))]
    in_specs += [pl.BlockSpec((tm, a.shape[1]), lambda i, j: (i, 0)) for a in lhs]
    in_specs += [pl.BlockSpec((w.shape[0], tn), lambda i, j: (0, j),
                              pipeline_mode=pl.Buffered(1) if tn == n else None)]
    return pl.pallas_call(
        _res_matmul_kernel,
        grid=(t // tm, n // tn),
        in_specs=in_specs,
        out_specs=pl.BlockSpec((tm, tn), lambda i, j: (i, j)),
        out_shape=jax.ShapeDtypeStruct((t, n), F32),
        compiler_params=_params("parallel", "arbitrary"),
        name=name,
    )(h, *lhs, w)


def _ffn_kernel(h_ref, g_ref, wg_ref, wu_ref, cw_ref, cb_ref, wd_ref, o_ref,
                u_ref, gs_ref, tail_ref, *, blocks_per_seq):
    i, j = pl.program_id(0), pl.program_id(1)
    tm = h_ref.shape[0]

    @pl.when(j == 0)
    def _():
        _norm_into(h_ref, g_ref, u_ref, 256)
        o_ref[...] = h_ref[...]

    @pl.when(jnp.logical_and(i == 0, j == 0))
    def _():
        tail_ref[...] = jnp.zeros(tail_ref.shape, F32)

    u = u_ref[...]
    seq_start = (i % blocks_per_seq) == 0
    prev = jnp.where(seq_start, 0.0, tail_ref[j])
    cw, cb = cw_ref[...], cb_ref[...]
    contrib = None
    for c0 in range(0, wg_ref.shape[1], FFN_STRIP):
        cols = slice(c0, c0 + FFN_STRIP)
        a = _dot(u, wg_ref[:, cols])
        up = _dot(u, wu_ref[:, cols])
        gs_ref[0:8, cols] = prev[:, cols]
        gs_ref[8:8 + tm, cols] = a
        tail_ref[j, :, cols] = a[tm - 8:tm, :]
        c = (cb[:, cols] + cw[0:1, cols] * gs_ref[6:6 + tm, cols] + cw[1:2, cols] * gs_ref[7:7 + tm, cols]
             + cw[2:3, cols] * a)
        act = (c * jax.nn.sigmoid(c) * up).astype(BF16)
        part = _dot(act, wd_ref[cols, :])
        contrib = part if contrib is None else contrib + part
    o_ref[...] += contrib


def conv_ffn(h, g, wg, wu, cw, cb, wd, seq, tm, tf):
    t, d = h.shape
    f = wg.shape[1]
    tm, tf = _tile(seq, tm), _tile(f, tf)
    kern = functools.partial(_ffn_kernel, blocks_per_seq=seq // tm)
    return pl.pallas_call(
        kern,
        grid=(t // tm, f // tf),
        in_specs=[pl.BlockSpec((tm, d), lambda i, j: (i, 0)),
                  pl.BlockSpec((1, d), lambda i, j: (0, 0)),
                  pl.BlockSpec((d, tf), lambda i, j: (0, j)),
                  pl.BlockSpec((d, tf), lambda i, j: (0, j)),
                  pl.BlockSpec((3, tf), lambda i, j: (0, j)),
                  pl.BlockSpec((1, tf), lambda i, j: (0, j)),
                  pl.BlockSpec((tf, d), lambda i, j: (j, 0))],
        out_specs=pl.BlockSpec((tm, d), lambda i, j: (i, 0)),
        out_shape=jax.ShapeDtypeStruct((t, d), F32),
        scratch_shapes=[pltpu.VMEM((tm, d), BF16),
                        pltpu.VMEM((tm + 8, tf), F32),
                        pltpu.VMEM((f // tf, 8, tf), F32)],
        compiler_params=_params("arbitrary", "arbitrary"),
        name="conv_ffn",
    )(h, g, wg, wu, cw, cb, wd)


def _ple_kernel(h_ref, g_ref, wgate_ref, p_ref, wproj_ref, o_ref, xn_ref, pb_ref):
    j = pl.program_id(1)
    tn = o_ref.shape[1]

    @pl.when(j == 0)
    def _():
        _norm_into(h_ref, g_ref, xn_ref, 256)
        pb_ref[...] = p_ref[...].astype(BF16)

    gate = jax.nn.sigmoid(_dot(xn_ref[...], wgate_ref[...]))
    proj = _dot(pb_ref[...], wproj_ref[...])
    c0 = pl.multiple_of(j * tn, tn)
    o_ref[...] = h_ref[:, pl.ds(c0, tn)] + proj * gate


def ple(h, g, wgate, p, layer, wproj, tm, tn):
    t, d = h.shape
    tm, tn = _tile(t, tm), _tile(d, tn)
    return pl.pallas_call(
        _ple_kernel,
        grid=(t // tm, d // tn),
        in_specs=[pl.BlockSpec((tm, d), lambda i, j: (i, 0)),
                  pl.BlockSpec((1, d), lambda i, j: (0, 0)),
                  pl.BlockSpec((d, tn), lambda i, j: (0, j)),
                  pl.BlockSpec((None, tm, PLE_DIM), lambda i, j: (layer, i, 0)),
                  pl.BlockSpec((PLE_DIM, tn), lambda i, j: (0, j))],
        out_specs=pl.BlockSpec((tm, tn), lambda i, j: (i, j)),
        out_shape=jax.ShapeDtypeStruct((t, d), F32),
        scratch_shapes=[pltpu.VMEM((tm, d), BF16), pltpu.VMEM((tm, PLE_DIM), BF16)],
        compiler_params=_params("parallel", "arbitrary"),
        name="ple",
    )(h, g, wgate, p, wproj)


def _hgrn_kernel(q_ref, f_ref, i_ref, g_ref, lbl_ref, on_ref, o_ref, st_ref):
    c = HG_CHUNK

    @pl.when(pl.program_id(1) == 0)
    def _():
        st_ref[...] = jnp.zeros(st_ref.shape, F32)

    lg = lbl_ref[...]
    e = jnp.exp(lg - jnp.max(lg, axis=0, keepdims=True))
    lb = e[0:1, :] / jnp.sum(e, axis=0, keepdims=True)
    onorm = on_ref[...]
    pw = 2 * HG_DK
    row = lax.broadcasted_iota(jnp.int32, (c, c), 0)
    col = lax.broadcasted_iota(jnp.int32, (c, c), 1)
    tril = jnp.where(row >= col, 1.0, 0.0).astype(BF16)
    row2 = lax.broadcasted_iota(jnp.int32, (c, 2 * c), 0)
    col2 = lax.broadcasted_iota(jnp.int32, (c, 2 * c), 1)
    causal2 = row2 >= jnp.where(col2 >= c, col2 - c, col2)
    left = lax.broadcasted_iota(jnp.int32, (c, pw), 1) < HG_DK
    same_head = ((lax.broadcasted_iota(jnp.int32, (pw, pw), 0) < HG_DV)
                 == (lax.broadcasted_iota(jnp.int32, (pw, pw), 1) < HG_DK))

    def block_diag(x):
        zero = jnp.zeros_like(x)
        return jnp.concatenate([jnp.where(left, x, zero), jnp.where(left, zero, x)], axis=0)

    def chunk(ci, carry):
        rows = pl.ds(pl.multiple_of(ci * c, c), c)
        q = q_ref[rows, :]
        v = i_ref[rows, :].astype(BF16)
        f = lb + (1.0 - lb) * jax.nn.sigmoid(f_ref[rows, :])
        k = 1.0 - f
        lf = jnp.log(f)
        hi = lf.astype(BF16)
        r1 = lf - hi.astype(F32)
        mid = r1.astype(BF16)
        lo = (r1 - mid.astype(F32)).astype(BF16)
        bc = _dot(tril, hi) + _dot(tril, mid) + _dot(tril, lo)
        ref = bc[c // 2 - 1:c // 2, :]
        b_last = bc[c - 1:c, :]
        q_rel = (q * jnp.exp(bc - ref)).astype(BF16)
        k_rel = (k * jnp.exp(ref - bc)).astype(BF16)
        q_dec = (q * jnp.exp(bc)).astype(BF16)
        k_dec = (k * jnp.exp(b_last - bc)).astype(BF16)
        decay = jnp.exp(b_last)
        for p in range(HG_HEADS // 2):
            cols = slice(p * pw, (p + 1) * pw)
            a = lax.dot_general(q_rel[:, cols], block_diag(k_rel[:, cols]), NT, preferred_element_type=F32)
            a = jnp.where(causal2, a, 0.0).astype(BF16)
            st = st_ref[p]
            out = _dot(a, block_diag(v[:, cols])) + lax.dot_general(q_dec[:, cols], st.astype(BF16), NT,
                                                                    preferred_element_type=F32)
            upd = lax.dot_general(v[:, cols], k_dec[:, cols], TN, preferred_element_type=F32)
            st_ref[p] = st * decay[:, cols] + jnp.where(same_head, upd, 0.0)
            for hh in range(2):
                hc = slice(p * pw + hh * HG_DV, p * pw + (hh + 1) * HG_DV)
                g = g_ref[rows, hc]
                o_ref[rows, hc] = (_rms(out[:, hh * HG_DV:(hh + 1) * HG_DV], onorm)
                                   * (g * jax.nn.sigmoid(g))).astype(o_ref.dtype)
        return carry

    lax.fori_loop(0, q_ref.shape[0] // c, chunk, 0, unroll=2)


def hgrn(z, lb_logits, onorm, batch, seq, cb):
    t = z.shape[0]
    cb = _tile(seq, cb)
    nblk = seq // cb
    spec = lambda part: pl.BlockSpec((cb, HG_W), lambda b, s, part=part: (b * nblk + s, part))
    return pl.pallas_call(
        _hgrn_kernel,
        grid=(batch, nblk),
        in_specs=[spec(0), spec(1), spec(2), spec(3),
                  pl.BlockSpec(lb_logits.shape, lambda b, s: (0, 0)),
                  pl.BlockSpec((1, HG_DV), lambda b, s: (0, 0))],
        out_specs=pl.BlockSpec((cb, HG_W), lambda b, s: (b * nblk + s, 0)),
        out_shape=jax.ShapeDtypeStruct((t, HG_W), BF16),
        scratch_shapes=[pltpu.VMEM((HG_HEADS // 2, 2 * HG_DV, 2 * HG_DK), F32)],
        compiler_params=_params("parallel", "arbitrary"),
        name="hgrn2",
    )(z, z, z, z, lb_logits, onorm)


def _mla_prep_kernel(z_ref, qa_ref, kva_ref, wuq_ref, wukv_ref, qn_ref, kn_ref, cos_ref, sin_ref,
                     qt_out, k_out, vt_out):
    cq = _rms(z_ref[:, 0:MLA_Q_RANK], qa_ref[...]).astype(BF16)
    ckv = _rms(z_ref[:, MLA_Q_RANK:MLA_Q_RANK + MLA_KV_RANK], kva_ref[...]).astype(BF16)
    kpe = z_ref[:, MLA_Q_RANK + MLA_KV_RANK:]
    qf = _dot(cq, wuq_ref[...])
    kvf = _dot(ckv, wukv_ref[...])
    cos, sin = cos_ref[...], sin_ref[...]
    qn, kn = qn_ref[...], kn_ref[...]
    scale = MLA_QK ** -0.5 * LOG2E
    tm = z_ref.shape[0]

    def rope(x):
        return x * cos + pltpu.roll(x, 64, 1) * sin

    kpe_ss = jnp.sum(kpe * kpe, axis=-1, keepdims=True)
    k_rope = rope(kpe * kn[:, MLA_NOPE:])
    ones_rows = jnp.where(lax.broadcasted_iota(jnp.int32, (VT_ROWS - MLA_V, tm), 0) == 0, 1.0, 0.0)
    for h in range(MLA_HEADS):
        c0 = h * MLA_PAD
        qh = qf[:, c0:c0 + MLA_PAD]
        rq = lax.rsqrt(jnp.sum(qh * qh, axis=-1, keepdims=True) * (1.0 / MLA_QK) + EPS) * scale
        qh = qh * rq * qn
        qt_out[h, 0:MLA_NOPE, :] = qh[:, :MLA_NOPE].T.astype(BF16)
        qt_out[h, MLA_NOPE:MLA_PAD, :] = rope(qh[:, MLA_NOPE:]).T.astype(BF16)
        kh = kvf[:, c0:c0 + MLA_NOPE]
        rk = lax.rsqrt((jnp.sum(kh * kh, axis=-1, keepdims=True) + kpe_ss) * (1.0 / MLA_QK) + EPS)
        k_out[:, c0:c0 + MLA_NOPE] = (kh * rk * kn[:, :MLA_NOPE]).astype(BF16)
        k_out[:, c0 + MLA_NOPE:c0 + MLA_PAD] = (k_rope * rk).astype(BF16)
        vt_out[h, 0:MLA_V, :] = kvf[:, c0 + MLA_NOPE:c0 + MLA_PAD].T.astype(BF16)
        vt_out[h, MLA_V:VT_ROWS, :] = ones_rows.astype(BF16)


def mla_prep(z, qa, kva, wuq, wukv, qn, kn, cos, sin, tm):
    t, zw = z.shape
    tm = _tile(t, tm)
    full = lambda a: pl.BlockSpec(a.shape, lambda i: (0, 0))
    rows = lambda w: pl.BlockSpec((tm, w), lambda i: (i, 0))
    cols = lambda r: pl.BlockSpec((MLA_HEADS, r, tm), lambda i: (0, 0, i))
    return pl.pallas_call(
        _mla_prep_kernel,
        grid=(t // tm,),
        in_specs=[rows(zw), full(qa), full(kva), full(wuq), full(wukv), full(qn), full(kn),
                  rows(128), rows(128)],
        out_specs=[cols(MLA_PAD), rows(MLA_HEADS * MLA_PAD), cols(VT_ROWS)],
        out_shape=[jax.ShapeDtypeStruct((MLA_HEADS, MLA_PAD, t), BF16),
                   jax.ShapeDtypeStruct((t, MLA_HEADS * MLA_PAD), BF16),
                   jax.ShapeDtypeStruct((MLA_HEADS, VT_ROWS, t), BF16)],
        compiler_params=_params("parallel"),
        name="mla_prep",
    )(z, qa, kva, wuq, wukv, qn, kn, cos, sin)


def _flash_kernel(qt_ref, k_ref, vt_ref, o_ref, acc_ref, s_ref, *, tq, tk, tk_main):
    qi = pl.program_id(2)
    qt = qt_ref[0]
    acc_ref[...] = jnp.zeros(acc_ref.shape, F32)

    def scores(k0, kw, q_lo):
        return _dot(k_ref[pl.ds(k0, kw), :], qt[:, q_lo:])

    def update(s, k0, kw, m_all, q_lo):
        m_prev = m_all[:, q_lo:]
        m_new = jnp.maximum(m_prev, jnp.max(s, axis=0, keepdims=True))
        alpha = jnp.exp2(m_prev - m_new)
        p = jnp.exp2(s - m_new).astype(BF16)
        acc_ref[:, q_lo:] = alpha * acc_ref[:, q_lo:] + _dot(vt_ref[0, :, pl.ds(k0, kw)], p)
        return m_new if q_lo == 0 else jnp.concatenate([m_all[:, :q_lo], m_new], axis=1)

    n_pairs = qi
    s_ref[0] = scores(0, tk_main, 0)

    def pair(t, m):
        k0 = pl.multiple_of(t * (2 * tk_main), 2 * tk_main)
        s_ref[1] = scores(k0 + tk_main, tk_main, 0)
        m = update(s_ref[0], k0, tk_main, m, 0)
        s_ref[0] = scores(k0 + 2 * tk_main, tk_main, 0)
        return update(s_ref[1], k0 + tk_main, tk_main, m, 0)

    m = lax.fori_loop(0, n_pairs, pair, jnp.full((1, tq), -jnp.inf, F32))
    base = pl.multiple_of(qi * tq, tq)
    s_ref[1, :, tk_main:] = scores(base + tk_main, tk_main, tk_main)
    for d in range(tq // tk):
        e, r0, q_lo = (d * tk) // tk_main, (d * tk) % tk_main, d * tk
        s = s_ref[e, r0:r0 + tk, q_lo:]
        row = lax.broadcasted_iota(jnp.int32, s.shape, 0)
        col = lax.broadcasted_iota(jnp.int32, s.shape, 1)
        m = update(jnp.where(row <= col, s, -jnp.inf), base + q_lo, tk, m, q_lo)
    acc = acc_ref[...]
    o_ref[...] = (acc[0:MLA_V, :] / acc[MLA_V:MLA_V + 1, :]).T.astype(o_ref.dtype)


def flash_attention(qt, k, vt, batch, seq, tq, tk, tk_main):
    t = k.shape[0]
    tq = _tile(seq, tq)
    tk = _tile(tq, tk)
    tk_main = _tile(tq, tk_main)
    assert tq == 2 * tk_main and tk_main % tk == 0
    nq = seq // tq
    return pl.pallas_call(
        functools.partial(_flash_kernel, tq=tq, tk=tk, tk_main=tk_main),
        grid=(batch, MLA_HEADS, nq),
        in_specs=[pl.BlockSpec((1, MLA_PAD, tq), lambda b, h, i: (h, 0, b * nq + i)),
                  pl.BlockSpec((seq, MLA_PAD), lambda b, h, i: (b, h)),
                  pl.BlockSpec((1, VT_ROWS, seq), lambda b, h, i: (h, 0, b))],
        out_specs=pl.BlockSpec((tq, MLA_V), lambda b, h, i: (b * nq + i, h)),
        out_shape=jax.ShapeDtypeStruct((t, MLA_HEADS * MLA_V), BF16),
        scratch_shapes=[pltpu.VMEM((VT_ROWS, tq), F32), pltpu.VMEM((2, tk_main, tq), F32)],
        compiler_params=_params("parallel", "parallel", "arbitrary"),
        name="mla_flash",
    )(qt, k, vt)


def _ret_kernel(q_ref, k_ref, v_ref, g_ref, o_ref, r_ref):
    c = RET_CHUNK
    hf = pl.program_id(1).astype(F32)

    @pl.when(pl.program_id(2) == 0)
    def _():
        r_ref[...] = jnp.zeros(r_ref.shape, F32)

    def log_gamma(shape):
        return jnp.log(1.0 - jnp.exp2(-5.0 - jnp.full(shape, hf, F32)))

    row = lax.broadcasted_iota(jnp.int32, (c, c), 0)
    col = lax.broadcasted_iota(jnp.int32, (c, c), 1)
    diff = (row - col).astype(F32)
    d_intra = jnp.where(diff >= 0, jnp.exp(jnp.where(diff >= 0, diff, 0.0) * log_gamma((c, c))), 0.0)
    idx = lax.broadcasted_iota(jnp.int32, (c, RET_DK), 0).astype(F32)
    lg_k = log_gamma((c, RET_DK))
    q_dec = jnp.exp((idx + 1.0) * lg_k).astype(BF16)
    k_dec = jnp.exp((c - 1.0 - idx) * lg_k).astype(BF16)
    c_dec = jnp.exp(c * log_gamma((1, RET_DV)))

    def chunk(ci, carry):
        rows = pl.ds(pl.multiple_of(ci * c, c), c)
        q, k, v = q_ref[rows, :], k_ref[rows, :], v_ref[rows, :]
        a = lax.dot_general(q, k, NT, preferred_element_type=F32) * d_intra
        r = r_ref[...]
        out = _dot(a.astype(BF16), v) + _dot(q * q_dec, r.astype(BF16))
        r_ref[...] = r * c_dec + lax.dot_general(k * k_dec, v, TN, preferred_element_type=F32)
        g = g_ref[rows, :].astype(F32)
        o_ref[rows, :] = (_rms(out) * (g * jax.nn.sigmoid(g))).astype(o_ref.dtype)
        return carry

    lax.fori_loop(0, q_ref.shape[0] // c, chunk, 0, unroll=2)


def retention(z, batch, seq, cb):
    t = z.shape[0]
    cb = _tile(seq, cb)
    assert cb % RET_CHUNK == 0
    nblk = seq // cb
    nqk = RET_HEADS * RET_DK // RET_DK
    nv = (2 * RET_HEADS * RET_DK) // RET_DV
    return pl.pallas_call(
        _ret_kernel,
        grid=(batch, RET_HEADS, nblk),
        in_specs=[pl.BlockSpec((cb, RET_DK), lambda b, h, s: (b * nblk + s, h)),
                  pl.BlockSpec((cb, RET_DK), lambda b, h, s: (b * nblk + s, nqk + h)),
                  pl.BlockSpec((cb, RET_DV), lambda b, h, s: (b * nblk + s, nv + h)),
                  pl.BlockSpec((cb, RET_DV), lambda b, h, s: (b * nblk + s, nv + RET_HEADS + h))],
        out_specs=pl.BlockSpec((cb, RET_DV), lambda b, h, s: (b * nblk + s, h)),
        out_shape=jax.ShapeDtypeStruct((t, RET_HEADS * RET_DV), BF16),
        scratch_shapes=[pltpu.VMEM((RET_DK, RET_DV), F32)],
        compiler_params=_params("parallel", "parallel", "arbitrary"),
        name="retention",
    )(z, z, z, z)


def _rope_pad(a, axis):
    x1, x2 = jnp.split(a, 2, axis=axis)
    z = jnp.zeros_like(x1)
    return jnp.concatenate([x1, z, x2, z], axis=axis)


def _pad_qk_gain(g):
    return jnp.concatenate([g[:MLA_NOPE], _rope_pad(g[MLA_NOPE:], 0)])[None, :]


def kernel(x, p, positions, norm_mix, norm_ffn, norm_ple, e_w_in, e_lb_logits, e_q_a_norm, e_kv_a_norm, e_w_uq, e_w_ukv, e_q_norm, e_k_norm, e_hg_onorm, e_w_out, o_w_in, o_w_out, ffn_w_gate, ffn_w_up, ffn_conv_w, ffn_conv_b, ffn_w_down, ple_w_proj, ple_w_gate):
    batch, seq, d = x.shape
    t = batch * seq
    h = x.reshape(t, d)
    pos_col = positions.reshape(t, 1)

    f_ret = (ROPE_BASE ** (-jnp.arange(RET_DK // 2, dtype=F32) / (RET_DK // 2)))[None, :]
    n_mla, stride = MLA_ROPE // 2, (RET_DK // 2) // (MLA_ROPE // 2)
    src = jnp.arange(n_mla) * stride
    sel = jnp.zeros((RET_DK // 2, 128), F32)
    sel_cos = sel.at[src, jnp.arange(n_mla)].set(1.0).at[src, 64 + jnp.arange(n_mla)].set(1.0)
    sel_sin = sel.at[src, jnp.arange(n_mla)].set(-1.0).at[src, 64 + jnp.arange(n_mla)].set(1.0)
    cos_ret, sin_ret, cos_mla, sin_mla = rope_tables(pos_col, f_ret, sel_cos.astype(BF16),
                                                     sel_sin.astype(BF16))

    w_in = e_w_in[0]
    w_hg = cast_bf16(e_w_in, 0, 4 * HG_W, "cast_w_hgrn")
    off = 4 * HG_W + MLA_Q_RANK + MLA_KV_RANK
    w_mla = jnp.concatenate([w_in[:, 4 * HG_W:off], _rope_pad(w_in[:, off:], 1)], axis=1).astype(BF16)
    wuq = e_w_uq[0].reshape(MLA_Q_RANK, MLA_HEADS, MLA_QK)
    wuq = jnp.concatenate([wuq[..., :MLA_NOPE], _rope_pad(wuq[..., MLA_NOPE:], 2)], axis=-1)
    wuq = wuq.reshape(MLA_Q_RANK, MLA_HEADS * MLA_PAD).astype(BF16)
    wukv = e_w_ukv[0].astype(BF16)
    g_mix0 = norm_mix[0][None, :]

    z_hg = norm_matmul(h, g_mix0, w_hg, F32, 1024, 2048, "in_proj_hgrn")
    z_mla = norm_matmul(h, g_mix0, w_mla, F32, 1024, w_mla.shape[1], "in_proj_mla")
    o_a = hgrn(z_hg, e_lb_logits, e_hg_onorm[0][None, :], batch, seq, 512)
    qt, k, vt = mla_prep(z_mla, e_q_a_norm[0][None, :], e_kv_a_norm[0][None, :], wuq, wukv,
                       _pad_qk_gain(e_q_norm[0]), _pad_qk_gain(e_k_norm[0]), cos_mla, sin_mla, 512)
    o_b = flash_attention(qt, k, vt, batch, seq, 2048, 512, 1024)
    h = res_matmul(h, [o_a, o_b], cast_bf16(e_w_out, 0, name="cast_w_even_out"), 1024, 2048,
                   "out_proj_even")

    def channel_mix(h, i):
        h = conv_ffn(h, norm_ffn[i][None, :], cast_bf16(ffn_w_gate, i, name="cast_w_gate"),
                     cast_bf16(ffn_w_up, i, name="cast_w_up"), ffn_conv_w[i], ffn_conv_b[i][None, :],
                     cast_bf16(ffn_w_down, i, name="cast_w_down"), seq, 1024, 512)
        return ple(h, norm_ple[i][None, :], cast_bf16(ple_w_gate, i, name="cast_w_ple"),
                   p.reshape(p.shape[0], t, PLE_DIM), i, ple_w_proj[i].astype(BF16), 1024, 2048)

    h = channel_mix(h, 0)

    z_ret = ret_in_proj(h, norm_mix[1][None, :], cast_bf16(o_w_in, 0, name="cast_w_ret_in"),
                        cos_ret, sin_ret, 1024, 2048)
    o_r = retention(z_ret, batch, seq, 2048)
    h = res_matmul(h, [o_r], cast_bf16(o_w_out, 0, name="cast_w_ret_out"), 1024, 1024, "out_proj_odd")
    h = channel_mix(h, 1)
    return h.reshape(batch, seq, d)
```

```python
import functools

import jax
import jax.numpy as jnp
from jax import lax
from jax.experimental import pallas as pl
from jax.experimental.pallas import tpu as pltpu

F32 = jnp.float32
BF16 = jnp.bfloat16

D_MODEL = 2048
PLE_DIM = 256
HG_HEADS = 8
HG_DK = 128
HG_DV = 128
HG_W = HG_HEADS * HG_DK
HG_CHUNK = 64
MLA_HEADS = 8
MLA_Q_RANK = 512
MLA_KV_RANK = 512
MLA_NOPE = 128
MLA_ROPE = 64
MLA_V = 128
MLA_QK = MLA_NOPE + MLA_ROPE
MLA_PAD = 256
VT_ROWS = MLA_V + 16
LOG2E = 1.4426950408889634
RET_HEADS = 8
RET_DK = 256
RET_DV = 512
RET_CHUNK = 256
D_FF = 5632
FFN_STRIP = 256
ROPE_BASE = 10000.0
EPS = 1e-6

VMEM_LIMIT = 60 * 2**20
CAST_BLOCK_BYTES = 8 * 2**20

NT = (((1,), (1,)), ((), ()))
TN = (((0,), (0,)), ((), ()))


def _params(*sem):
    return pltpu.CompilerParams(dimension_semantics=sem, vmem_limit_bytes=VMEM_LIMIT)


def _tile(n, pref):
    t = min(n, pref)
    assert n % t == 0, (n, pref)
    return t


def _dot(a, b):
    return jnp.dot(a, b, preferred_element_type=F32)


def _rms(x, g=None, n=None):
    n = x.shape[-1] if n is None else n
    y = x * lax.rsqrt(jnp.sum(x * x, axis=-1, keepdims=True) * (1.0 / n) + EPS)
    return y if g is None else y * g


def _norm_into(x_ref, g_ref, dst_ref, rows):
    g = g_ref[...]

    def body(c, carry):
        r0 = pl.multiple_of(c * rows, rows)
        dst_ref[pl.ds(r0, rows), :] = _rms(x_ref[pl.ds(r0, rows), :], g).astype(dst_ref.dtype)
        return carry

    lax.fori_loop(0, x_ref.shape[0] // rows, body, 0)


def _cast_kernel(w_ref, o_ref):
    o_ref[...] = w_ref[...].astype(o_ref.dtype)


def cast_bf16(w, layer, ncols=None, name="cast_bf16"):
    _, r, c = w.shape
    ncols = c if ncols is None else ncols
    tr = r
    while tr % 16 == 0 and tr * ncols * 4 > CAST_BLOCK_BYTES:
        tr //= 2
    return pl.pallas_call(
        _cast_kernel,
        grid=(r // tr,),
        in_specs=[pl.BlockSpec((None, tr, ncols), lambda i: (layer, i, 0))],
        out_specs=pl.BlockSpec((tr, ncols), lambda i: (i, 0)),
        out_shape=jax.ShapeDtypeStruct((r, ncols), BF16),
        compiler_params=_params("parallel"),
        name=name,
    )(w)


def _select_lanes(x, sel):
    hi = x.astype(BF16)
    r1 = x - hi.astype(F32)
    mid = r1.astype(BF16)
    lo = (r1 - mid.astype(F32)).astype(BF16)
    return _dot(hi, sel) + _dot(mid, sel) + _dot(lo, sel)


def _rope_tab_kernel(pos_ref, f_ref, selc_ref, sels_ref, c_ref, s_ref, cm_ref, sm_ref):
    ang = pos_ref[...].astype(F32) * f_ref[...]
    c, s = jnp.cos(ang), jnp.sin(ang)
    c_ref[...] = c
    s_ref[...] = s
    cm_ref[...] = _select_lanes(c, selc_ref[...])
    sm_ref[...] = _select_lanes(s, sels_ref[...])


def rope_tables(pos_col, freq, sel_cos, sel_sin):
    t = pos_col.shape[0]
    tm = _tile(t, 1024)
    const = lambda a: pl.BlockSpec(a.shape, lambda i: (0, 0))
    return pl.pallas_call(
        _rope_tab_kernel,
        grid=(t // tm,),
        in_specs=[pl.BlockSpec((tm, 1), lambda i: (i, 0)), const(freq), const(sel_cos), const(sel_sin)],
        out_specs=[pl.BlockSpec((tm, 128), lambda i: (i, 0))] * 4,
        out_shape=[jax.ShapeDtypeStruct((t, 128), F32)] * 4,
        compiler_params=_params("parallel"),
        name="rope_tables",
    )(pos_col, freq, sel_cos, sel_sin)


def _norm_matmul_kernel(x_ref, g_ref, w_ref, o_ref, xn_ref):
    j = pl.program_id(1)
    tn = o_ref.shape[1]

    @pl.when(j == 0)
    def _():
        _norm_into(x_ref, g_ref, xn_ref, 256)

    w = w_ref[:, pl.ds(pl.multiple_of(j * tn, tn), tn)]
    o_ref[...] = _dot(xn_ref[...], w).astype(o_ref.dtype)


def norm_matmul(x, g, w, out_dtype, tm, tn, name):
    t, k = x.shape
    n = w.shape[1]
    tm, tn = _tile(t, tm), _tile(n, tn)
    return pl.pallas_call(
        _norm_matmul_kernel,
        grid=(t // tm, n // tn),
        in_specs=[pl.BlockSpec((tm, k), lambda i, j: (i, 0)),
                  pl.BlockSpec((1, k), lambda i, j: (0, 0)),
                  pl.BlockSpec((k, n), lambda i, j: (0, 0), pipeline_mode=pl.Buffered(1))],
        out_specs=pl.BlockSpec((tm, tn), lambda i, j: (i, j)),
        out_shape=jax.ShapeDtypeStruct((t, n), out_dtype),
        scratch_shapes=[pltpu.VMEM((tm, k), BF16)],
        compiler_params=_params("parallel", "arbitrary"),
        name=name,
    )(x, g, w)


def _ret_in_kernel(x_ref, g_ref, w_ref, cos_ref, sin_ref, o_ref, xn_ref, *, n_q_tiles, n_rope_tiles):
    j = pl.program_id(1)

    @pl.when(j == 0)
    def _():
        _norm_into(x_ref, g_ref, xn_ref, 256)

    acc = _dot(xn_ref[...], w_ref[...])
    is_rope = j < n_rope_tiles
    scale = jnp.where(j >= n_q_tiles, RET_DK ** -0.5, 1.0).astype(F32)
    cos = cos_ref[...] * scale
    sin = sin_ref[...] * scale
    half = RET_DK // 2
    for hh in range(acc.shape[1] // RET_DK):
        x1 = acc[:, hh * RET_DK:hh * RET_DK + half]
        x2 = acc[:, hh * RET_DK + half:(hh + 1) * RET_DK]
        o_ref[:, hh * RET_DK:hh * RET_DK + half] = jnp.where(
            is_rope, x1 * cos - x2 * sin, x1).astype(o_ref.dtype)
        o_ref[:, hh * RET_DK + half:(hh + 1) * RET_DK] = jnp.where(
            is_rope, x2 * cos + x1 * sin, x2).astype(o_ref.dtype)


def ret_in_proj(x, g, w, cos, sin, tm, tn):
    t, k = x.shape
    n = w.shape[1]
    tm, tn = _tile(t, tm), _tile(n, tn)
    qw = RET_HEADS * RET_DK
    kern = functools.partial(_ret_in_kernel, n_q_tiles=qw // tn, n_rope_tiles=2 * qw // tn)
    return pl.pallas_call(
        kern,
        grid=(t // tm, n // tn),
        in_specs=[pl.BlockSpec((tm, k), lambda i, j: (i, 0)),
                  pl.BlockSpec((1, k), lambda i, j: (0, 0)),
                  pl.BlockSpec((k, tn), lambda i, j: (0, j)),
                  pl.BlockSpec((tm, 128), lambda i, j: (i, 0)),
                  pl.BlockSpec((tm, 128), lambda i, j: (i, 0))],
        out_specs=pl.BlockSpec((tm, tn), lambda i, j: (i, j)),
        out_shape=jax.ShapeDtypeStruct((t, n), BF16),
        scratch_shapes=[pltpu.VMEM((tm, k), BF16)],
        compiler_params=_params("parallel", "arbitrary"),
        name="ret_in_proj",
    )(x, g, w, cos, sin)


def _res_matmul_kernel(*refs):
    h_ref, w_ref, o_ref = refs[0], refs[-2], refs[-1]
    lhs = [a_ref[...] for a_ref in refs[1:-2]]
    lhs = lhs[0] if len(lhs) == 1 else jnp.concatenate(lhs, axis=1)
    o_ref[...] = h_ref[...] + _dot(lhs, w_ref[...])


def res_matmul(h, lhs, w, tm, tn, name):
    t, n = h.shape
    tm, tn = _tile(t, tm), _tile(n, tn)
    assert sum(a.shape[1] for a in lhs) == w.shape[0]
    in_specs = [pl.BlockSpec((tm, tn), lambda i, j: (i, j))]
    in_specs += [pl.BlockSpec((tm, a.shape[1]), lambda i, j: (i, 0)) for a in lhs]
    in_specs += [pl.BlockSpec((w.shape[0], tn), lambda i, j: (0, j),
                              pipeline_mode=pl.Buffered(1) if tn == n else None)]
    return pl.pallas_call(
        _res_matmul_kernel,
        grid=(t // tm, n // tn),
        in_specs=in_specs,
        out_specs=pl.BlockSpec((tm, tn), lambda i, j: (i, j)),
        out_shape=jax.ShapeDtypeStruct((t, n), F32),
        compiler_params=_params("parallel", "arbitrary"),
        name=name,
    )(h, *lhs, w)


def _ffn_kernel(h_ref, g_ref, wg_ref, wu_ref, cw_ref, cb_ref, wd_ref, o_ref,
                u_ref, gs_ref, tail_ref, *, blocks_per_seq):
    i, j = pl.program_id(0), pl.program_id(1)
    tm = h_ref.shape[0]

    @pl.when(j == 0)
    def _():
        _norm_into(h_ref, g_ref, u_ref, 256)
        o_ref[...] = h_ref[...]

    @pl.when(jnp.logical_and(i == 0, j == 0))
    def _():
        tail_ref[...] = jnp.zeros(tail_ref.shape, F32)

    u = u_ref[...]
    seq_start = (i % blocks_per_seq) == 0
    prev = jnp.where(seq_start, 0.0, tail_ref[j])
    cw, cb = cw_ref[...], cb_ref[...]
    contrib = None
    for c0 in range(0, wg_ref.shape[1], FFN_STRIP):
        cols = slice(c0, c0 + FFN_STRIP)
        a = _dot(u, wg_ref[:, cols])
        up = _dot(u, wu_ref[:, cols])
        gs_ref[0:8, cols] = prev[:, cols]
        gs_ref[8:8 + tm, cols] = a
        tail_ref[j, :, cols] = a[tm - 8:tm, :]
        c = (cb[:, cols] + cw[0:1, cols] * gs_ref[6:6 + tm, cols] + cw[1:2, cols] * gs_ref[7:7 + tm, cols]
             + cw[2:3, cols] * a)
        act = (c * jax.nn.sigmoid(c) * up).astype(BF16)
        part = _dot(act, wd_ref[cols, :])
        contrib = part if contrib is None else contrib + part
    o_ref[...] += contrib


def conv_ffn(h, g, wg, wu, cw, cb, wd, seq, tm, tf):
    t, d = h.shape
    f = wg.shape[1]
    tm, tf = _tile(seq, tm), _tile(f, tf)
    kern = functools.partial(_ffn_kernel, blocks_per_seq=seq // tm)
    return pl.pallas_call(
        kern,
        grid=(t // tm, f // tf),
        in_specs=[pl.BlockSpec((tm, d), lambda i, j: (i, 0)),
                  pl.BlockSpec((1, d), lambda i, j: (0, 0)),
                  pl.BlockSpec((d, tf), lambda i, j: (0, j)),
                  pl.BlockSpec((d, tf), lambda i, j: (0, j)),
                  pl.BlockSpec((3, tf), lambda i, j: (0, j)),
                  pl.BlockSpec((1, tf), lambda i, j: (0, j)),
                  pl.BlockSpec((tf, d), lambda i, j: (j, 0))],
        out_specs=pl.BlockSpec((tm, d), lambda i, j: (i, 0)),
        out_shape=jax.ShapeDtypeStruct((t, d), F32),
        scratch_shapes=[pltpu.VMEM((tm, d), BF16),
                        pltpu.VMEM((tm + 8, tf), F32),
                        pltpu.VMEM((f // tf, 8, tf), F32)],
        compiler_params=_params("arbitrary", "arbitrary"),
        name="conv_ffn",
    )(h, g, wg, wu, cw, cb, wd)


def _ple_kernel(h_ref, g_ref, wgate_ref, p_ref, wproj_ref, o_ref, xn_ref, pb_ref):
    j = pl.program_id(1)
    tn = o_ref.shape[1]

    @pl.when(j == 0)
    def _():
        _norm_into(h_ref, g_ref, xn_ref, 256)
        pb_ref[...] = p_ref[...].astype(BF16)

    gate = jax.nn.sigmoid(_dot(xn_ref[...], wgate_ref[...]))
    proj = _dot(pb_ref[...], wproj_ref[...])
    c0 = pl.multiple_of(j * tn, tn)
    o_ref[...] = h_ref[:, pl.ds(c0, tn)] + proj * gate


def ple(h, g, wgate, p, layer, wproj, tm, tn):
    t, d = h.shape
    tm, tn = _tile(t, tm), _tile(d, tn)
    return pl.pallas_call(
        _ple_kernel,
        grid=(t // tm, d // tn),
        in_specs=[pl.BlockSpec((tm, d), lambda i, j: (i, 0)),
                  pl.BlockSpec((1, d), lambda i, j: (0, 0)),
                  pl.BlockSpec((d, tn), lambda i, j: (0, j)),
                  pl.BlockSpec((None, tm, PLE_DIM), lambda i, j: (layer, i, 0)),
                  pl.BlockSpec((PLE_DIM, tn), lambda i, j: (0, j))],
        out_specs=pl.BlockSpec((tm, tn), lambda i, j: (i, j)),
        out_shape=jax.ShapeDtypeStruct((t, d), F32),
        scratch_shapes=[pltpu.VMEM((tm, d), BF16), pltpu.VMEM((tm, PLE_DIM), BF16)],
        compiler_params=_params("parallel", "arbitrary"),
        name="ple",
    )(h, g, wgate, p, wproj)


def _hgrn_kernel(q_ref, f_ref, i_ref, g_ref, lbl_ref, on_ref, o_ref, st_ref):
    c = HG_CHUNK

    @pl.when(pl.program_id(1) == 0)
    def _():
        st_ref[...] = jnp.zeros(st_ref.shape, F32)

    lg = lbl_ref[...]
    e = jnp.exp(lg - jnp.max(lg, axis=0, keepdims=True))
    lb = e[0:1, :] / jnp.sum(e, axis=0, keepdims=True)
    onorm = on_ref[...]
    pw = 2 * HG_DK
    row = lax.broadcasted_iota(jnp.int32, (c, c), 0)
    col = lax.broadcasted_iota(jnp.int32, (c, c), 1)
    tril = jnp.where(row >= col, 1.0, 0.0).astype(BF16)
    row2 = lax.broadcasted_iota(jnp.int32, (c, 2 * c), 0)
    col2 = lax.broadcasted_iota(jnp.int32, (c, 2 * c), 1)
    causal2 = row2 >= jnp.where(col2 >= c, col2 - c, col2)
    left = lax.broadcasted_iota(jnp.int32, (c, pw), 1) < HG_DK
    same_head = ((lax.broadcasted_iota(jnp.int32, (pw, pw), 0) < HG_DV)
                 == (lax.broadcasted_iota(jnp.int32, (pw, pw), 1) < HG_DK))

    def block_diag(x):
        zero = jnp.zeros_like(x)
        return jnp.concatenate([jnp.where(left, x, zero), jnp.where(left, zero, x)], axis=0)

    def chunk(ci, carry):
        rows = pl.ds(pl.multiple_of(ci * c, c), c)
        q = q_ref[rows, :]
        v = i_ref[rows, :].astype(BF16)
        f = lb + (1.0 - lb) * jax.nn.sigmoid(f_ref[rows, :])
        k = 1.0 - f
        lf = jnp.log(f)
        hi = lf.astype(BF16)
        r1 = lf - hi.astype(F32)
        mid = r1.astype(BF16)
        lo = (r1 - mid.astype(F32)).astype(BF16)
        bc = _dot(tril, hi) + _dot(tril, mid) + _dot(tril, lo)
        ref = bc[c // 2 - 1:c // 2, :]
        b_last = bc[c - 1:c, :]
        q_rel = (q * jnp.exp(bc - ref)).astype(BF16)
        k_rel = (k * jnp.exp(ref - bc)).astype(BF16)
        q_dec = (q * jnp.exp(bc)).astype(BF16)
        k_dec = (k * jnp.exp(b_last - bc)).astype(BF16)
        decay = jnp.exp(b_last)
        for p in range(HG_HEADS // 2):
            cols = slice(p * pw, (p + 1) * pw)
            a = lax.dot_general(q_rel[:, cols], block_diag(k_rel[:, cols]), NT, preferred_element_type=F32)
            a = jnp.where(causal2, a, 0.0).astype(BF16)
            st = st_ref[p]
            out = _dot(a, block_diag(v[:, cols])) + lax.dot_general(q_dec[:, cols], st.astype(BF16), NT,
                                                                    preferred_element_type=F32)
            upd = lax.dot_general(v[:, cols], k_dec[:, cols], TN, preferred_element_type=F32)
            st_ref[p] = st * decay[:, cols] + jnp.where(same_head, upd, 0.0)
            for hh in range(2):
                hc = slice(p * pw + hh * HG_DV, p * pw + (hh + 1) * HG_DV)
                g = g_ref[rows, hc]
                o_ref[rows, hc] = (_rms(out[:, hh * HG_DV:(hh + 1) * HG_DV], onorm)
                                   * (g * jax.nn.sigmoid(g))).astype(o_ref.dtype)
        return carry

    lax.fori_loop(0, q_ref.shape[0] // c, chunk, 0, unroll=2)


def hgrn(z, lb_logits, onorm, batch, seq, cb):
    t = z.shape[0]
    cb = _tile(seq, cb)
    nblk = seq // cb
    spec = lambda part: pl.BlockSpec((cb, HG_W), lambda b, s, part=part: (b * nblk + s, part))
    return pl.pallas_call(
        _hgrn_kernel,
        grid=(batch, nblk),
        in_specs=[spec(0), spec(1), spec(2), spec(3),
                  pl.BlockSpec(lb_logits.shape, lambda b, s: (0, 0)),
                  pl.BlockSpec((1, HG_DV), lambda b, s: (0, 0))],
        out_specs=pl.BlockSpec((cb, HG_W), lambda b, s: (b * nblk + s, 0)),
        out_shape=jax.ShapeDtypeStruct((t, HG_W), BF16),
        scratch_shapes=[pltpu.VMEM((HG_HEADS // 2, 2 * HG_DV, 2 * HG_DK), F32)],
        compiler_params=_params("parallel", "arbitrary"),
        name="hgrn2",
    )(z, z, z, z, lb_logits, onorm)


def _mla_prep_kernel(z_ref, qa_ref, kva_ref, wuq_ref, wukv_ref, qn_ref, kn_ref, cos_ref, sin_ref,
                     qt_out, k_out, vt_out):
    cq = _rms(z_ref[:, 0:MLA_Q_RANK], qa_ref[...]).astype(BF16)
    ckv = _rms(z_ref[:, MLA_Q_RANK:MLA_Q_RANK + MLA_KV_RANK], kva_ref[...]).astype(BF16)
    kpe = z_ref[:, MLA_Q_RANK + MLA_KV_RANK:]
    qf = _dot(cq, wuq_ref[...])
    kvf = _dot(ckv, wukv_ref[...])
    cos, sin = cos_ref[...], sin_ref[...]
    qn, kn = qn_ref[...], kn_ref[...]
    scale = MLA_QK ** -0.5 * LOG2E
    tm = z_ref.shape[0]

    def rope(x):
        return x * cos + pltpu.roll(x, 64, 1) * sin

    kpe_ss = jnp.sum(kpe * kpe, axis=-1, keepdims=True)
    k_rope = rope(kpe * kn[:, MLA_NOPE:])
    ones_rows = jnp.where(lax.broadcasted_iota(jnp.int32, (VT_ROWS - MLA_V, tm), 0) == 0, 1.0, 0.0)
    for h in range(MLA_HEADS):
        c0 = h * MLA_PAD
        qh = qf[:, c0:c0 + MLA_PAD]
        rq = lax.rsqrt(jnp.sum(qh * qh, axis=-1, keepdims=True) * (1.0 / MLA_QK) + EPS) * scale
        qh = qh * rq * qn
        qt_out[h, 0:MLA_NOPE, :] = qh[:, :MLA_NOPE].T.astype(BF16)
        qt_out[h, MLA_NOPE:MLA_PAD, :] = rope(qh[:, MLA_NOPE:]).T.astype(BF16)
        kh = kvf[:, c0:c0 + MLA_NOPE]
        rk = lax.rsqrt((jnp.sum(kh * kh, axis=-1, keepdims=True) + kpe_ss) * (1.0 / MLA_QK) + EPS)
        k_out[:, c0:c0 + MLA_NOPE] = (kh * rk * kn[:, :MLA_NOPE]).astype(BF16)
        k_out[:, c0 + MLA_NOPE:c0 + MLA_PAD] = (k_rope * rk).astype(BF16)
        vt_out[h, 0:MLA_V, :] = kvf[:, c0 + MLA_NOPE:c0 + MLA_PAD].T.astype(BF16)
        vt_out[h, MLA_V:VT_ROWS, :] = ones_rows.astype(BF16)


def mla_prep(z, qa, kva, wuq, wukv, qn, kn, cos, sin, tm):
    t, zw = z.shape
    tm = _tile(t, tm)
    full = lambda a: pl.BlockSpec(a.shape, lambda i: (0, 0))
    rows = lambda w: pl.BlockSpec((tm, w), lambda i: (i, 0))
    cols = lambda r: pl.BlockSpec((MLA_HEADS, r, tm), lambda i: (0, 0, i))
    return pl.pallas_call(
        _mla_prep_kernel,
        grid=(t // tm,),
        in_specs=[rows(zw), full(qa), full(kva), full(wuq), full(wukv), full(qn), full(kn),
                  rows(128), rows(128)],
        out_specs=[cols(MLA_PAD), rows(MLA_HEADS * MLA_PAD), cols(VT_ROWS)],
        out_shape=[jax.ShapeDtypeStruct((MLA_HEADS, MLA_PAD, t), BF16),
                   jax.ShapeDtypeStruct((t, MLA_HEADS * MLA_PAD), BF16),
                   jax.ShapeDtypeStruct((MLA_HEADS, VT_ROWS, t), BF16)],
        compiler_params=_params("parallel"),
        name="mla_prep",
    )(z, qa, kva, wuq, wukv, qn, kn, cos, sin)


def _flash_kernel(qt_ref, k_ref, vt_ref, o_ref, acc_ref, s_ref, *, tq, tk, tk_main):
    qi = pl.program_id(2)
    qt = qt_ref[0]
    acc_ref[...] = jnp.zeros(acc_ref.shape, F32)

    def scores(k0, kw, q_lo):
        return _dot(k_ref[pl.ds(k0, kw), :], qt[:, q_lo:])

    def update(s, k0, kw, m_all, q_lo):
        m_prev = m_all[:, q_lo:]
        m_new = jnp.maximum(m_prev, jnp.max(s, axis=0, keepdims=True))
        alpha = jnp.exp2(m_prev - m_new)
        p = jnp.exp2(s - m_new).astype(BF16)
        acc_ref[:, q_lo:] = alpha * acc_ref[:, q_lo:] + _dot(vt_ref[0, :, pl.ds(k0, kw)], p)
        return m_new if q_lo == 0 else jnp.concatenate([m_all[:, :q_lo], m_new], axis=1)

    n_pairs = qi
    s_ref[0] = scores(0, tk_main, 0)

    def pair(t, m):
        k0 = pl.multiple_of(t * (2 * tk_main), 2 * tk_main)
        s_ref[1] = scores(k0 + tk_main, tk_main, 0)
        m = update(s_ref[0], k0, tk_main, m, 0)
        s_ref[0] = scores(k0 + 2 * tk_main, tk_main, 0)
        return update(s_ref[1], k0 + tk_main, tk_main, m, 0)

    m = lax.fori_loop(0, n_pairs, pair, jnp.full((1, tq), -jnp.inf, F32))
    base = pl.multiple_of(qi * tq, tq)
    s_ref[1, :, tk_main:] = scores(base + tk_main, tk_main, tk_main)
    for d in range(tq // tk):
        e, r0, q_lo = (d * tk) // tk_main, (d * tk) % tk_main, d * tk
        s = s_ref[e, r0:r0 + tk, q_lo:]
        row = lax.broadcasted_iota(jnp.int32, s.shape, 0)
        col = lax.broadcasted_iota(jnp.int32, s.shape, 1)
        m = update(jnp.where(row <= col, s, -jnp.inf), base + q_lo, tk, m, q_lo)
    acc = acc_ref[...]
    o_ref[...] = (acc[0:MLA_V, :] / acc[MLA_V:MLA_V + 1, :]).T.astype(o_ref.dtype)


def flash_attention(qt, k, vt, batch, seq, tq, tk, tk_main):
    t = k.shape[0]
    tq = _tile(seq, tq)
    tk = _tile(tq, tk)
    tk_main = _tile(tq, tk_main)
    assert tq == 2 * tk_main and tk_main % tk == 0
    nq = seq // tq
    return pl.pallas_call(
        functools.partial(_flash_kernel, tq=tq, tk=tk, tk_main=tk_main),
        grid=(batch, MLA_HEADS, nq),
        in_specs=[pl.BlockSpec((1, MLA_PAD, tq), lambda b, h, i: (h, 0, b * nq + i)),
                  pl.BlockSpec((seq, MLA_PAD), lambda b, h, i: (b, h)),
                  pl.BlockSpec((1, VT_ROWS, seq), lambda b, h, i: (h, 0, b))],
        out_specs=pl.BlockSpec((tq, MLA_V), lambda b, h, i: (b * nq + i, h)),
        out_shape=jax.ShapeDtypeStruct((t, MLA_HEADS * MLA_V), BF16),
        scratch_shapes=[pltpu.VMEM((VT_ROWS, tq), F32), pltpu.VMEM((2, tk_main, tq), F32)],
        compiler_params=_params("parallel", "parallel", "arbitrary"),
        name="mla_flash",
    )(qt, k, vt)


def _ret_kernel(q_ref, k_ref, v_ref, g_ref, o_ref, r_ref):
    c = RET_CHUNK
    hf = pl.program_id(1).astype(F32)

    @pl.when(pl.program_id(2) == 0)
    def _():
        r_ref[...] = jnp.zeros(r_ref.shape, F32)

    def log_gamma(shape):
        return jnp.log(1.0 - jnp.exp2(-5.0 - jnp.full(shape, hf, F32)))

    row = lax.broadcasted_iota(jnp.int32, (c, c), 0)
    col = lax.broadcasted_iota(jnp.int32, (c, c), 1)
    diff = (row - col).astype(F32)
    d_intra = jnp.where(diff >= 0, jnp.exp(jnp.where(diff >= 0, diff, 0.0) * log_gamma((c, c))), 0.0)
    idx = lax.broadcasted_iota(jnp.int32, (c, RET_DK), 0).astype(F32)
    lg_k = log_gamma((c, RET_DK))
    q_dec = jnp.exp((idx + 1.0) * lg_k).astype(BF16)
    k_dec = jnp.exp((c - 1.0 - idx) * lg_k).astype(BF16)
    c_dec = jnp.exp(c * log_gamma((1, RET_DV)))

    def chunk(ci, carry):
        rows = pl.ds(pl.multiple_of(ci * c, c), c)
        q, k, v = q_ref[rows, :], k_ref[rows, :], v_ref[rows, :]
        a = lax.dot_general(q, k, NT, preferred_element_type=F32) * d_intra
        r = r_ref[...]
        out = _dot(a.astype(BF16), v) + _dot(q * q_dec, r.astype(BF16))
        r_ref[...] = r * c_dec + lax.dot_general(k * k_dec, v, TN, preferred_element_type=F32)
        g = g_ref[rows, :].astype(F32)
        o_ref[rows, :] = (_rms(out) * (g * jax.nn.sigmoid(g))).astype(o_ref.dtype)
        return carry

    lax.fori_loop(0, q_ref.shape[0] // c, chunk, 0, unroll=2)


def retention(z, batch, seq, cb):
    t = z.shape[0]
    cb = _tile(seq, cb)
    assert cb % RET_CHUNK == 0
    nblk = seq // cb
    nqk = RET_HEADS * RET_DK // RET_DK
    nv = (2 * RET_HEADS * RET_DK) // RET_DV
    return pl.pallas_call(
        _ret_kernel,
        grid=(batch, RET_HEADS, nblk),
        in_specs=[pl.BlockSpec((cb, RET_DK), lambda b, h, s: (b * nblk + s, h)),
                  pl.BlockSpec((cb, RET_DK), lambda b, h, s: (b * nblk + s, nqk + h)),
                  pl.BlockSpec((cb, RET_DV), lambda b, h, s: (b * nblk + s, nv + h)),
                  pl.BlockSpec((cb, RET_DV), lambda b, h, s: (b * nblk + s, nv + RET_HEADS + h))],
        out_specs=pl.BlockSpec((cb, RET_DV), lambda b, h, s: (b * nblk + s, h)),
        out_shape=jax.ShapeDtypeStruct((t, RET_HEADS * RET_DV), BF16),
        scratch_shapes=[pltpu.VMEM((RET_DK, RET_DV), F32)],
        compiler_params=_params("parallel", "parallel", "arbitrary"),
        name="retention",
    )(z, z, z, z)


def _rope_pad(a, axis):
    x1, x2 = jnp.split(a, 2, axis=axis)
    z = jnp.zeros_like(x1)
    return jnp.concatenate([x1, z, x2, z], axis=axis)


def _pad_qk_gain(g):
    return jnp.concatenate([g[:MLA_NOPE], _rope_pad(g[MLA_NOPE:], 0)])[None, :]


def kernel(x, p, positions, norm_mix, norm_ffn, norm_ple, e_w_in, e_lb_logits, e_q_a_norm, e_kv_a_norm, e_w_uq, e_w_ukv, e_q_norm, e_k_norm, e_hg_onorm, e_w_out, o_w_in, o_w_out, ffn_w_gate, ffn_w_up, ffn_conv_w, ffn_conv_b, ffn_w_down, ple_w_proj, ple_w_gate):
    batch, seq, d = x.shape
    t = batch * seq
    h = x.reshape(t, d)
    pos_col = positions.reshape(t, 1)

    f_ret = (ROPE_BASE ** (-jnp.arange(RET_DK // 2, dtype=F32) / (RET_DK // 2)))[None, :]
    n_mla, stride = MLA_ROPE // 2, (RET_DK // 2) // (MLA_ROPE // 2)
    src = jnp.arange(n_mla) * stride
    sel = jnp.zeros((RET_DK // 2, 128), F32)
    sel_cos = sel.at[src, jnp.arange(n_mla)].set(1.0).at[src, 64 + jnp.arange(n_mla)].set(1.0)
    sel_sin = sel.at[src, jnp.arange(n_mla)].set(-1.0).at[src, 64 + jnp.arange(n_mla)].set(1.0)
    cos_ret, sin_ret, cos_mla, sin_mla = rope_tables(pos_col, f_ret, sel_cos.astype(BF16),
                                                     sel_sin.astype(BF16))

    w_in = e_w_in[0]
    w_hg = cast_bf16(e_w_in, 0, 4 * HG_W, "cast_w_hgrn")
    off = 4 * HG_W + MLA_Q_RANK + MLA_KV_RANK
    w_mla = jnp.concatenate([w_in[:, 4 * HG_W:off], _rope_pad(w_in[:, off:], 1)], axis=1).astype(BF16)
    wuq = e_w_uq[0].reshape(MLA_Q_RANK, MLA_HEADS, MLA_QK)
    wuq = jnp.concatenate([wuq[..., :MLA_NOPE], _rope_pad(wuq[..., MLA_NOPE:], 2)], axis=-1)
    wuq = wuq.reshape(MLA_Q_RANK, MLA_HEADS * MLA_PAD).astype(BF16)
    wukv = e_w_ukv[0].astype(BF16)
    g_mix0 = norm_mix[0][None, :]

    z_hg = norm_matmul(h, g_mix0, w_hg, F32, 1024, 2048, "in_proj_hgrn")
    z_mla = norm_matmul(h, g_mix0, w_mla, F32, 1024, w_mla.shape[1], "in_proj_mla")
    o_a = hgrn(z_hg, e_lb_logits, e_hg_onorm[0][None, :], batch, seq, 512)
    qt, k, vt = mla_prep(z_mla, e_q_a_norm[0][None, :], e_kv_a_norm[0][None, :], wuq, wukv,
                       _pad_qk_gain(e_q_norm[0]), _pad_qk_gain(e_k_norm[0]), cos_mla, sin_mla, 512)
    o_b = flash_attention(qt, k, vt, batch, seq, 2048, 512, 1024)
    h = res_matmul(h, [o_a, o_b], cast_bf16(e_w_out, 0, name="cast_w_even_out"), 1024, 2048,
                   "out_proj_even")

    def channel_mix(h, i):
        h = conv_ffn(h, norm_ffn[i][None, :], cast_bf16(ffn_w_gate, i, name="cast_w_gate"),
                     cast_bf16(ffn_w_up, i, name="cast_w_up"), ffn_conv_w[i], ffn_conv_b[i][None, :],
                     cast_bf16(ffn_w_down, i, name="cast_w_down"), seq, 1024, 512)
        return ple(h, norm_ple[i][None, :], cast_bf16(ple_w_gate, i, name="cast_w_ple"),
                   p.reshape(p.shape[0], t, PLE_DIM), i, ple_w_proj[i].astype(BF16), 1024, 2048)

    h = channel_mix(h, 0)

    z_ret = ret_in_proj(h, norm_mix[1][None, :], cast_bf16(o_w_in, 0, name="cast_w_ret_in"),
                        cos_ret, sin_ret, 1024, 2048)
    o_r = retention(z_ret, batch, seq, 2048)
    h = res_matmul(h, [o_r], cast_bf16(o_w_out, 0, name="cast_w_ret_out"), 1024, 1024, "out_proj_odd")
    h = channel_mix(h, 1)
    return h.reshape(batch, seq, d)
```

```python
import functools

import jax
import jax.numpy as jnp
from jax import lax
from jax.experimental import pallas as pl
from jax.experimental.pallas import tpu as pltpu

F32 = jnp.float32
BF16 = jnp.bfloat16

LANES = 128
SUBLANES = 8
BF16_SUBLANES = 16
MXU_DIM = 256
VMEM_BYTES = 64 * 2**20

PLE_DIM = 256
HG_HEADS = 8
HG_DK = 128
HG_DV = 128
HG_W = HG_HEADS * HG_DK
HG_CHUNK = 64
MLA_HEADS = 8
MLA_Q_RANK = 512
MLA_KV_RANK = 512
MLA_NOPE = 128
MLA_ROPE = 64
MLA_V = 128
MLA_QK = MLA_NOPE + MLA_ROPE
MLA_PAD = MLA_NOPE + LANES
VT_ROWS = MLA_V + BF16_SUBLANES
LOG2E = 1.4426950408889634
RET_HEADS = 8
RET_DK = 256
RET_DV = 512
RET_CHUNK = 256
CONV_W = 3
FFN_STRIP = MXU_DIM
ROPE_BASE = 10000.0
EPS = 1e-6

VMEM_LIMIT = VMEM_BYTES - 4 * 2**20
CAST_BLOCK_BYTES = 8 * 2**20

ROWS = 1024
NORM_ROWS = 256
COLS = 2048
FFN_COLS = 512
PREP_ROWS = 512
HGRN_ROWS = 512
RET_ROWS = 2048
FLASH_Q = 2048
FLASH_KV = 1024
FLASH_KV_DIAG = 512

NT = (((1,), (1,)), ((), ()))
TN = (((0,), (0,)), ((), ()))


def _params(*sem):
    return pltpu.CompilerParams(dimension_semantics=sem, vmem_limit_bytes=VMEM_LIMIT)


def _tile(n, pref):
    t = min(n, pref)
    assert n % t == 0, (n, pref)
    return t


def _dot(a, b):
    return jnp.dot(a, b, preferred_element_type=F32)


def _rms(x, g=None):
    y = x * lax.rsqrt(jnp.sum(x * x, axis=-1, keepdims=True) * (1.0 / x.shape[-1]) + EPS)
    return y if g is None else y * g


def _norm_into(x_ref, g_ref, dst_ref, rows=NORM_ROWS):
    g = g_ref[...]

    def body(c, carry):
        r0 = pl.multiple_of(c * rows, rows)
        dst_ref[pl.ds(r0, rows), :] = _rms(x_ref[pl.ds(r0, rows), :], g).astype(dst_ref.dtype)
        return carry

    lax.fori_loop(0, x_ref.shape[0] // rows, body, 0)


def _cast_kernel(w_ref, o_ref):
    o_ref[...] = w_ref[...].astype(o_ref.dtype)


def cast_bf16(w, layer, ncols=None, name="cast_bf16"):
    _, r, c = w.shape
    ncols = c if ncols is None else ncols
    tr = r
    while tr % 16 == 0 and tr * ncols * 4 > CAST_BLOCK_BYTES:
        tr //= 2
    return pl.pallas_call(
        _cast_kernel,
        grid=(r // tr,),
        in_specs=[pl.BlockSpec((None, tr, ncols), lambda i: (layer, i, 0))],
        out_specs=pl.BlockSpec((tr, ncols), lambda i: (i, 0)),
        out_shape=jax.ShapeDtypeStruct((r, ncols), BF16),
        compiler_params=_params("parallel"),
        name=name,
    )(w)


def _select_lanes(x, sel):
    hi = x.astype(BF16)
    r1 = x - hi.astype(F32)
    mid = r1.astype(BF16)
    lo = (r1 - mid.astype(F32)).astype(BF16)
    return _dot(hi, sel) + _dot(mid, sel) + _dot(lo, sel)


def _rope_tab_kernel(pos_ref, f_ref, selc_ref, sels_ref, c_ref, s_ref, cm_ref, sm_ref):
    ang = pos_ref[...].astype(F32) * f_ref[...]
    c, s = jnp.cos(ang), jnp.sin(ang)
    c_ref[...] = c
    s_ref[...] = s
    cm_ref[...] = _select_lanes(c, selc_ref[...])
    sm_ref[...] = _select_lanes(s, sels_ref[...])


def rope_tables(pos_col, freq, sel_cos, sel_sin):
    t = pos_col.shape[0]
    tm = _tile(t, ROWS)
    const = lambda a: pl.BlockSpec(a.shape, lambda i: (0, 0))
    return pl.pallas_call(
        _rope_tab_kernel,
        grid=(t // tm,),
        in_specs=[pl.BlockSpec((tm, 1), lambda i: (i, 0)), const(freq), const(sel_cos), const(sel_sin)],
        out_specs=[pl.BlockSpec((tm, LANES), lambda i: (i, 0))] * 4,
        out_shape=[jax.ShapeDtypeStruct((t, LANES), F32)] * 4,
        compiler_params=_params("parallel"),
        name="rope_tables",
    )(pos_col, freq, sel_cos, sel_sin)


def _norm_matmul_kernel(x_ref, g_ref, w_ref, o_ref, xn_ref):
    j = pl.program_id(1)
    tn = o_ref.shape[1]

    @pl.when(j == 0)
    def _():
        _norm_into(x_ref, g_ref, xn_ref)

    w = w_ref[:, pl.ds(pl.multiple_of(j * tn, tn), tn)]
    o_ref[...] = _dot(xn_ref[...], w).astype(o_ref.dtype)


def norm_matmul(x, g, w, out_dtype, tm, tn, name):
    t, k = x.shape
    n = w.shape[1]
    tm, tn = _tile(t, tm), _tile(n, tn)
    return pl.pallas_call(
        _norm_matmul_kernel,
        grid=(t // tm, n // tn),
        in_specs=[pl.BlockSpec((tm, k), lambda i, j: (i, 0)),
                  pl.BlockSpec((1, k), lambda i, j: (0, 0)),
                  pl.BlockSpec((k, n), lambda i, j: (0, 0), pipeline_mode=pl.Buffered(1))],
        out_specs=pl.BlockSpec((tm, tn), lambda i, j: (i, j)),
        out_shape=jax.ShapeDtypeStruct((t, n), out_dtype),
        scratch_shapes=[pltpu.VMEM((tm, k), BF16)],
        compiler_params=_params("parallel", "arbitrary"),
        name=name,
    )(x, g, w)


def _ret_in_kernel(x_ref, g_ref, w_ref, cos_ref, sin_ref, o_ref, xn_ref, *, n_q_tiles, n_rope_tiles):
    j = pl.program_id(1)

    @pl.when(j == 0)
    def _():
        _norm_into(x_ref, g_ref, xn_ref)

    acc = _dot(xn_ref[...], w_ref[...])
    is_rope = j < n_rope_tiles
    scale = jnp.where(j >= n_q_tiles, RET_DK ** -0.5, 1.0).astype(F32)
    cos = cos_ref[...] * scale
    sin = sin_ref[...] * scale
    half = RET_DK // 2
    for hh in range(acc.shape[1] // RET_DK):
        x1 = acc[:, hh * RET_DK:hh * RET_DK + half]
        x2 = acc[:, hh * RET_DK + half:(hh + 1) * RET_DK]
        o_ref[:, hh * RET_DK:hh * RET_DK + half] = jnp.where(
            is_rope, x1 * cos - x2 * sin, x1).astype(o_ref.dtype)
        o_ref[:, hh * RET_DK + half:(hh + 1) * RET_DK] = jnp.where(
            is_rope, x2 * cos + x1 * sin, x2).astype(o_ref.dtype)


def ret_in_proj(x, g, w, cos, sin, tm, tn):
    t, k = x.shape
    n = w.shape[1]
    tm, tn = _tile(t, tm), _tile(n, tn)
    qw = RET_HEADS * RET_DK
    kern = functools.partial(_ret_in_kernel, n_q_tiles=qw // tn, n_rope_tiles=2 * qw // tn)
    return pl.pallas_call(
        kern,
        grid=(t // tm, n // tn),
        in_specs=[pl.BlockSpec((tm, k), lambda i, j: (i, 0)),
                  pl.BlockSpec((1, k), lambda i, j: (0, 0)),
                  pl.BlockSpec((k, tn), lambda i, j: (0, j)),
                  pl.BlockSpec((tm, RET_DK // 2), lambda i, j: (i, 0)),
                  pl.BlockSpec((tm, RET_DK // 2), lambda i, j: (i, 0))],
        out_specs=pl.BlockSpec((tm, tn), lambda i, j: (i, j)),
        out_shape=jax.ShapeDtypeStruct((t, n), BF16),
        scratch_shapes=[pltpu.VMEM((tm, k), BF16)],
        compiler_params=_params("parallel", "arbitrary"),
        name="ret_in_proj",
    )(x, g, w, cos, sin)


def _res_matmul_kernel(*refs):
    h_ref, w_ref, o_ref = refs[0], refs[-2], refs[-1]
    lhs = [a_ref[...] for a_ref in refs[1:-2]]
    lhs = lhs[0] if len(lhs) == 1 else jnp.concatenate(lhs, axis=1)
    o_ref[...] = h_ref[...] + _dot(lhs, w_ref[...])


def res_matmul(h, lhs, w, tm, tn, name):
    t, n = h.shape
    tm, tn = _tile(t, tm), _tile(n, tn)
    assert sum(a.shape[1] for a in lhs) == w.shape[0]
    in_specs = [pl.BlockSpec((tm, tn), lambda i, j: (i, j))]
    in_specs += [pl.BlockSpec((tm, a.shape[1]), lambda i, j: (i, 0)) for a in lhs]
    in_specs += [pl.BlockSpec((w.shape[0], tn), lambda i, j: (0, j),
                              pipeline_mode=pl.Buffered(1) if tn == n else None)]
    return pl.pallas_call(
        _res_matmul_kernel,
        grid=(t // tm, n // tn),
        in_specs=in_specs,
        out_specs=pl.BlockSpec((tm, tn), lambda i, j: (i, j)),
        out_shape=jax.ShapeDtypeStruct((t, n), F32),
        compiler_params=_params("parallel", "arbitrary"),
        name=name,
    )(h, *lhs, w)


def _ffn_kernel(h_ref, g_ref, wg_ref, wu_ref, cw_ref, cb_ref, wd_ref, o_ref,
                u_ref, gs_ref, tail_ref, *, blocks_per_seq):
    i, j = pl.program_id(0), pl.program_id(1)
    tm = h_ref.shape[0]

    @pl.when(j == 0)
    def _():
        _norm_into(h_ref, g_ref, u_ref)
        o_ref[...] = h_ref[...]

    @pl.when(jnp.logical_and(i == 0, j == 0))
    def _():
        tail_ref[...] = jnp.zeros(tail_ref.shape, F32)

    u = u_ref[...]
    seq_start = (i % blocks_per_seq) == 0
    prev = jnp.where(seq_start, 0.0, tail_ref[j])
    cw, cb = cw_ref[...], cb_ref[...]
    contrib = None
    for c0 in range(0, wg_ref.shape[1], FFN_STRIP):
        cols = slice(c0, c0 + FFN_STRIP)
        a = _dot(u, wg_ref[:, cols])
        up = _dot(u, wu_ref[:, cols])
        gs_ref[0:SUBLANES, cols] = prev[:, cols]
        gs_ref[SUBLANES:SUBLANES + tm, cols] = a
        tail_ref[j, :, cols] = a[tm - SUBLANES:tm, :]
        c = cb[:, cols] + cw[CONV_W - 1:CONV_W, cols] * a
        for tap in range(CONV_W - 1):
            r0 = SUBLANES - (CONV_W - 1 - tap)
            c = c + cw[tap:tap + 1, cols] * gs_ref[r0:r0 + tm, cols]
        act = (c * jax.nn.sigmoid(c) * up).astype(BF16)
        part = _dot(act, wd_ref[cols, :])
        contrib = part if contrib is None else contrib + part
    o_ref[...] += contrib


def conv_ffn(h, g, wg, wu, cw, cb, wd, seq, tm, tf):
    t, d = h.shape
    f = wg.shape[1]
    tm, tf = _tile(seq, tm), _tile(f, tf)
    kern = functools.partial(_ffn_kernel, blocks_per_seq=seq // tm)
    return pl.pallas_call(
        kern,
        grid=(t // tm, f // tf),
        in_specs=[pl.BlockSpec((tm, d), lambda i, j: (i, 0)),
                  pl.BlockSpec((1, d), lambda i, j: (0, 0)),
                  pl.BlockSpec((d, tf), lambda i, j: (0, j)),
                  pl.BlockSpec((d, tf), lambda i, j: (0, j)),
                  pl.BlockSpec((CONV_W, tf), lambda i, j: (0, j)),
                  pl.BlockSpec((1, tf), lambda i, j: (0, j)),
                  pl.BlockSpec((tf, d), lambda i, j: (j, 0))],
        out_specs=pl.BlockSpec((tm, d), lambda i, j: (i, 0)),
        out_shape=jax.ShapeDtypeStruct((t, d), F32),
        scratch_shapes=[pltpu.VMEM((tm, d), BF16),
                        pltpu.VMEM((tm + SUBLANES, tf), F32),
                        pltpu.VMEM((f // tf, SUBLANES, tf), F32)],
        compiler_params=_params("arbitrary", "arbitrary"),
        name="conv_ffn",
    )(h, g, wg, wu, cw, cb, wd)


def _ple_kernel(h_ref, g_ref, wgate_ref, p_ref, wproj_ref, o_ref, xn_ref, pb_ref):
    j = pl.program_id(1)
    tn = o_ref.shape[1]

    @pl.when(j == 0)
    def _():
        _norm_into(h_ref, g_ref, xn_ref)
        pb_ref[...] = p_ref[...].astype(BF16)

    gate = jax.nn.sigmoid(_dot(xn_ref[...], wgate_ref[...]))
    proj = _dot(pb_ref[...], wproj_ref[...])
    c0 = pl.multiple_of(j * tn, tn)
    o_ref[...] = h_ref[:, pl.ds(c0, tn)] + proj * gate


def ple(h, g, wgate, p, layer, wproj, tm, tn):
    t, d = h.shape
    tm, tn = _tile(t, tm), _tile(d, tn)
    return pl.pallas_call(
        _ple_kernel,
        grid=(t // tm, d // tn),
        in_specs=[pl.BlockSpec((tm, d), lambda i, j: (i, 0)),
                  pl.BlockSpec((1, d), lambda i, j: (0, 0)),
                  pl.BlockSpec((d, tn), lambda i, j: (0, j)),
                  pl.BlockSpec((None, tm, PLE_DIM), lambda i, j: (layer, i, 0)),
                  pl.BlockSpec((PLE_DIM, tn), lambda i, j: (0, j))],
        out_specs=pl.BlockSpec((tm, tn), lambda i, j: (i, j)),
        out_shape=jax.ShapeDtypeStruct((t, d), F32),
        scratch_shapes=[pltpu.VMEM((tm, d), BF16), pltpu.VMEM((tm, PLE_DIM), BF16)],
        compiler_params=_params("parallel", "arbitrary"),
        name="ple",
    )(h, g, wgate, p, wproj)


def _hgrn_kernel(q_ref, f_ref, i_ref, g_ref, lbl_ref, on_ref, o_ref, st_ref):
    c = HG_CHUNK

    @pl.when(pl.program_id(1) == 0)
    def _():
        st_ref[...] = jnp.zeros(st_ref.shape, F32)

    lg = lbl_ref[...]
    e = jnp.exp(lg - jnp.max(lg, axis=0, keepdims=True))
    lb = e[0:1, :] / jnp.sum(e, axis=0, keepdims=True)
    onorm = on_ref[...]
    pw = 2 * HG_DK
    row = lax.broadcasted_iota(jnp.int32, (c, c), 0)
    col = lax.broadcasted_iota(jnp.int32, (c, c), 1)
    tril = jnp.where(row >= col, 1.0, 0.0).astype(BF16)
    row2 = lax.broadcasted_iota(jnp.int32, (c, 2 * c), 0)
    col2 = lax.broadcasted_iota(jnp.int32, (c, 2 * c), 1)
    causal2 = row2 >= jnp.where(col2 >= c, col2 - c, col2)
    left = lax.broadcasted_iota(jnp.int32, (c, pw), 1) < HG_DK
    same_head = ((lax.broadcasted_iota(jnp.int32, (pw, pw), 0) < HG_DV)
                 == (lax.broadcasted_iota(jnp.int32, (pw, pw), 1) < HG_DK))

    def block_diag(x):
        zero = jnp.zeros_like(x)
        return jnp.concatenate([jnp.where(left, x, zero), jnp.where(left, zero, x)], axis=0)

    def chunk(ci, carry):
        rows = pl.ds(pl.multiple_of(ci * c, c), c)
        q = q_ref[rows, :]
        v = i_ref[rows, :].astype(BF16)
        f = lb + (1.0 - lb) * jax.nn.sigmoid(f_ref[rows, :])
        k = 1.0 - f
        lf = jnp.log(f)
        hi = lf.astype(BF16)
        r1 = lf - hi.astype(F32)
        mid = r1.astype(BF16)
        lo = (r1 - mid.astype(F32)).astype(BF16)
        bc = _dot(tril, hi) + _dot(tril, mid) + _dot(tril, lo)
        ref = bc[c // 2 - 1:c // 2, :]
        b_last = bc[c - 1:c, :]
        q_rel = (q * jnp.exp(bc - ref)).astype(BF16)
        k_rel = (k * jnp.exp(ref - bc)).astype(BF16)
        q_dec = (q * jnp.exp(bc)).astype(BF16)
        k_dec = (k * jnp.exp(b_last - bc)).astype(BF16)
        decay = jnp.exp(b_last)
        for p in range(HG_HEADS // 2):
            cols = slice(p * pw, (p + 1) * pw)
            a = lax.dot_general(q_rel[:, cols], block_diag(k_rel[:, cols]), NT, preferred_element_type=F32)
            a = jnp.where(causal2, a, 0.0).astype(BF16)
            st = st_ref[p]
            out = _dot(a, block_diag(v[:, cols])) + lax.dot_general(q_dec[:, cols], st.astype(BF16), NT,
                                                                    preferred_element_type=F32)
            upd = lax.dot_general(v[:, cols], k_dec[:, cols], TN, preferred_element_type=F32)
            st_ref[p] = st * decay[:, cols] + jnp.where(same_head, upd, 0.0)
            for hh in range(2):
                hc = slice(p * pw + hh * HG_DV, p * pw + (hh + 1) * HG_DV)
                g = g_ref[rows, hc]
                o_ref[rows, hc] = (_rms(out[:, hh * HG_DV:(hh + 1) * HG_DV], onorm)
                                   * (g * jax.nn.sigmoid(g))).astype(o_ref.dtype)
        return carry

    lax.fori_loop(0, q_ref.shape[0] // c, chunk, 0, unroll=4)


def hgrn(z, lb_logits, onorm, batch, seq, cb):
    t = z.shape[0]
    cb = _tile(seq, cb)
    nblk = seq // cb
    spec = lambda part: pl.BlockSpec((cb, HG_W), lambda b, s, part=part: (b * nblk + s, part))
    return pl.pallas_call(
        _hgrn_kernel,
        grid=(batch, nblk),
        in_specs=[spec(0), spec(1), spec(2), spec(3),
                  pl.BlockSpec(lb_logits.shape, lambda b, s: (0, 0)),
                  pl.BlockSpec((1, HG_DV), lambda b, s: (0, 0))],
        out_specs=pl.BlockSpec((cb, HG_W), lambda b, s: (b * nblk + s, 0)),
        out_shape=jax.ShapeDtypeStruct((t, HG_W), BF16),
        scratch_shapes=[pltpu.VMEM((HG_HEADS // 2, 2 * HG_DV, 2 * HG_DK), F32)],
        compiler_params=_params("parallel", "arbitrary"),
        name="hgrn2",
    )(z, z, z, z, lb_logits, onorm)


def _mla_prep_kernel(z_ref, qa_ref, kva_ref, wuq_ref, wukv_ref, qn_ref, kn_ref, cos_ref, sin_ref,
                     qt_out, k_out, vt_out):
    cq = _rms(z_ref[:, 0:MLA_Q_RANK], qa_ref[...]).astype(BF16)
    ckv = _rms(z_ref[:, MLA_Q_RANK:MLA_Q_RANK + MLA_KV_RANK], kva_ref[...]).astype(BF16)
    kpe = z_ref[:, MLA_Q_RANK + MLA_KV_RANK:]
    qf = _dot(cq, wuq_ref[...])
    kvf = _dot(ckv, wukv_ref[...])
    cos, sin = cos_ref[...], sin_ref[...]
    qn, kn = qn_ref[...], kn_ref[...]
    scale = MLA_QK ** -0.5 * LOG2E
    tm = z_ref.shape[0]

    def rope(x):
        return x * cos + pltpu.roll(x, LANES // 2, 1) * sin

    kpe_ss = jnp.sum(kpe * kpe, axis=-1, keepdims=True)
    k_rope = rope(kpe * kn[:, MLA_NOPE:])
    ones_rows = jnp.where(lax.broadcasted_iota(jnp.int32, (VT_ROWS - MLA_V, tm), 0) == 0, 1.0, 0.0)
    for h in range(MLA_HEADS):
        c0 = h * MLA_PAD
        qh = qf[:, c0:c0 + MLA_PAD]
        rq = lax.rsqrt(jnp.sum(qh * qh, axis=-1, keepdims=True) * (1.0 / MLA_QK) + EPS) * scale
        qh = qh * rq * qn
        qt_out[h, 0:MLA_NOPE, :] = qh[:, :MLA_NOPE].T.astype(BF16)
        qt_out[h, MLA_NOPE:MLA_PAD, :] = rope(qh[:, MLA_NOPE:]).T.astype(BF16)
        kh = kvf[:, c0:c0 + MLA_NOPE]
        rk = lax.rsqrt((jnp.sum(kh * kh, axis=-1, keepdims=True) + kpe_ss) * (1.0 / MLA_QK) + EPS)
        k_out[:, c0:c0 + MLA_NOPE] = (kh * rk * kn[:, :MLA_NOPE]).astype(BF16)
        k_out[:, c0 + MLA_NOPE:c0 + MLA_PAD] = (k_rope * rk).astype(BF16)
        vt_out[h, 0:MLA_V, :] = kvf[:, c0 + MLA_NOPE:c0 + MLA_PAD].T.astype(BF16)
        vt_out[h, MLA_V:VT_ROWS, :] = ones_rows.astype(BF16)


def mla_prep(z, qa, kva, wuq, wukv, qn, kn, cos, sin, tm):
    t, zw = z.shape
    tm = _tile(t, tm)
    full = lambda a: pl.BlockSpec(a.shape, lambda i: (0, 0))
    rows = lambda w: pl.BlockSpec((tm, w), lambda i: (i, 0))
    cols = lambda r: pl.BlockSpec((MLA_HEADS, r, tm), lambda i: (0, 0, i))
    return pl.pallas_call(
        _mla_prep_kernel,
        grid=(t // tm,),
        in_specs=[rows(zw), full(qa), full(kva), full(wuq), full(wukv), full(qn), full(kn),
                  rows(LANES), rows(LANES)],
        out_specs=[cols(MLA_PAD), rows(MLA_HEADS * MLA_PAD), cols(VT_ROWS)],
        out_shape=[jax.ShapeDtypeStruct((MLA_HEADS, MLA_PAD, t), BF16),
                   jax.ShapeDtypeStruct((t, MLA_HEADS * MLA_PAD), BF16),
                   jax.ShapeDtypeStruct((MLA_HEADS, VT_ROWS, t), BF16)],
        compiler_params=_params("parallel"),
        name="mla_prep",
    )(z, qa, kva, wuq, wukv, qn, kn, cos, sin)


def _flash_kernel(qt_ref, k_ref, vt_ref, o_ref, acc_ref, s_ref, *, tq, tk, tk_main):
    qi = pl.program_id(2)
    qt = qt_ref[0]
    acc_ref[...] = jnp.zeros(acc_ref.shape, F32)

    def scores(k0, kw, q_lo):
        return _dot(k_ref[pl.ds(k0, kw), :], qt[:, q_lo:])

    def update(s, k0, kw, m_all, q_lo):
        m_prev = m_all[:, q_lo:]
        m_new = jnp.maximum(m_prev, jnp.max(s, axis=0, keepdims=True))
        alpha = jnp.exp2(m_prev - m_new)
        p = jnp.exp2(s - m_new).astype(BF16)
        acc_ref[:, q_lo:] = alpha * acc_ref[:, q_lo:] + _dot(vt_ref[0, :, pl.ds(k0, kw)], p)
        return m_new if q_lo == 0 else jnp.concatenate([m_all[:, :q_lo], m_new], axis=1)

    n_pairs = qi
    s_ref[0] = scores(0, tk_main, 0)

    def pair(t, m):
        k0 = pl.multiple_of(t * (2 * tk_main), 2 * tk_main)
        s_ref[1] = scores(k0 + tk_main, tk_main, 0)
        m = update(s_ref[0], k0, tk_main, m, 0)
        s_ref[0] = scores(k0 + 2 * tk_main, tk_main, 0)
        return update(s_ref[1], k0 + tk_main, tk_main, m, 0)

    m = lax.fori_loop(0, n_pairs, pair, jnp.full((1, tq), -jnp.inf, F32))
    base = pl.multiple_of(qi * tq, tq)
    s_ref[1, :, tk_main:] = scores(base + tk_main, tk_main, tk_main)
    for d in range(tq // tk):
        e, r0, q_lo = (d * tk) // tk_main, (d * tk) % tk_main, d * tk
        s = s_ref[e, r0:r0 + tk, q_lo:]
        row = lax.broadcasted_iota(jnp.int32, s.shape, 0)
        col = lax.broadcasted_iota(jnp.int32, s.shape, 1)
        m = update(jnp.where(row <= col, s, -jnp.inf), base + q_lo, tk, m, q_lo)
    acc = acc_ref[...]
    o_ref[...] = (acc[0:MLA_V, :] / acc[MLA_V:MLA_V + 1, :]).T.astype(o_ref.dtype)


def flash_attention(qt, k, vt, batch, seq, tq, tk, tk_main):
    t = k.shape[0]
    tq = _tile(seq, tq)
    tk = _tile(tq, tk)
    tk_main = _tile(tq, tk_main)
    assert tq == 2 * tk_main and tk_main % tk == 0
    nq = seq // tq
    return pl.pallas_call(
        functools.partial(_flash_kernel, tq=tq, tk=tk, tk_main=tk_main),
        grid=(batch, MLA_HEADS, nq),
        in_specs=[pl.BlockSpec((1, MLA_PAD, tq), lambda b, h, i: (h, 0, b * nq + i)),
                  pl.BlockSpec((seq, MLA_PAD), lambda b, h, i: (b, h)),
                  pl.BlockSpec((1, VT_ROWS, seq), lambda b, h, i: (h, 0, b))],
        out_specs=pl.BlockSpec((tq, MLA_V), lambda b, h, i: (b * nq + i, h)),
        out_shape=jax.ShapeDtypeStruct((t, MLA_HEADS * MLA_V), BF16),
        scratch_shapes=[pltpu.VMEM((VT_ROWS, tq), F32), pltpu.VMEM((2, tk_main, tq), F32)],
        compiler_params=_params("parallel", "parallel", "arbitrary"),
        name="mla_flash",
    )(qt, k, vt)


def _ret_kernel(q_ref, k_ref, v_ref, g_ref, o_ref, r_ref):
    c = RET_CHUNK
    hf = pl.program_id(1).astype(F32)

    @pl.when(pl.program_id(2) == 0)
    def _():
        r_ref[...] = jnp.zeros(r_ref.shape, F32)

    def log_gamma(shape):
        return jnp.log(1.0 - jnp.exp2(-5.0 - jnp.full(shape, hf, F32)))

    row = lax.broadcasted_iota(jnp.int32, (c, c), 0)
    col = lax.broadcasted_iota(jnp.int32, (c, c), 1)
    diff = (row - col).astype(F32)
    d_intra = jnp.where(diff >= 0, jnp.exp(jnp.where(diff >= 0, diff, 0.0) * log_gamma((c, c))), 0.0)
    idx = lax.broadcasted_iota(jnp.int32, (c, RET_DK), 0).astype(F32)
    lg_k = log_gamma((c, RET_DK))
    q_dec = jnp.exp((idx + 1.0) * lg_k).astype(BF16)
    k_dec = jnp.exp((c - 1.0 - idx) * lg_k).astype(BF16)
    c_dec = jnp.exp(c * log_gamma((1, RET_DV)))

    def chunk(ci, carry):
        rows = pl.ds(pl.multiple_of(ci * c, c), c)
        q, k, v = q_ref[rows, :], k_ref[rows, :], v_ref[rows, :]
        a = lax.dot_general(q, k, NT, preferred_element_type=F32) * d_intra
        r = r_ref[...]
        out = _dot(a.astype(BF16), v) + _dot(q * q_dec, r.astype(BF16))
        r_ref[...] = r * c_dec + lax.dot_general(k * k_dec, v, TN, preferred_element_type=F32)
        g = g_ref[rows, :].astype(F32)
        o_ref[rows, :] = (_rms(out) * (g * jax.nn.sigmoid(g))).astype(o_ref.dtype)
        return carry

    lax.fori_loop(0, q_ref.shape[0] // c, chunk, 0, unroll=2)


def retention(z, batch, seq, cb):
    t = z.shape[0]
    cb = _tile(seq, cb)
    assert cb % RET_CHUNK == 0
    nblk = seq // cb
    nqk = RET_HEADS * RET_DK // RET_DK
    nv = (2 * RET_HEADS * RET_DK) // RET_DV
    return pl.pallas_call(
        _ret_kernel,
        grid=(batch, RET_HEADS, nblk),
        in_specs=[pl.BlockSpec((cb, RET_DK), lambda b, h, s: (b * nblk + s, h)),
                  pl.BlockSpec((cb, RET_DK), lambda b, h, s: (b * nblk + s, nqk + h)),
                  pl.BlockSpec((cb, RET_DV), lambda b, h, s: (b * nblk + s, nv + h)),
                  pl.BlockSpec((cb, RET_DV), lambda b, h, s: (b * nblk + s, nv + RET_HEADS + h))],
        out_specs=pl.BlockSpec((cb, RET_DV), lambda b, h, s: (b * nblk + s, h)),
        out_shape=jax.ShapeDtypeStruct((t, RET_HEADS * RET_DV), BF16),
        scratch_shapes=[pltpu.VMEM((RET_DK, RET_DV), F32)],
        compiler_params=_params("parallel", "parallel", "arbitrary"),
        name="retention",
    )(z, z, z, z)


def _rope_pad(a, axis):
    x1, x2 = jnp.split(a, 2, axis=axis)
    z = jnp.zeros_like(x1)
    return jnp.concatenate([x1, z, x2, z], axis=axis)


def _pad_qk_gain(g):
    return jnp.concatenate([g[:MLA_NOPE], _rope_pad(g[MLA_NOPE:], 0)])[None, :]


def kernel(x, p, positions, norm_mix, norm_ffn, norm_ple, e_w_in, e_lb_logits, e_q_a_norm, e_kv_a_norm, e_w_uq, e_w_ukv, e_q_norm, e_k_norm, e_hg_onorm, e_w_out, o_w_in, o_w_out, ffn_w_gate, ffn_w_up, ffn_conv_w, ffn_conv_b, ffn_w_down, ple_w_proj, ple_w_gate):
    batch, seq, d = x.shape
    assert norm_mix.shape[0] == 2 and e_w_in.shape[0] == 1 and o_w_in.shape[0] == 1
    t = batch * seq
    h = x.reshape(t, d)
    pos_col = positions.reshape(t, 1)

    f_ret = (ROPE_BASE ** (-jnp.arange(RET_DK // 2, dtype=F32) / (RET_DK // 2)))[None, :]
    n_mla, stride = MLA_ROPE // 2, (RET_DK // 2) // (MLA_ROPE // 2)
    src, x1, x2 = jnp.arange(n_mla) * stride, jnp.arange(n_mla), LANES // 2 + jnp.arange(n_mla)
    sel = jnp.zeros((RET_DK // 2, LANES), F32)
    sel_cos = sel.at[src, x1].set(1.0).at[src, x2].set(1.0)
    sel_sin = sel.at[src, x1].set(-1.0).at[src, x2].set(1.0)
    cos_ret, sin_ret, cos_mla, sin_mla = rope_tables(pos_col, f_ret, sel_cos.astype(BF16),
                                                     sel_sin.astype(BF16))

    w_in = e_w_in[0]
    w_hg = cast_bf16(e_w_in, 0, 4 * HG_W, "cast_w_hgrn")
    off = 4 * HG_W + MLA_Q_RANK + MLA_KV_RANK
    w_mla = jnp.concatenate([w_in[:, 4 * HG_W:off], _rope_pad(w_in[:, off:], 1)], axis=1).astype(BF16)
    wuq = e_w_uq[0].reshape(MLA_Q_RANK, MLA_HEADS, MLA_QK)
    wuq = jnp.concatenate([wuq[..., :MLA_NOPE], _rope_pad(wuq[..., MLA_NOPE:], 2)], axis=-1)
    wuq = wuq.reshape(MLA_Q_RANK, MLA_HEADS * MLA_PAD).astype(BF16)
    wukv = e_w_ukv[0].astype(BF16)
    g_mix0 = norm_mix[0][None, :]

    z_hg = norm_matmul(h, g_mix0, w_hg, F32, ROWS, COLS, "in_proj_hgrn")
    z_mla = norm_matmul(h, g_mix0, w_mla, F32, ROWS, w_mla.shape[1], "in_proj_mla")
    o_a = hgrn(z_hg, e_lb_logits, e_hg_onorm[0][None, :], batch, seq, HGRN_ROWS)
    qt, k, vt = mla_prep(z_mla, e_q_a_norm[0][None, :], e_kv_a_norm[0][None, :], wuq, wukv,
                         _pad_qk_gain(e_q_norm[0]), _pad_qk_gain(e_k_norm[0]), cos_mla, sin_mla, PREP_ROWS)
    o_b = flash_attention(qt, k, vt, batch, seq, FLASH_Q, FLASH_KV_DIAG, FLASH_KV)
    h = res_matmul(h, [o_a, o_b], cast_bf16(e_w_out, 0, name="cast_w_even_out"), ROWS, COLS,
                   "out_proj_even")

    def channel_mix(h, i):
        h = conv_ffn(h, norm_ffn[i][None, :], cast_bf16(ffn_w_gate, i, name="cast_w_gate"),
                     cast_bf16(ffn_w_up, i, name="cast_w_up"), ffn_conv_w[i], ffn_conv_b[i][None, :],
                     cast_bf16(ffn_w_down, i, name="cast_w_down"), seq, ROWS, FFN_COLS)
        return ple(h, norm_ple[i][None, :], cast_bf16(ple_w_gate, i, name="cast_w_ple"),
                   p.reshape(p.shape[0], t, PLE_DIM), i, ple_w_proj[i].astype(BF16), ROWS, COLS)

    h = channel_mix(h, 0)

    z_ret = ret_in_proj(h, norm_mix[1][None, :], cast_bf16(o_w_in, 0, name="cast_w_ret_in"),
                        cos_ret, sin_ret, ROWS, COLS)
    o_r = retention(z_ret, batch, seq, RET_ROWS)
    h = res_matmul(h, [o_r], cast_bf16(o_w_out, 0, name="cast_w_ret_out"), ROWS, COLS // 2, "out_proj_odd")
    h = channel_mix(h, 1)
    return h.reshape(batch, seq, d)
```

```python
import functools

import jax
import jax.numpy as jnp
from jax import lax
from jax.experimental import pallas as pl
from jax.experimental.pallas import tpu as pltpu

F32 = jnp.float32
BF16 = jnp.bfloat16

LANES = 128
SUBLANES = 8
BF16_SUBLANES = 16
MXU_DIM = 256
VMEM_BYTES = 64 * 2**20

PLE_DIM = 256
HG_HEADS = 8
HG_DK = 128
HG_DV = 128
HG_W = HG_HEADS * HG_DK
HG_CHUNK = 64
MLA_HEADS = 8
MLA_Q_RANK = 512
MLA_KV_RANK = 512
MLA_NOPE = 128
MLA_ROPE = 64
MLA_V = 128
MLA_QK = MLA_NOPE + MLA_ROPE
MLA_PAD = MLA_NOPE + LANES
VT_ROWS = MLA_V + BF16_SUBLANES
LOG2E = 1.4426950408889634
RET_HEADS = 8
RET_DK = 256
RET_DV = 512
RET_CHUNK = 256
CONV_W = 3
FFN_STRIP = MXU_DIM
ROPE_BASE = 10000.0
EPS = 1e-6

VMEM_LIMIT = VMEM_BYTES - 4 * 2**20
CAST_BLOCK_BYTES = 8 * 2**20

ROWS = 1024
NORM_ROWS = 256
COLS = 2048
FFN_COLS = 512
PREP_ROWS = 512
HGRN_ROWS = 512
RET_ROWS = 2048
FLASH_Q = 2048
FLASH_KV = 1024
FLASH_KV_DIAG = 512

NT = (((1,), (1,)), ((), ()))
TN = (((0,), (0,)), ((), ()))


def _params(*sem):
    return pltpu.CompilerParams(dimension_semantics=sem, vmem_limit_bytes=VMEM_LIMIT)


def _tile(n, pref):
    t = min(n, pref)
    assert n % t == 0, (n, pref)
    return t


def _dot(a, b):
    return jnp.dot(a, b, preferred_element_type=F32)


def _rms(x, g=None):
    y = x * lax.rsqrt(jnp.sum(x * x, axis=-1, keepdims=True) * (1.0 / x.shape[-1]) + EPS)
    return y if g is None else y * g


def _norm_into(x_ref, g_ref, dst_ref, rows=NORM_ROWS):
    g = g_ref[...]

    def body(c, carry):
        r0 = pl.multiple_of(c * rows, rows)
        dst_ref[pl.ds(r0, rows), :] = _rms(x_ref[pl.ds(r0, rows), :], g).astype(dst_ref.dtype)
        return carry

    lax.fori_loop(0, x_ref.shape[0] // rows, body, 0)


def _cast_kernel(w_ref, o_ref):
    o_ref[...] = w_ref[...].astype(o_ref.dtype)


def cast_bf16(w, layer, ncols=None, name="cast_bf16"):
    _, r, c = w.shape
    ncols = c if ncols is None else ncols
    tr = r
    while tr % 16 == 0 and tr * ncols * 4 > CAST_BLOCK_BYTES:
        tr //= 2
    return pl.pallas_call(
        _cast_kernel,
        grid=(r // tr,),
        in_specs=[pl.BlockSpec((None, tr, ncols), lambda i: (layer, i, 0))],
        out_specs=pl.BlockSpec((tr, ncols), lambda i: (i, 0)),
        out_shape=jax.ShapeDtypeStruct((r, ncols), BF16),
        compiler_params=_params("parallel"),
        name=name,
    )(w)


def _select_lanes(x, sel):
    hi = x.astype(BF16)
    r1 = x - hi.astype(F32)
    mid = r1.astype(BF16)
    lo = (r1 - mid.astype(F32)).astype(BF16)
    return _dot(hi, sel) + _dot(mid, sel) + _dot(lo, sel)


def _rope_tab_kernel(pos_ref, f_ref, selc_ref, sels_ref, c_ref, s_ref, cm_ref, sm_ref):
    ang = pos_ref[...].astype(F32) * f_ref[...]
    c, s = jnp.cos(ang), jnp.sin(ang)
    c_ref[...] = c
    s_ref[...] = s
    cm_ref[...] = _select_lanes(c, selc_ref[...])
    sm_ref[...] = _select_lanes(s, sels_ref[...])


def rope_tables(pos_col, freq, sel_cos, sel_sin):
    t = pos_col.shape[0]
    tm = _tile(t, ROWS)
    const = lambda a: pl.BlockSpec(a.shape, lambda i: (0, 0))
    return pl.pallas_call(
        _rope_tab_kernel,
        grid=(t // tm,),
        in_specs=[pl.BlockSpec((tm, 1), lambda i: (i, 0)), const(freq), const(sel_cos), const(sel_sin)],
        out_specs=[pl.BlockSpec((tm, LANES), lambda i: (i, 0))] * 4,
        out_shape=[jax.ShapeDtypeStruct((t, LANES), F32)] * 4,
        compiler_params=_params("parallel"),
        name="rope_tables",
    )(pos_col, freq, sel_cos, sel_sin)


def _norm_matmul_kernel(x_ref, g_ref, w_ref, o_ref, xn_ref):
    j = pl.program_id(1)
    tn = o_ref.shape[1]

    @pl.when(j == 0)
    def _():
        _norm_into(x_ref, g_ref, xn_ref)

    w = w_ref[:, pl.ds(pl.multiple_of(j * tn, tn), tn)]
    o_ref[...] = _dot(xn_ref[...], w).astype(o_ref.dtype)


def norm_matmul(x, g, w, out_dtype, tm, tn, name):
    t, k = x.shape
    n = w.shape[1]
    tm, tn = _tile(t, tm), _tile(n, tn)
    return pl.pallas_call(
        _norm_matmul_kernel,
        grid=(t // tm, n // tn),
        in_specs=[pl.BlockSpec((tm, k), lambda i, j: (i, 0)),
                  pl.BlockSpec((1, k), lambda i, j: (0, 0)),
                  pl.BlockSpec((k, n), lambda i, j: (0, 0), pipeline_mode=pl.Buffered(1))],
        out_specs=pl.BlockSpec((tm, tn), lambda i, j: (i, j)),
        out_shape=jax.ShapeDtypeStruct((t, n), out_dtype),
        scratch_shapes=[pltpu.VMEM((tm, k), BF16)],
        compiler_params=_params("parallel", "arbitrary"),
        name=name,
    )(x, g, w)


def _ret_in_kernel(x_ref, g_ref, w_ref, cos_ref, sin_ref, o_ref, xn_ref, *, n_q_tiles, n_rope_tiles):
    j = pl.program_id(1)

    @pl.when(j == 0)
    def _():
        _norm_into(x_ref, g_ref, xn_ref)

    acc = _dot(xn_ref[...], w_ref[...])
    is_rope = j < n_rope_tiles
    scale = jnp.where(j >= n_q_tiles, RET_DK ** -0.5, 1.0).astype(F32)
    cos = cos_ref[...] * scale
    sin = sin_ref[...] * scale
    half = RET_DK // 2
    for hh in range(acc.shape[1] // RET_DK):
        x1 = acc[:, hh * RET_DK:hh * RET_DK + half]
        x2 = acc[:, hh * RET_DK + half:(hh + 1) * RET_DK]
        o_ref[:, hh * RET_DK:hh * RET_DK + half] = jnp.where(
            is_rope, x1 * cos - x2 * sin, x1).astype(o_ref.dtype)
        o_ref[:, hh * RET_DK + half:(hh + 1) * RET_DK] = jnp.where(
            is_rope, x2 * cos + x1 * sin, x2).astype(o_ref.dtype)


def ret_in_proj(x, g, w, cos, sin, tm, tn):
    t, k = x.shape
    n = w.shape[1]
    tm, tn = _tile(t, tm), _tile(n, tn)
    qw = RET_HEADS * RET_DK
    kern = functools.partial(_ret_in_kernel, n_q_tiles=qw // tn, n_rope_tiles=2 * qw // tn)
    return pl.pallas_call(
        kern,
        grid=(t // tm, n // tn),
        in_specs=[pl.BlockSpec((tm, k), lambda i, j: (i, 0)),
                  pl.BlockSpec((1, k), lambda i, j: (0, 0)),
                  pl.BlockSpec((k, tn), lambda i, j: (0, j)),
                  pl.BlockSpec((tm, RET_DK // 2), lambda i, j: (i, 0)),
                  pl.BlockSpec((tm, RET_DK // 2), lambda i, j: (i, 0))],
        out_specs=pl.BlockSpec((tm, tn), lambda i, j: (i, j)),
        out_shape=jax.ShapeDtypeStruct((t, n), BF16),
        scratch_shapes=[pltpu.VMEM((tm, k), BF16)],
        compiler_params=_params("parallel", "arbitrary"),
        name="ret_in_proj",
    )(x, g, w, cos, sin)


def _res_matmul_kernel(*refs):
    h_ref, w_ref, o_ref = refs[0], refs[-2], refs[-1]
    lhs = [a_ref[...] for a_ref in refs[1:-2]]
    lhs = lhs[0] if len(lhs) == 1 else jnp.concatenate(lhs, axis=1)
    o_ref[...] = h_ref[...] + _dot(lhs, w_ref[...])


def res_matmul(h, lhs, w, tm, tn, name):
    t, n = h.shape
    tm, tn = _tile(t, tm), _tile(n, tn)
    assert sum(a.shape[1] for a in lhs) == w.shape[0]
    in_specs = [pl.BlockSpec((tm, tn), lambda i, j: (i, j))]
    in_specs += [pl.BlockSpec((tm, a.shape[1]), lambda i, j: (i, 0)) for a in lhs]
    in_specs += [pl.BlockSpec((w.shape[0], tn), lambda i, j: (0, j),
                              pipeline_mode=pl.Buffered(1) if tn == n else None)]
    return pl.pallas_call(
        _res_matmul_kernel,
        grid=(t // tm, n // tn),
        in_specs=in_specs,
        out_specs=pl.BlockSpec((tm, tn), lambda i, j: (i, j)),
        out_shape=jax.ShapeDtypeStruct((t, n), F32),
        compiler_params=_params("parallel", "arbitrary"),
        name=name,
    )(h, *lhs, w)


def _cast_gate_up_kernel(wg_ref, wu_ref, o_ref):
    for s in range(wg_ref.shape[1] // FFN_STRIP):
        src = slice(s * FFN_STRIP, (s + 1) * FFN_STRIP)
        o_ref[:, 2 * s * FFN_STRIP:(2 * s + 1) * FFN_STRIP] = wg_ref[:, src].astype(o_ref.dtype)
        o_ref[:, (2 * s + 1) * FFN_STRIP:(2 * s + 2) * FFN_STRIP] = wu_ref[:, src].astype(o_ref.dtype)


def cast_gate_up(wg, wu, layer, tf):
    _, r, c = wg.shape
    spec = pl.BlockSpec((None, r, tf), lambda j: (layer, 0, j))
    return pl.pallas_call(
        _cast_gate_up_kernel,
        grid=(c // tf,),
        in_specs=[spec, spec],
        out_specs=pl.BlockSpec((r, 2 * tf), lambda j: (0, j)),
        out_shape=jax.ShapeDtypeStruct((r, 2 * c), BF16),
        compiler_params=_params("parallel"),
        name="cast_w_gate_up",
    )(wg, wu)


def _ffn_kernel(h_ref, g_ref, wgu_ref, cw_ref, cb_ref, wd_ref, o_ref,
                u_ref, gs_ref, tail_ref, *, blocks_per_seq):
    i, j = pl.program_id(0), pl.program_id(1)
    tm = h_ref.shape[0]

    @pl.when(j == 0)
    def _():
        _norm_into(h_ref, g_ref, u_ref)
        o_ref[...] = h_ref[...]

    @pl.when(jnp.logical_and(i == 0, j == 0))
    def _():
        tail_ref[...] = jnp.zeros(tail_ref.shape, F32)

    u = u_ref[...]
    seq_start = (i % blocks_per_seq) == 0
    prev = jnp.where(seq_start, 0.0, tail_ref[j])
    cw, cb = cw_ref[...], cb_ref[...]
    gate_up = _dot(u, wgu_ref[...])
    contrib = None
    for s in range(wd_ref.shape[0] // FFN_STRIP):
        cols = slice(s * FFN_STRIP, (s + 1) * FFN_STRIP)
        a = gate_up[:, 2 * s * FFN_STRIP:(2 * s + 1) * FFN_STRIP]
        up = gate_up[:, (2 * s + 1) * FFN_STRIP:(2 * s + 2) * FFN_STRIP]
        gs_ref[0:SUBLANES, cols] = prev[:, cols]
        gs_ref[SUBLANES:SUBLANES + tm, cols] = a
        tail_ref[j, :, cols] = a[tm - SUBLANES:tm, :]
        c = cb[:, cols] + cw[CONV_W - 1:CONV_W, cols] * a
        for tap in range(CONV_W - 1):
            r0 = SUBLANES - (CONV_W - 1 - tap)
            c = c + cw[tap:tap + 1, cols] * gs_ref[r0:r0 + tm, cols]
        act = (c * jax.nn.sigmoid(c) * up).astype(BF16)
        part = _dot(act, wd_ref[cols, :])
        contrib = part if contrib is None else contrib + part
    o_ref[...] += contrib


def conv_ffn(h, g, wgu, cw, cb, wd, seq, tm, tf):
    t, d = h.shape
    f = wd.shape[0]
    tm, tf = _tile(seq, tm), _tile(f, tf)
    kern = functools.partial(_ffn_kernel, blocks_per_seq=seq // tm)
    return pl.pallas_call(
        kern,
        grid=(t // tm, f // tf),
        in_specs=[pl.BlockSpec((tm, d), lambda i, j: (i, 0)),
                  pl.BlockSpec((1, d), lambda i, j: (0, 0)),
                  pl.BlockSpec((d, 2 * tf), lambda i, j: (0, j)),
                  pl.BlockSpec((CONV_W, tf), lambda i, j: (0, j)),
                  pl.BlockSpec((1, tf), lambda i, j: (0, j)),
                  pl.BlockSpec((tf, d), lambda i, j: (j, 0))],
        out_specs=pl.BlockSpec((tm, d), lambda i, j: (i, 0)),
        out_shape=jax.ShapeDtypeStruct((t, d), F32),
        scratch_shapes=[pltpu.VMEM((tm, d), BF16),
                        pltpu.VMEM((tm + SUBLANES, tf), F32),
                        pltpu.VMEM((f // tf, SUBLANES, tf), F32)],
        compiler_params=_params("arbitrary", "arbitrary"),
        name="conv_ffn",
    )(h, g, wgu, cw, cb, wd)


def _ple_kernel(h_ref, g_ref, wgate_ref, p_ref, wproj_ref, o_ref, xn_ref, pb_ref):
    j = pl.program_id(1)
    tn = o_ref.shape[1]

    @pl.when(j == 0)
    def _():
        _norm_into(h_ref, g_ref, xn_ref)
        pb_ref[...] = p_ref[...].astype(BF16)

    gate = jax.nn.sigmoid(_dot(xn_ref[...], wgate_ref[...]))
    proj = _dot(pb_ref[...], wproj_ref[...])
    c0 = pl.multiple_of(j * tn, tn)
    o_ref[...] = h_ref[:, pl.ds(c0, tn)] + proj * gate


def ple(h, g, wgate, p, layer, wproj, tm, tn):
    t, d = h.shape
    tm, tn = _tile(t, tm), _tile(d, tn)
    return pl.pallas_call(
        _ple_kernel,
        grid=(t // tm, d // tn),
        in_specs=[pl.BlockSpec((tm, d), lambda i, j: (i, 0)),
                  pl.BlockSpec((1, d), lambda i, j: (0, 0)),
                  pl.BlockSpec((d, tn), lambda i, j: (0, j)),
                  pl.BlockSpec((None, tm, PLE_DIM), lambda i, j: (layer, i, 0)),
                  pl.BlockSpec((PLE_DIM, tn), lambda i, j: (0, j))],
        out_specs=pl.BlockSpec((tm, tn), lambda i, j: (i, j)),
        out_shape=jax.ShapeDtypeStruct((t, d), F32),
        scratch_shapes=[pltpu.VMEM((tm, d), BF16), pltpu.VMEM((tm, PLE_DIM), BF16)],
        compiler_params=_params("parallel", "arbitrary"),
        name="ple",
    )(h, g, wgate, p, wproj)


def _hgrn_kernel(q_ref, f_ref, i_ref, g_ref, lbl_ref, on_ref, o_ref, st_ref):
    c = HG_CHUNK

    @pl.when(pl.program_id(1) == 0)
    def _():
        st_ref[...] = jnp.zeros(st_ref.shape, F32)

    lg = lbl_ref[...]
    e = jnp.exp(lg - jnp.max(lg, axis=0, keepdims=True))
    lb = e[0:1, :] / jnp.sum(e, axis=0, keepdims=True)
    onorm = on_ref[...]
    pw = 2 * HG_DK
    row = lax.broadcasted_iota(jnp.int32, (c, c), 0)
    col = lax.broadcasted_iota(jnp.int32, (c, c), 1)
    tril = jnp.where(row >= col, 1.0, 0.0).astype(BF16)
    row2 = lax.broadcasted_iota(jnp.int32, (c, 2 * c), 0)
    col2 = lax.broadcasted_iota(jnp.int32, (c, 2 * c), 1)
    causal2 = row2 >= jnp.where(col2 >= c, col2 - c, col2)
    left = lax.broadcasted_iota(jnp.int32, (c, pw), 1) < HG_DK
    same_head = ((lax.broadcasted_iota(jnp.int32, (pw, pw), 0) < HG_DV)
                 == (lax.broadcasted_iota(jnp.int32, (pw, pw), 1) < HG_DK))

    def block_diag(x):
        zero = jnp.zeros_like(x)
        return jnp.concatenate([jnp.where(left, x, zero), jnp.where(left, zero, x)], axis=0)

    def chunk(ci, carry):
        rows = pl.ds(pl.multiple_of(ci * c, c), c)
        q = q_ref[rows, :]
        v = i_ref[rows, :].astype(BF16)
        f = lb + (1.0 - lb) * jax.nn.sigmoid(f_ref[rows, :])
        k = 1.0 - f
        lf = jnp.log(f)
        hi = lf.astype(BF16)
        r1 = lf - hi.astype(F32)
        mid = r1.astype(BF16)
        lo = (r1 - mid.astype(F32)).astype(BF16)
        bc = _dot(tril, hi) + _dot(tril, mid) + _dot(tril, lo)
        ref = bc[c // 2 - 1:c // 2, :]
        b_last = bc[c - 1:c, :]
        q_rel = (q * jnp.exp(bc - ref)).astype(BF16)
        k_rel = (k * jnp.exp(ref - bc)).astype(BF16)
        q_dec = (q * jnp.exp(bc)).astype(BF16)
        k_dec = (k * jnp.exp(b_last - bc)).astype(BF16)
        decay = jnp.exp(b_last)
        for p in range(HG_HEADS // 2):
            cols = slice(p * pw, (p + 1) * pw)
            a = lax.dot_general(q_rel[:, cols], block_diag(k_rel[:, cols]), NT, preferred_element_type=F32)
            a = jnp.where(causal2, a, 0.0).astype(BF16)
            st = st_ref[p]
            out = _dot(a, block_diag(v[:, cols])) + lax.dot_general(q_dec[:, cols], st.astype(BF16), NT,
                                                                    preferred_element_type=F32)
            upd = lax.dot_general(v[:, cols], k_dec[:, cols], TN, preferred_element_type=F32)
            st_ref[p] = st * decay[:, cols] + jnp.where(same_head, upd, 0.0)
            for hh in range(2):
                hc = slice(p * pw + hh * HG_DV, p * pw + (hh + 1) * HG_DV)
                g = g_ref[rows, hc]
                o_ref[rows, hc] = (_rms(out[:, hh * HG_DV:(hh + 1) * HG_DV], onorm)
                                   * (g * jax.nn.sigmoid(g))).astype(o_ref.dtype)
        return carry

    lax.fori_loop(0, q_ref.shape[0] // c, chunk, 0, unroll=4)


def hgrn(z, lb_logits, onorm, batch, seq, cb):
    t = z.shape[0]
    cb = _tile(seq, cb)
    nblk = seq // cb
    spec = lambda part: pl.BlockSpec((cb, HG_W), lambda b, s, part=part: (b * nblk + s, part))
    return pl.pallas_call(
        _hgrn_kernel,
        grid=(batch, nblk),
        in_specs=[spec(0), spec(1), spec(2), spec(3),
                  pl.BlockSpec(lb_logits.shape, lambda b, s: (0, 0)),
                  pl.BlockSpec((1, HG_DV), lambda b, s: (0, 0))],
        out_specs=pl.BlockSpec((cb, HG_W), lambda b, s: (b * nblk + s, 0)),
        out_shape=jax.ShapeDtypeStruct((t, HG_W), BF16),
        scratch_shapes=[pltpu.VMEM((HG_HEADS // 2, 2 * HG_DV, 2 * HG_DK), F32)],
        compiler_params=_params("parallel", "arbitrary"),
        name="hgrn2",
    )(z, z, z, z, lb_logits, onorm)


def _mla_prep_kernel(z_ref, qa_ref, kva_ref, wuq_ref, wukv_ref, qn_ref, kn_ref, cos_ref, sin_ref,
                     qt_out, k_out, vt_out):
    cq = _rms(z_ref[:, 0:MLA_Q_RANK], qa_ref[...]).astype(BF16)
    ckv = _rms(z_ref[:, MLA_Q_RANK:MLA_Q_RANK + MLA_KV_RANK], kva_ref[...]).astype(BF16)
    kpe = z_ref[:, MLA_Q_RANK + MLA_KV_RANK:]
    qf = _dot(cq, wuq_ref[...])
    kvf = _dot(ckv, wukv_ref[...])
    cos, sin = cos_ref[...], sin_ref[...]
    qn, kn = qn_ref[...], kn_ref[...]
    scale = MLA_QK ** -0.5 * LOG2E
    tm = z_ref.shape[0]

    def rope(x):
        return x * cos + pltpu.roll(x, LANES // 2, 1) * sin

    kpe_ss = jnp.sum(kpe * kpe, axis=-1, keepdims=True)
    k_rope = rope(kpe * kn[:, MLA_NOPE:])
    ones_rows = jnp.where(lax.broadcasted_iota(jnp.int32, (VT_ROWS - MLA_V, tm), 0) == 0, 1.0, 0.0)
    for h in range(MLA_HEADS):
        c0 = h * MLA_PAD
        qh = qf[:, c0:c0 + MLA_PAD]
        rq = lax.rsqrt(jnp.sum(qh * qh, axis=-1, keepdims=True) * (1.0 / MLA_QK) + EPS) * scale
        qh = qh * rq * qn
        qt_out[h, 0:MLA_NOPE, :] = qh[:, :MLA_NOPE].T.astype(BF16)
        qt_out[h, MLA_NOPE:MLA_PAD, :] = rope(qh[:, MLA_NOPE:]).T.astype(BF16)
        kh = kvf[:, c0:c0 + MLA_NOPE]
        rk = lax.rsqrt((jnp.sum(kh * kh, axis=-1, keepdims=True) + kpe_ss) * (1.0 / MLA_QK) + EPS)
        k_out[:, c0:c0 + MLA_NOPE] = (kh * rk * kn[:, :MLA_NOPE]).astype(BF16)
        k_out[:, c0 + MLA_NOPE:c0 + MLA_PAD] = (k_rope * rk).astype(BF16)
        vt_out[h, 0:MLA_V, :] = kvf[:, c0 + MLA_NOPE:c0 + MLA_PAD].T.astype(BF16)
        vt_out[h, MLA_V:VT_ROWS, :] = ones_rows.astype(BF16)


def mla_prep(z, qa, kva, wuq, wukv, qn, kn, cos, sin, tm):
    t, zw = z.shape
    tm = _tile(t, tm)
    full = lambda a: pl.BlockSpec(a.shape, lambda i: (0, 0))
    rows = lambda w: pl.BlockSpec((tm, w), lambda i: (i, 0))
    cols = lambda r: pl.BlockSpec((MLA_HEADS, r, tm), lambda i: (0, 0, i))
    return pl.pallas_call(
        _mla_prep_kernel,
        grid=(t // tm,),
        in_specs=[rows(zw), full(qa), full(kva), full(wuq), full(wukv), full(qn), full(kn),
                  rows(LANES), rows(LANES)],
        out_specs=[cols(MLA_PAD), rows(MLA_HEADS * MLA_PAD), cols(VT_ROWS)],
        out_shape=[jax.ShapeDtypeStruct((MLA_HEADS, MLA_PAD, t), BF16),
                   jax.ShapeDtypeStruct((t, MLA_HEADS * MLA_PAD), BF16),
                   jax.ShapeDtypeStruct((MLA_HEADS, VT_ROWS, t), BF16)],
        compiler_params=_params("parallel"),
        name="mla_prep",
    )(z, qa, kva, wuq, wukv, qn, kn, cos, sin)


def _flash_kernel(qt_ref, k_ref, vt_ref, o_ref, acc_ref, s_ref, *, tq, tk, tk_main):
    qi = pl.program_id(2)
    qt = qt_ref[0]
    acc_ref[...] = jnp.zeros(acc_ref.shape, F32)

    def scores(k0, kw, q_lo):
        return _dot(k_ref[pl.ds(k0, kw), :], qt[:, q_lo:])

    def update(s, k0, kw, m_all, q_lo):
        m_prev = m_all[:, q_lo:]
        m_new = jnp.maximum(m_prev, jnp.max(s, axis=0, keepdims=True))
        alpha = jnp.exp2(m_prev - m_new)
        p = jnp.exp2(s - m_new).astype(BF16)
        acc_ref[:, q_lo:] = alpha * acc_ref[:, q_lo:] + _dot(vt_ref[0, :, pl.ds(k0, kw)], p)
        return m_new if q_lo == 0 else jnp.concatenate([m_all[:, :q_lo], m_new], axis=1)

    n_pairs = qi
    s_ref[0] = scores(0, tk_main, 0)

    def pair(t, m):
        k0 = pl.multiple_of(t * (2 * tk_main), 2 * tk_main)
        s_ref[1] = scores(k0 + tk_main, tk_main, 0)
        m = update(s_ref[0], k0, tk_main, m, 0)
        s_ref[0] = scores(k0 + 2 * tk_main, tk_main, 0)
        return update(s_ref[1], k0 + tk_main, tk_main, m, 0)

    m = lax.fori_loop(0, n_pairs, pair, jnp.full((1, tq), -jnp.inf, F32))
    base = pl.multiple_of(qi * tq, tq)
    s_ref[1, :, tk_main:] = scores(base + tk_main, tk_main, tk_main)
    for d in range(tq // tk):
        e, r0, q_lo = (d * tk) // tk_main, (d * tk) % tk_main, d * tk
        s = s_ref[e, r0:r0 + tk, q_lo:]
        row = lax.broadcasted_iota(jnp.int32, s.shape, 0)
        col = lax.broadcasted_iota(jnp.int32, s.shape, 1)
        m = update(jnp.where(row <= col, s, -jnp.inf), base + q_lo, tk, m, q_lo)
    acc = acc_ref[...]
    o_ref[...] = (acc[0:MLA_V, :] / acc[MLA_V:MLA_V + 1, :]).T.astype(o_ref.dtype)


def flash_attention(qt, k, vt, batch, seq, tq, tk, tk_main):
    t = k.shape[0]
    tq = _tile(seq, tq)
    tk = _tile(tq, tk)
    tk_main = _tile(tq, tk_main)
    assert tq == 2 * tk_main and tk_main % tk == 0
    nq = seq // tq
    return pl.pallas_call(
        functools.partial(_flash_kernel, tq=tq, tk=tk, tk_main=tk_main),
        grid=(batch, MLA_HEADS, nq),
        in_specs=[pl.BlockSpec((1, MLA_PAD, tq), lambda b, h, i: (h, 0, b * nq + i)),
                  pl.BlockSpec((seq, MLA_PAD), lambda b, h, i: (b, h)),
                  pl.BlockSpec((1, VT_ROWS, seq), lambda b, h, i: (h, 0, b))],
        out_specs=pl.BlockSpec((tq, MLA_V), lambda b, h, i: (b * nq + i, h)),
        out_shape=jax.ShapeDtypeStruct((t, MLA_HEADS * MLA_V), BF16),
        scratch_shapes=[pltpu.VMEM((VT_ROWS, tq), F32), pltpu.VMEM((2, tk_main, tq), F32)],
        compiler_params=_params("parallel", "parallel", "arbitrary"),
        name="mla_flash",
    )(qt, k, vt)


def _ret_kernel(q_ref, k_ref, v_ref, g_ref, o_ref, r_ref):
    c = RET_CHUNK
    hf = pl.program_id(1).astype(F32)

    @pl.when(pl.program_id(2) == 0)
    def _():
        r_ref[...] = jnp.zeros(r_ref.shape, F32)

    def log_gamma(shape):
        return jnp.log(1.0 - jnp.exp2(-5.0 - jnp.full(shape, hf, F32)))

    row = lax.broadcasted_iota(jnp.int32, (c, c), 0)
    col = lax.broadcasted_iota(jnp.int32, (c, c), 1)
    diff = (row - col).astype(F32)
    d_intra = jnp.where(diff >= 0, jnp.exp(jnp.where(diff >= 0, diff, 0.0) * log_gamma((c, c))), 0.0)
    idx = lax.broadcasted_iota(jnp.int32, (c, RET_DK), 0).astype(F32)
    lg_k = log_gamma((c, RET_DK))
    q_dec = jnp.exp((idx + 1.0) * lg_k).astype(BF16)
    k_dec = jnp.exp((c - 1.0 - idx) * lg_k).astype(BF16)
    c_dec = jnp.exp(c * log_gamma((1, RET_DV)))

    def chunk(ci, carry):
        rows = pl.ds(pl.multiple_of(ci * c, c), c)
        q, k, v = q_ref[rows, :], k_ref[rows, :], v_ref[rows, :]
        a = lax.dot_general(q, k, NT, preferred_element_type=F32) * d_intra
        r = r_ref[...]
        out = _dot(a.astype(BF16), v) + _dot(q * q_dec, r.astype(BF16))
        r_ref[...] = r * c_dec + lax.dot_general(k * k_dec, v, TN, preferred_element_type=F32)
        g = g_ref[rows, :].astype(F32)
        o_ref[rows, :] = (_rms(out) * (g * jax.nn.sigmoid(g))).astype(o_ref.dtype)
        return carry

    lax.fori_loop(0, q_ref.shape[0] // c, chunk, 0, unroll=2)


def retention(z, batch, seq, cb):
    t = z.shape[0]
    cb = _tile(seq, cb)
    assert cb % RET_CHUNK == 0
    nblk = seq // cb
    nqk = RET_HEADS * RET_DK // RET_DK
    nv = (2 * RET_HEADS * RET_DK) // RET_DV
    return pl.pallas_call(
        _ret_kernel,
        grid=(batch, RET_HEADS, nblk),
        in_specs=[pl.BlockSpec((cb, RET_DK), lambda b, h, s: (b * nblk + s, h)),
                  pl.BlockSpec((cb, RET_DK), lambda b, h, s: (b * nblk + s, nqk + h)),
                  pl.BlockSpec((cb, RET_DV), lambda b, h, s: (b * nblk + s, nv + h)),
                  pl.BlockSpec((cb, RET_DV), lambda b, h, s: (b * nblk + s, nv + RET_HEADS + h))],
        out_specs=pl.BlockSpec((cb, RET_DV), lambda b, h, s: (b * nblk + s, h)),
        out_shape=jax.ShapeDtypeStruct((t, RET_HEADS * RET_DV), BF16),
        scratch_shapes=[pltpu.VMEM((RET_DK, RET_DV), F32)],
        compiler_params=_params("parallel", "parallel", "arbitrary"),
        name="retention",
    )(z, z, z, z)


def _rope_pad(a, axis):
    x1, x2 = jnp.split(a, 2, axis=axis)
    z = jnp.zeros_like(x1)
    return jnp.concatenate([x1, z, x2, z], axis=axis)


def _pad_qk_gain(g):
    return jnp.concatenate([g[:MLA_NOPE], _rope_pad(g[MLA_NOPE:], 0)])[None, :]


def kernel(x, p, positions, norm_mix, norm_ffn, norm_ple, e_w_in, e_lb_logits, e_q_a_norm, e_kv_a_norm, e_w_uq, e_w_ukv, e_q_norm, e_k_norm, e_hg_onorm, e_w_out, o_w_in, o_w_out, ffn_w_gate, ffn_w_up, ffn_conv_w, ffn_conv_b, ffn_w_down, ple_w_proj, ple_w_gate):
    batch, seq, d = x.shape
    assert norm_mix.shape[0] == 2 and e_w_in.shape[0] == 1 and o_w_in.shape[0] == 1
    t = batch * seq
    h = x.reshape(t, d)
    pos_col = positions.reshape(t, 1)

    f_ret = (ROPE_BASE ** (-jnp.arange(RET_DK // 2, dtype=F32) / (RET_DK // 2)))[None, :]
    n_mla, stride = MLA_ROPE // 2, (RET_DK // 2) // (MLA_ROPE // 2)
    src, x1, x2 = jnp.arange(n_mla) * stride, jnp.arange(n_mla), LANES // 2 + jnp.arange(n_mla)
    sel = jnp.zeros((RET_DK // 2, LANES), F32)
    sel_cos = sel.at[src, x1].set(1.0).at[src, x2].set(1.0)
    sel_sin = sel.at[src, x1].set(-1.0).at[src, x2].set(1.0)
    cos_ret, sin_ret, cos_mla, sin_mla = rope_tables(pos_col, f_ret, sel_cos.astype(BF16),
                                                     sel_sin.astype(BF16))

    w_in = e_w_in[0]
    w_hg = cast_bf16(e_w_in, 0, 4 * HG_W, "cast_w_hgrn")
    off = 4 * HG_W + MLA_Q_RANK + MLA_KV_RANK
    w_mla = jnp.concatenate([w_in[:, 4 * HG_W:off], _rope_pad(w_in[:, off:], 1)], axis=1).astype(BF16)
    wuq = e_w_uq[0].reshape(MLA_Q_RANK, MLA_HEADS, MLA_QK)
    wuq = jnp.concatenate([wuq[..., :MLA_NOPE], _rope_pad(wuq[..., MLA_NOPE:], 2)], axis=-1)
    wuq = wuq.reshape(MLA_Q_RANK, MLA_HEADS * MLA_PAD).astype(BF16)
    wukv = e_w_ukv[0].astype(BF16)
    g_mix0 = norm_mix[0][None, :]

    z_hg = norm_matmul(h, g_mix0, w_hg, F32, ROWS, COLS, "in_proj_hgrn")
    z_mla = norm_matmul(h, g_mix0, w_mla, F32, ROWS, w_mla.shape[1], "in_proj_mla")
    o_a = hgrn(z_hg, e_lb_logits, e_hg_onorm[0][None, :], batch, seq, HGRN_ROWS)
    qt, k, vt = mla_prep(z_mla, e_q_a_norm[0][None, :], e_kv_a_norm[0][None, :], wuq, wukv,
                         _pad_qk_gain(e_q_norm[0]), _pad_qk_gain(e_k_norm[0]), cos_mla, sin_mla, PREP_ROWS)
    o_b = flash_attention(qt, k, vt, batch, seq, FLASH_Q, FLASH_KV_DIAG, FLASH_KV)
    h = res_matmul(h, [o_a, o_b], cast_bf16(e_w_out, 0, name="cast_w_even_out"), ROWS, COLS,
                   "out_proj_even")

    def channel_mix(h, i):
        h = conv_ffn(h, norm_ffn[i][None, :], cast_gate_up(ffn_w_gate, ffn_w_up, i, FFN_COLS),
                     ffn_conv_w[i], ffn_conv_b[i][None, :],
                     cast_bf16(ffn_w_down, i, name="cast_w_down"), seq, ROWS, FFN_COLS)
        return ple(h, norm_ple[i][None, :], cast_bf16(ple_w_gate, i, name="cast_w_ple"),
                   p.reshape(p.shape[0], t, PLE_DIM), i, ple_w_proj[i].astype(BF16), ROWS, COLS)

    h = channel_mix(h, 0)

    z_ret = ret_in_proj(h, norm_mix[1][None, :], cast_bf16(o_w_in, 0, name="cast_w_ret_in"),
                        cos_ret, sin_ret, ROWS, COLS)
    o_r = retention(z_ret, batch, seq, RET_ROWS)
    h = res_matmul(h, [o_r], cast_bf16(o_w_out, 0, name="cast_w_ret_out"), ROWS, COLS // 2, "out_proj_odd")
    h = channel_mix(h, 1)
    return h.reshape(batch, seq, d)
```

```python
import functools

import jax
import jax.numpy as jnp
from jax import lax
from jax.experimental import pallas as pl
from jax.experimental.pallas import tpu as pltpu

F32 = jnp.float32
BF16 = jnp.bfloat16

LANES = 128
SUBLANES = 8
BF16_SUBLANES = 16
MXU_DIM = 256
VMEM_BYTES = 64 * 2**20

PLE_DIM = 256
HG_HEADS = 8
HG_DK = 128
HG_DV = 128
HG_W = HG_HEADS * HG_DK
HG_CHUNK = 64
MLA_HEADS = 8
MLA_Q_RANK = 512
MLA_KV_RANK = 512
MLA_NOPE = 128
MLA_ROPE = 64
MLA_V = 128
MLA_QK = MLA_NOPE + MLA_ROPE
MLA_PAD = MLA_NOPE + LANES
VT_ROWS = MLA_V + BF16_SUBLANES
LOG2E = 1.4426950408889634
RET_HEADS = 8
RET_DK = 256
RET_DV = 512
RET_CHUNK = 256
CONV_W = 3
FFN_STRIP = MXU_DIM
ROPE_BASE = 10000.0
EPS = 1e-6

VMEM_LIMIT = VMEM_BYTES - 4 * 2**20
CAST_BLOCK_BYTES = 8 * 2**20

ROWS = 1024
NORM_ROWS = 256
COLS = 2048
FFN_COLS = 512
PREP_ROWS = 512
HGRN_ROWS = 512
RET_ROWS = 2048
FLASH_Q = 2048
FLASH_KV = 1024
FLASH_KV_DIAG = 512

NT = (((1,), (1,)), ((), ()))
TN = (((0,), (0,)), ((), ()))


def _params(*sem):
    return pltpu.CompilerParams(dimension_semantics=sem, vmem_limit_bytes=VMEM_LIMIT)


def _tile(n, pref):
    t = min(n, pref)
    assert n % t == 0, (n, pref)
    return t


def _dot(a, b):
    return jnp.dot(a, b, preferred_element_type=F32)


def _rms(x, g=None):
    y = x * lax.rsqrt(jnp.sum(x * x, axis=-1, keepdims=True) * (1.0 / x.shape[-1]) + EPS)
    return y if g is None else y * g


def _norm_into(x_ref, g_ref, dst_ref, rows=NORM_ROWS):
    g = g_ref[...]

    def body(c, carry):
        r0 = pl.multiple_of(c * rows, rows)
        dst_ref[pl.ds(r0, rows), :] = _rms(x_ref[pl.ds(r0, rows), :], g).astype(dst_ref.dtype)
        return carry

    lax.fori_loop(0, x_ref.shape[0] // rows, body, 0)


def _cast_kernel(w_ref, o_ref):
    o_ref[...] = w_ref[...].astype(o_ref.dtype)


def cast_bf16(w, layer, ncols=None, name="cast_bf16"):
    _, r, c = w.shape
    ncols = c if ncols is None else ncols
    tr = r
    while tr % 16 == 0 and tr * ncols * 4 > CAST_BLOCK_BYTES:
        tr //= 2
    return pl.pallas_call(
        _cast_kernel,
        grid=(r // tr,),
        in_specs=[pl.BlockSpec((None, tr, ncols), lambda i: (layer, i, 0))],
        out_specs=pl.BlockSpec((tr, ncols), lambda i: (i, 0)),
        out_shape=jax.ShapeDtypeStruct((r, ncols), BF16),
        compiler_params=_params("parallel"),
        name=name,
    )(w)


def _select_lanes(x, sel):
    hi = x.astype(BF16)
    r1 = x - hi.astype(F32)
    mid = r1.astype(BF16)
    lo = (r1 - mid.astype(F32)).astype(BF16)
    return _dot(hi, sel) + _dot(mid, sel) + _dot(lo, sel)


def _rope_tab_kernel(pos_ref, f_ref, selc_ref, sels_ref, c_ref, s_ref, cm_ref, sm_ref):
    ang = pos_ref[...].astype(F32) * f_ref[...]
    c, s = jnp.cos(ang), jnp.sin(ang)
    c_ref[...] = c
    s_ref[...] = s
    cm_ref[...] = _select_lanes(c, selc_ref[...])
    sm_ref[...] = _select_lanes(s, sels_ref[...])


def rope_tables(pos_col, freq, sel_cos, sel_sin):
    t = pos_col.shape[0]
    tm = _tile(t, ROWS)
    const = lambda a: pl.BlockSpec(a.shape, lambda i: (0, 0))
    return pl.pallas_call(
        _rope_tab_kernel,
        grid=(t // tm,),
        in_specs=[pl.BlockSpec((tm, 1), lambda i: (i, 0)), const(freq), const(sel_cos), const(sel_sin)],
        out_specs=[pl.BlockSpec((tm, LANES), lambda i: (i, 0))] * 4,
        out_shape=[jax.ShapeDtypeStruct((t, LANES), F32)] * 4,
        compiler_params=_params("parallel"),
        name="rope_tables",
    )(pos_col, freq, sel_cos, sel_sin)


def _norm_matmul_kernel(x_ref, g_ref, w_ref, o_ref, xn_ref):
    j = pl.program_id(1)
    tn = o_ref.shape[1]

    @pl.when(j == 0)
    def _():
        _norm_into(x_ref, g_ref, xn_ref)

    w = w_ref[:, pl.ds(pl.multiple_of(j * tn, tn), tn)]
    o_ref[...] = _dot(xn_ref[...], w).astype(o_ref.dtype)


def norm_matmul(x, g, w, out_dtype, tm, tn, name):
    t, k = x.shape
    n = w.shape[1]
    tm, tn = _tile(t, tm), _tile(n, tn)
    return pl.pallas_call(
        _norm_matmul_kernel,
        grid=(t // tm, n // tn),
        in_specs=[pl.BlockSpec((tm, k), lambda i, j: (i, 0)),
                  pl.BlockSpec((1, k), lambda i, j: (0, 0)),
                  pl.BlockSpec((k, n), lambda i, j: (0, 0), pipeline_mode=pl.Buffered(1))],
        out_specs=[pl.BlockSpec((tm, tn), lambda i, j: (i, j)),
                   pl.BlockSpec((tm, k), lambda i, j: (i, 0))],
        out_shape=[jax.ShapeDtypeStruct((t, n), out_dtype), jax.ShapeDtypeStruct((t, k), BF16)],
        compiler_params=_params("parallel", "arbitrary"),
        name=name,
    )(x, g, w)


def _matmul_kernel(x_ref, w_ref, o_ref):
    o_ref[...] = _dot(x_ref[...], w_ref[...])


def matmul(x, w, tm, name):
    t, k = x.shape
    n = w.shape[1]
    tm = _tile(t, tm)
    return pl.pallas_call(
        _matmul_kernel,
        grid=(t // tm,),
        in_specs=[pl.BlockSpec((tm, k), lambda i: (i, 0)),
                  pl.BlockSpec((k, n), lambda i: (0, 0), pipeline_mode=pl.Buffered(1))],
        out_specs=pl.BlockSpec((tm, n), lambda i: (i, 0)),
        out_shape=jax.ShapeDtypeStruct((t, n), F32),
        compiler_params=_params("parallel"),
        name=name,
    )(x, w)


def _ret_in_kernel(x_ref, g_ref, w_ref, cos_ref, sin_ref, o_ref, xn_ref, *, n_q_tiles, n_rope_tiles):
    j = pl.program_id(1)

    @pl.when(j == 0)
    def _():
        _norm_into(x_ref, g_ref, xn_ref)

    acc = _dot(xn_ref[...], w_ref[...])
    is_rope = j < n_rope_tiles
    scale = jnp.where(j >= n_q_tiles, RET_DK ** -0.5, 1.0).astype(F32)
    cos = cos_ref[...] * scale
    sin = sin_ref[...] * scale
    half = RET_DK // 2
    for hh in range(acc.shape[1] // RET_DK):
        x1 = acc[:, hh * RET_DK:hh * RET_DK + half]
        x2 = acc[:, hh * RET_DK + half:(hh + 1) * RET_DK]
        o_ref[:, hh * RET_DK:hh * RET_DK + half] = jnp.where(
            is_rope, x1 * cos - x2 * sin, x1).astype(o_ref.dtype)
        o_ref[:, hh * RET_DK + half:(hh + 1) * RET_DK] = jnp.where(
            is_rope, x2 * cos + x1 * sin, x2).astype(o_ref.dtype)


def ret_in_proj(x, g, w, cos, sin, tm, tn):
    t, k = x.shape
    n = w.shape[1]
    tm, tn = _tile(t, tm), _tile(n, tn)
    qw = RET_HEADS * RET_DK
    kern = functools.partial(_ret_in_kernel, n_q_tiles=qw // tn, n_rope_tiles=2 * qw // tn)
    return pl.pallas_call(
        kern,
        grid=(t // tm, n // tn),
        in_specs=[pl.BlockSpec((tm, k), lambda i, j: (i, 0)),
                  pl.BlockSpec((1, k), lambda i, j: (0, 0)),
                  pl.BlockSpec((k, tn), lambda i, j: (0, j)),
                  pl.BlockSpec((tm, RET_DK // 2), lambda i, j: (i, 0)),
                  pl.BlockSpec((tm, RET_DK // 2), lambda i, j: (i, 0))],
        out_specs=pl.BlockSpec((tm, tn), lambda i, j: (i, j)),
        out_shape=jax.ShapeDtypeStruct((t, n), BF16),
        scratch_shapes=[pltpu.VMEM((tm, k), BF16)],
        compiler_params=_params("parallel", "arbitrary"),
        name="ret_in_proj",
    )(x, g, w, cos, sin)


def _res_matmul_kernel(*refs):
    h_ref, w_ref, o_ref = refs[0], refs[-2], refs[-1]
    lhs = [a_ref[...] for a_ref in refs[1:-2]]
    lhs = lhs[0] if len(lhs) == 1 else jnp.concatenate(lhs, axis=1)
    o_ref[...] = h_ref[...] + _dot(lhs, w_ref[...])


def res_matmul(h, lhs, w, tm, tn, name):
    t, n = h.shape
    tm, tn = _tile(t, tm), _tile(n, tn)
    assert sum(a.shape[1] for a in lhs) == w.shape[0]
    in_specs = [pl.BlockSpec((tm, tn), lambda i, j: (i, j))]
    in_specs += [pl.BlockSpec((tm, a.shape[1]), lambda i, j: (i, 0)) for a in lhs]
    in_specs += [pl.BlockSpec((w.shape[0], tn), lambda i, j: (0, j),
                              pipeline_mode=pl.Buffered(1) if tn == n else None)]
    return pl.pallas_call(
        _res_matmul_kernel,
        grid=(t // tm, n // tn),
        in_specs=in_specs,
        out_specs=pl.BlockSpec((tm, tn), lambda i, j: (i, j)),
        out_shape=jax.ShapeDtypeStruct((t, n), F32),
        compiler_params=_params("parallel", "arbitrary"),
        name=name,
    )(h, *lhs, w)


def _ffn_kernel(h_ref, g_ref, wg_ref, wu_ref, cw_ref, cb_ref, wd_ref, o_ref,
                u_ref, gs_ref, tail_ref, *, blocks_per_seq):
    i, j = pl.program_id(0), pl.program_id(1)
    tm = h_ref.shape[0]

    @pl.when(j == 0)
    def _():
        _norm_into(h_ref, g_ref, u_ref)
        o_ref[...] = h_ref[...]

    @pl.when(jnp.logical_and(i == 0, j == 0))
    def _():
        tail_ref[...] = jnp.zeros(tail_ref.shape, F32)

    u = u_ref[...]
    seq_start = (i % blocks_per_seq) == 0
    prev = jnp.where(seq_start, 0.0, tail_ref[j])
    cw, cb = cw_ref[...], cb_ref[...]
    contrib = None
    for c0 in range(0, wg_ref.shape[1], FFN_STRIP):
        cols = slice(c0, c0 + FFN_STRIP)
        a = _dot(u, wg_ref[:, cols])
        up = _dot(u, wu_ref[:, cols])
        gs_ref[0:SUBLANES, cols] = prev[:, cols]
        gs_ref[SUBLANES:SUBLANES + tm, cols] = a
        tail_ref[j, :, cols] = a[tm - SUBLANES:tm, :]
        c = cb[:, cols] + cw[CONV_W - 1:CONV_W, cols] * a
        for tap in range(CONV_W - 1):
            r0 = SUBLANES - (CONV_W - 1 - tap)
            c = c + cw[tap:tap + 1, cols] * gs_ref[r0:r0 + tm, cols]
        act = (c * jax.nn.sigmoid(c) * up).astype(BF16)
        part = _dot(act, wd_ref[cols, :])
        contrib = part if contrib is None else contrib + part
    o_ref[...] += contrib


def conv_ffn(h, g, wg, wu, cw, cb, wd, seq, tm, tf):
    t, d = h.shape
    f = wg.shape[1]
    tm, tf = _tile(seq, tm), _tile(f, tf)
    kern = functools.partial(_ffn_kernel, blocks_per_seq=seq // tm)
    return pl.pallas_call(
        kern,
        grid=(t // tm, f // tf),
        in_specs=[pl.BlockSpec((tm, d), lambda i, j: (i, 0)),
                  pl.BlockSpec((1, d), lambda i, j: (0, 0)),
                  pl.BlockSpec((d, tf), lambda i, j: (0, j)),
                  pl.BlockSpec((d, tf), lambda i, j: (0, j)),
                  pl.BlockSpec((CONV_W, tf), lambda i, j: (0, j)),
                  pl.BlockSpec((1, tf), lambda i, j: (0, j)),
                  pl.BlockSpec((tf, d), lambda i, j: (j, 0))],
        out_specs=pl.BlockSpec((tm, d), lambda i, j: (i, 0)),
        out_shape=jax.ShapeDtypeStruct((t, d), F32),
        scratch_shapes=[pltpu.VMEM((tm, d), BF16),
                        pltpu.VMEM((tm + SUBLANES, tf), F32),
                        pltpu.VMEM((f // tf, SUBLANES, tf), F32)],
        compiler_params=_params("arbitrary", "arbitrary"),
        name="conv_ffn",
    )(h, g, wg, wu, cw, cb, wd)


def _ple_kernel(h_ref, g_ref, wgate_ref, p_ref, wproj_ref, o_ref, xn_ref, pb_ref):
    j = pl.program_id(1)
    tn = o_ref.shape[1]

    @pl.when(j == 0)
    def _():
        _norm_into(h_ref, g_ref, xn_ref)
        pb_ref[...] = p_ref[...].astype(BF16)

    gate = jax.nn.sigmoid(_dot(xn_ref[...], wgate_ref[...]))
    proj = _dot(pb_ref[...], wproj_ref[...])
    c0 = pl.multiple_of(j * tn, tn)
    o_ref[...] = h_ref[:, pl.ds(c0, tn)] + proj * gate


def ple(h, g, wgate, p, layer, wproj, tm, tn):
    t, d = h.shape
    tm, tn = _tile(t, tm), _tile(d, tn)
    return pl.pallas_call(
        _ple_kernel,
        grid=(t // tm, d // tn),
        in_specs=[pl.BlockSpec((tm, d), lambda i, j: (i, 0)),
                  pl.BlockSpec((1, d), lambda i, j: (0, 0)),
                  pl.BlockSpec((d, tn), lambda i, j: (0, j)),
                  pl.BlockSpec((None, tm, PLE_DIM), lambda i, j: (layer, i, 0)),
                  pl.BlockSpec((PLE_DIM, tn), lambda i, j: (0, j))],
        out_specs=pl.BlockSpec((tm, tn), lambda i, j: (i, j)),
        out_shape=jax.ShapeDtypeStruct((t, d), F32),
        scratch_shapes=[pltpu.VMEM((tm, d), BF16), pltpu.VMEM((tm, PLE_DIM), BF16)],
        compiler_params=_params("parallel", "arbitrary"),
        name="ple",
    )(h, g, wgate, p, wproj)


def _hgrn_kernel(q_ref, f_ref, i_ref, g_ref, lbl_ref, on_ref, o_ref, st_ref):
    c = HG_CHUNK

    @pl.when(pl.program_id(1) == 0)
    def _():
        st_ref[...] = jnp.zeros(st_ref.shape, F32)

    lg = lbl_ref[...]
    e = jnp.exp(lg - jnp.max(lg, axis=0, keepdims=True))
    lb = e[0:1, :] / jnp.sum(e, axis=0, keepdims=True)
    onorm = on_ref[...]
    pw = 2 * HG_DK
    row = lax.broadcasted_iota(jnp.int32, (c, c), 0)
    col = lax.broadcasted_iota(jnp.int32, (c, c), 1)
    tril = jnp.where(row >= col, 1.0, 0.0).astype(BF16)
    row2 = lax.broadcasted_iota(jnp.int32, (c, 2 * c), 0)
    col2 = lax.broadcasted_iota(jnp.int32, (c, 2 * c), 1)
    causal2 = row2 >= jnp.where(col2 >= c, col2 - c, col2)
    left = lax.broadcasted_iota(jnp.int32, (c, pw), 1) < HG_DK
    same_head = ((lax.broadcasted_iota(jnp.int32, (pw, pw), 0) < HG_DV)
                 == (lax.broadcasted_iota(jnp.int32, (pw, pw), 1) < HG_DK))

    def block_diag(x):
        zero = jnp.zeros_like(x)
        return jnp.concatenate([jnp.where(left, x, zero), jnp.where(left, zero, x)], axis=0)

    def chunk(ci, carry):
        rows = pl.ds(pl.multiple_of(ci * c, c), c)
        q = q_ref[rows, :]
        v = i_ref[rows, :].astype(BF16)
        f = lb + (1.0 - lb) * jax.nn.sigmoid(f_ref[rows, :])
        k = 1.0 - f
        lf = jnp.log(f)
        hi = lf.astype(BF16)
        r1 = lf - hi.astype(F32)
        mid = r1.astype(BF16)
        lo = (r1 - mid.astype(F32)).astype(BF16)
        bc = _dot(tril, hi) + _dot(tril, mid) + _dot(tril, lo)
        ref = bc[c // 2 - 1:c // 2, :]
        b_last = bc[c - 1:c, :]
        q_rel = (q * jnp.exp(bc - ref)).astype(BF16)
        k_rel = (k * jnp.exp(ref - bc)).astype(BF16)
        q_dec = (q * jnp.exp(bc)).astype(BF16)
        k_dec = (k * jnp.exp(b_last - bc)).astype(BF16)
        decay = jnp.exp(b_last)
        for p in range(HG_HEADS // 2):
            cols = slice(p * pw, (p + 1) * pw)
            a = lax.dot_general(q_rel[:, cols], block_diag(k_rel[:, cols]), NT, preferred_element_type=F32)
            a = jnp.where(causal2, a, 0.0).astype(BF16)
            st = st_ref[p]
            out = _dot(a, block_diag(v[:, cols])) + lax.dot_general(q_dec[:, cols], st.astype(BF16), NT,
                                                                    preferred_element_type=F32)
            upd = lax.dot_general(v[:, cols], k_dec[:, cols], TN, preferred_element_type=F32)
            st_ref[p] = st * decay[:, cols] + jnp.where(same_head, upd, 0.0)
            for hh in range(2):
                hc = slice(p * pw + hh * HG_DV, p * pw + (hh + 1) * HG_DV)
                g = g_ref[rows, hc]
                o_ref[rows, hc] = (_rms(out[:, hh * HG_DV:(hh + 1) * HG_DV], onorm)
                                   * (g * jax.nn.sigmoid(g))).astype(o_ref.dtype)
        return carry

    lax.fori_loop(0, q_ref.shape[0] // c, chunk, 0, unroll=4)


def hgrn(z, lb_logits, onorm, batch, seq, cb):
    t = z.shape[0]
    cb = _tile(seq, cb)
    nblk = seq // cb
    spec = lambda part: pl.BlockSpec((cb, HG_W), lambda b, s, part=part: (b * nblk + s, part))
    return pl.pallas_call(
        _hgrn_kernel,
        grid=(batch, nblk),
        in_specs=[spec(0), spec(1), spec(2), spec(3),
                  pl.BlockSpec(lb_logits.shape, lambda b, s: (0, 0)),
                  pl.BlockSpec((1, HG_DV), lambda b, s: (0, 0))],
        out_specs=pl.BlockSpec((cb, HG_W), lambda b, s: (b * nblk + s, 0)),
        out_shape=jax.ShapeDtypeStruct((t, HG_W), BF16),
        scratch_shapes=[pltpu.VMEM((HG_HEADS // 2, 2 * HG_DV, 2 * HG_DK), F32)],
        compiler_params=_params("parallel", "arbitrary"),
        name="hgrn2",
    )(z, z, z, z, lb_logits, onorm)


def _mla_prep_kernel(z_ref, qa_ref, kva_ref, wuq_ref, wukv_ref, qn_ref, kn_ref, cos_ref, sin_ref,
                     qt_out, k_out, vt_out):
    cq = _rms(z_ref[:, 0:MLA_Q_RANK], qa_ref[...]).astype(BF16)
    ckv = _rms(z_ref[:, MLA_Q_RANK:MLA_Q_RANK + MLA_KV_RANK], kva_ref[...]).astype(BF16)
    kpe = z_ref[:, MLA_Q_RANK + MLA_KV_RANK:]
    qf = _dot(cq, wuq_ref[...])
    kvf = _dot(ckv, wukv_ref[...])
    cos, sin = cos_ref[...], sin_ref[...]
    qn, kn = qn_ref[...], kn_ref[...]
    scale = MLA_QK ** -0.5 * LOG2E
    tm = z_ref.shape[0]

    def rope(x):
        return x * cos + pltpu.roll(x, LANES // 2, 1) * sin

    kpe_ss = jnp.sum(kpe * kpe, axis=-1, keepdims=True)
    k_rope = rope(kpe * kn[:, MLA_NOPE:])
    ones_rows = jnp.where(lax.broadcasted_iota(jnp.int32, (VT_ROWS - MLA_V, tm), 0) == 0, 1.0, 0.0)
    for h in range(MLA_HEADS):
        c0 = h * MLA_PAD
        qh = qf[:, c0:c0 + MLA_PAD]
        rq = lax.rsqrt(jnp.sum(qh * qh, axis=-1, keepdims=True) * (1.0 / MLA_QK) + EPS) * scale
        qh = qh * rq * qn
        qt_out[h, 0:MLA_NOPE, :] = qh[:, :MLA_NOPE].T.astype(BF16)
        qt_out[h, MLA_NOPE:MLA_PAD, :] = rope(qh[:, MLA_NOPE:]).T.astype(BF16)
        kh = kvf[:, c0:c0 + MLA_NOPE]
        rk = lax.rsqrt((jnp.sum(kh * kh, axis=-1, keepdims=True) + kpe_ss) * (1.0 / MLA_QK) + EPS)
        k_out[:, c0:c0 + MLA_NOPE] = (kh * rk * kn[:, :MLA_NOPE]).astype(BF16)
        k_out[:, c0 + MLA_NOPE:c0 + MLA_PAD] = (k_rope * rk).astype(BF16)
        vt_out[h, 0:MLA_V, :] = kvf[:, c0 + MLA_NOPE:c0 + MLA_PAD].T.astype(BF16)
        vt_out[h, MLA_V:VT_ROWS, :] = ones_rows.astype(BF16)


def mla_prep(z, qa, kva, wuq, wukv, qn, kn, cos, sin, tm):
    t, zw = z.shape
    tm = _tile(t, tm)
    full = lambda a: pl.BlockSpec(a.shape, lambda i: (0, 0))
    rows = lambda w: pl.BlockSpec((tm, w), lambda i: (i, 0))
    cols = lambda r: pl.BlockSpec((MLA_HEADS, r, tm), lambda i: (0, 0, i))
    return pl.pallas_call(
        _mla_prep_kernel,
        grid=(t // tm,),
        in_specs=[rows(zw), full(qa), full(kva), full(wuq), full(wukv), full(qn), full(kn),
                  rows(LANES), rows(LANES)],
        out_specs=[cols(MLA_PAD), rows(MLA_HEADS * MLA_PAD), cols(VT_ROWS)],
        out_shape=[jax.ShapeDtypeStruct((MLA_HEADS, MLA_PAD, t), BF16),
                   jax.ShapeDtypeStruct((t, MLA_HEADS * MLA_PAD), BF16),
                   jax.ShapeDtypeStruct((MLA_HEADS, VT_ROWS, t), BF16)],
        compiler_params=_params("parallel"),
        name="mla_prep",
    )(z, qa, kva, wuq, wukv, qn, kn, cos, sin)


def _flash_kernel(qt_ref, k_ref, vt_ref, o_ref, acc_ref, s_ref, *, tq, tk, tk_main):
    qi = pl.program_id(2)
    qt = qt_ref[0]
    acc_ref[...] = jnp.zeros(acc_ref.shape, F32)

    def scores(k0, kw, q_lo):
        return _dot(k_ref[pl.ds(k0, kw), :], qt[:, q_lo:])

    def update(s, k0, kw, m_all, q_lo):
        m_prev = m_all[:, q_lo:]
        m_new = jnp.maximum(m_prev, jnp.max(s, axis=0, keepdims=True))
        alpha = jnp.exp2(m_prev - m_new)
        p = jnp.exp2(s - m_new).astype(BF16)
        acc_ref[:, q_lo:] = alpha * acc_ref[:, q_lo:] + _dot(vt_ref[0, :, pl.ds(k0, kw)], p)
        return m_new if q_lo == 0 else jnp.concatenate([m_all[:, :q_lo], m_new], axis=1)

    n_pairs = qi
    s_ref[0] = scores(0, tk_main, 0)

    def pair(t, m):
        k0 = pl.multiple_of(t * (2 * tk_main), 2 * tk_main)
        s_ref[1] = scores(k0 + tk_main, tk_main, 0)
        m = update(s_ref[0], k0, tk_main, m, 0)
        s_ref[0] = scores(k0 + 2 * tk_main, tk_main, 0)
        return update(s_ref[1], k0 + tk_main, tk_main, m, 0)

    m = lax.fori_loop(0, n_pairs, pair, jnp.full((1, tq), -jnp.inf, F32))
    base = pl.multiple_of(qi * tq, tq)
    s_ref[1, :, tk_main:] = scores(base + tk_main, tk_main, tk_main)
    for d in range(tq // tk):
        e, r0, q_lo = (d * tk) // tk_main, (d * tk) % tk_main, d * tk
        s = s_ref[e, r0:r0 + tk, q_lo:]
        row = lax.broadcasted_iota(jnp.int32, s.shape, 0)
        col = lax.broadcasted_iota(jnp.int32, s.shape, 1)
        m = update(jnp.where(row <= col, s, -jnp.inf), base + q_lo, tk, m, q_lo)
    acc = acc_ref[...]
    o_ref[...] = (acc[0:MLA_V, :] / acc[MLA_V:MLA_V + 1, :]).T.astype(o_ref.dtype)


def flash_attention(qt, k, vt, batch, seq, tq, tk, tk_main):
    t = k.shape[0]
    tq = _tile(seq, tq)
    tk = _tile(tq, tk)
    tk_main = _tile(tq, tk_main)
    assert tq == 2 * tk_main and tk_main % tk == 0
    nq = seq // tq
    return pl.pallas_call(
        functools.partial(_flash_kernel, tq=tq, tk=tk, tk_main=tk_main),
        grid=(batch, MLA_HEADS, nq),
        in_specs=[pl.BlockSpec((1, MLA_PAD, tq), lambda b, h, i: (h, 0, b * nq + i)),
                  pl.BlockSpec((seq, MLA_PAD), lambda b, h, i: (b, h)),
                  pl.BlockSpec((1, VT_ROWS, seq), lambda b, h, i: (h, 0, b))],
        out_specs=pl.BlockSpec((tq, MLA_V), lambda b, h, i: (b * nq + i, h)),
        out_shape=jax.ShapeDtypeStruct((t, MLA_HEADS * MLA_V), BF16),
        scratch_shapes=[pltpu.VMEM((VT_ROWS, tq), F32), pltpu.VMEM((2, tk_main, tq), F32)],
        compiler_params=_params("parallel", "parallel", "arbitrary"),
        name="mla_flash",
    )(qt, k, vt)


def _ret_kernel(q_ref, k_ref, v_ref, g_ref, o_ref, r_ref):
    c = RET_CHUNK
    hf = pl.program_id(1).astype(F32)

    @pl.when(pl.program_id(2) == 0)
    def _():
        r_ref[...] = jnp.zeros(r_ref.shape, F32)

    def log_gamma(shape):
        return jnp.log(1.0 - jnp.exp2(-5.0 - jnp.full(shape, hf, F32)))

    row = lax.broadcasted_iota(jnp.int32, (c, c), 0)
    col = lax.broadcasted_iota(jnp.int32, (c, c), 1)
    diff = (row - col).astype(F32)
    d_intra = jnp.where(diff >= 0, jnp.exp(jnp.where(diff >= 0, diff, 0.0) * log_gamma((c, c))), 0.0)
    idx = lax.broadcasted_iota(jnp.int32, (c, RET_DK), 0).astype(F32)
    lg_k = log_gamma((c, RET_DK))
    q_dec = jnp.exp((idx + 1.0) * lg_k).astype(BF16)
    k_dec = jnp.exp((c - 1.0 - idx) * lg_k).astype(BF16)
    c_dec = jnp.exp(c * log_gamma((1, RET_DV)))

    def chunk(ci, carry):
        rows = pl.ds(pl.multiple_of(ci * c, c), c)
        q, k, v = q_ref[rows, :], k_ref[rows, :], v_ref[rows, :]
        a = lax.dot_general(q, k, NT, preferred_element_type=F32) * d_intra
        r = r_ref[...]
        out = _dot(a.astype(BF16), v) + _dot(q * q_dec, r.astype(BF16))
        r_ref[...] = r * c_dec + lax.dot_general(k * k_dec, v, TN, preferred_element_type=F32)
        g = g_ref[rows, :].astype(F32)
        o_ref[rows, :] = (_rms(out) * (g * jax.nn.sigmoid(g))).astype(o_ref.dtype)
        return carry

    lax.fori_loop(0, q_ref.shape[0] // c, chunk, 0, unroll=2)


def retention(z, batch, seq, cb):
    t = z.shape[0]
    cb = _tile(seq, cb)
    assert cb % RET_CHUNK == 0
    nblk = seq // cb
    nqk = RET_HEADS * RET_DK // RET_DK
    nv = (2 * RET_HEADS * RET_DK) // RET_DV
    return pl.pallas_call(
        _ret_kernel,
        grid=(batch, RET_HEADS, nblk),
        in_specs=[pl.BlockSpec((cb, RET_DK), lambda b, h, s: (b * nblk + s, h)),
                  pl.BlockSpec((cb, RET_DK), lambda b, h, s: (b * nblk + s, nqk + h)),
                  pl.BlockSpec((cb, RET_DV), lambda b, h, s: (b * nblk + s, nv + h)),
                  pl.BlockSpec((cb, RET_DV), lambda b, h, s: (b * nblk + s, nv + RET_HEADS + h))],
        out_specs=pl.BlockSpec((cb, RET_DV), lambda b, h, s: (b * nblk + s, h)),
        out_shape=jax.ShapeDtypeStruct((t, RET_HEADS * RET_DV), BF16),
        scratch_shapes=[pltpu.VMEM((RET_DK, RET_DV), F32)],
        compiler_params=_params("parallel", "parallel", "arbitrary"),
        name="retention",
    )(z, z, z, z)


def _rope_pad(a, axis):
    x1, x2 = jnp.split(a, 2, axis=axis)
    z = jnp.zeros_like(x1)
    return jnp.concatenate([x1, z, x2, z], axis=axis)


def _pad_qk_gain(g):
    return jnp.concatenate([g[:MLA_NOPE], _rope_pad(g[MLA_NOPE:], 0)])[None, :]


def kernel(x, p, positions, norm_mix, norm_ffn, norm_ple, e_w_in, e_lb_logits, e_q_a_norm, e_kv_a_norm, e_w_uq, e_w_ukv, e_q_norm, e_k_norm, e_hg_onorm, e_w_out, o_w_in, o_w_out, ffn_w_gate, ffn_w_up, ffn_conv_w, ffn_conv_b, ffn_w_down, ple_w_proj, ple_w_gate):
    batch, seq, d = x.shape
    assert norm_mix.shape[0] == 2 and e_w_in.shape[0] == 1 and o_w_in.shape[0] == 1
    t = batch * seq
    h = x.reshape(t, d)
    pos_col = positions.reshape(t, 1)

    f_ret = (ROPE_BASE ** (-jnp.arange(RET_DK // 2, dtype=F32) / (RET_DK // 2)))[None, :]
    n_mla, stride = MLA_ROPE // 2, (RET_DK // 2) // (MLA_ROPE // 2)
    src, x1, x2 = jnp.arange(n_mla) * stride, jnp.arange(n_mla), LANES // 2 + jnp.arange(n_mla)
    sel = jnp.zeros((RET_DK // 2, LANES), F32)
    sel_cos = sel.at[src, x1].set(1.0).at[src, x2].set(1.0)
    sel_sin = sel.at[src, x1].set(-1.0).at[src, x2].set(1.0)
    cos_ret, sin_ret, cos_mla, sin_mla = rope_tables(pos_col, f_ret, sel_cos.astype(BF16),
                                                     sel_sin.astype(BF16))

    w_in = e_w_in[0]
    w_hg = cast_bf16(e_w_in, 0, 4 * HG_W, "cast_w_hgrn")
    off = 4 * HG_W + MLA_Q_RANK + MLA_KV_RANK
    w_mla = jnp.concatenate([w_in[:, 4 * HG_W:off], _rope_pad(w_in[:, off:], 1)], axis=1).astype(BF16)
    wuq = e_w_uq[0].reshape(MLA_Q_RANK, MLA_HEADS, MLA_QK)
    wuq = jnp.concatenate([wuq[..., :MLA_NOPE], _rope_pad(wuq[..., MLA_NOPE:], 2)], axis=-1)
    wuq = wuq.reshape(MLA_Q_RANK, MLA_HEADS * MLA_PAD).astype(BF16)
    wukv = e_w_ukv[0].astype(BF16)
    g_mix0 = norm_mix[0][None, :]

    z_hg, u_mix0 = norm_matmul(h, g_mix0, w_hg, F32, ROWS, COLS, "in_proj_hgrn")
    z_mla = matmul(u_mix0, w_mla, ROWS, "in_proj_mla")
    o_a = hgrn(z_hg, e_lb_logits, e_hg_onorm[0][None, :], batch, seq, HGRN_ROWS)
    qt, k, vt = mla_prep(z_mla, e_q_a_norm[0][None, :], e_kv_a_norm[0][None, :], wuq, wukv,
                         _pad_qk_gain(e_q_norm[0]), _pad_qk_gain(e_k_norm[0]), cos_mla, sin_mla, PREP_ROWS)
    o_b = flash_attention(qt, k, vt, batch, seq, FLASH_Q, FLASH_KV_DIAG, FLASH_KV)
    h = res_matmul(h, [o_a, o_b], cast_bf16(e_w_out, 0, name="cast_w_even_out"), ROWS, COLS,
                   "out_proj_even")

    def channel_mix(h, i):
        h = conv_ffn(h, norm_ffn[i][None, :], cast_bf16(ffn_w_gate, i, name="cast_w_gate"),
                     cast_bf16(ffn_w_up, i, name="cast_w_up"), ffn_conv_w[i], ffn_conv_b[i][None, :],
                     cast_bf16(ffn_w_down, i, name="cast_w_down"), seq, ROWS, FFN_COLS)
        return ple(h, norm_ple[i][None, :], cast_bf16(ple_w_gate, i, name="cast_w_ple"),
                   p.reshape(p.shape[0], t, PLE_DIM), i, ple_w_proj[i].astype(BF16), ROWS, COLS)

    h = channel_mix(h, 0)

    z_ret = ret_in_proj(h, norm_mix[1][None, :], cast_bf16(o_w_in, 0, name="cast_w_ret_in"),
                        cos_ret, sin_ret, ROWS, COLS)
    o_r = retention(z_ret, batch, seq, RET_ROWS)
    h = res_matmul(h, [o_r], cast_bf16(o_w_out, 0, name="cast_w_ret_out"), ROWS, COLS // 2, "out_proj_odd")
    h = channel_mix(h, 1)
    return h.reshape(batch, seq, d)
```

```python
import functools

import jax
import jax.numpy as jnp
from jax import lax
from jax.experimental import pallas as pl
from jax.experimental.pallas import tpu as pltpu

F32 = jnp.float32
BF16 = jnp.bfloat16

LANES = 128
SUBLANES = 8
BF16_SUBLANES = 16
MXU_DIM = 256
VMEM_BYTES = 64 * 2**20

PLE_DIM = 256
HG_HEADS = 8
HG_DK = 128
HG_DV = 128
HG_W = HG_HEADS * HG_DK
HG_CHUNK = 64
MLA_HEADS = 8
MLA_Q_RANK = 512
MLA_KV_RANK = 512
MLA_NOPE = 128
MLA_ROPE = 64
MLA_V = 128
MLA_QK = MLA_NOPE + MLA_ROPE
MLA_PAD = MLA_NOPE + LANES
VT_ROWS = MLA_V + BF16_SUBLANES
LOG2E = 1.4426950408889634
RET_HEADS = 8
RET_DK = 256
RET_DV = 512
RET_CHUNK = 256
CONV_W = 3
FFN_STRIP = MXU_DIM
ROPE_BASE = 10000.0
EPS = 1e-6

VMEM_LIMIT = VMEM_BYTES - 4 * 2**20
CAST_BLOCK_BYTES = 8 * 2**20

ROWS = 1024
NORM_ROWS = 256
COLS = 2048
FFN_COLS = 512
PREP_ROWS = 512
HGRN_ROWS = 512
RET_ROWS = 2048
FLASH_Q = 2048
FLASH_KV = 1024
FLASH_KV_DIAG = 512

NT = (((1,), (1,)), ((), ()))
TN = (((0,), (0,)), ((), ()))


def _params(*sem):
    return pltpu.CompilerParams(dimension_semantics=sem, vmem_limit_bytes=VMEM_LIMIT)


def _tile(n, pref):
    t = min(n, pref)
    assert n % t == 0, (n, pref)
    return t


def _dot(a, b):
    return jnp.dot(a, b, preferred_element_type=F32)


def _rms(x, g=None):
    y = x * lax.rsqrt(jnp.sum(x * x, axis=-1, keepdims=True) * (1.0 / x.shape[-1]) + EPS)
    return y if g is None else y * g


def _norm_into(x_ref, g_ref, dst_ref, rows=NORM_ROWS):
    g = g_ref[...]

    def body(c, carry):
        r0 = pl.multiple_of(c * rows, rows)
        dst_ref[pl.ds(r0, rows), :] = _rms(x_ref[pl.ds(r0, rows), :], g).astype(dst_ref.dtype)
        return carry

    lax.fori_loop(0, x_ref.shape[0] // rows, body, 0)


def _cast_kernel(w_ref, o_ref):
    o_ref[...] = w_ref[...].astype(o_ref.dtype)


def cast_bf16(w, layer, ncols=None, name="cast_bf16"):
    _, r, c = w.shape
    ncols = c if ncols is None else ncols
    tr = r
    while tr % 16 == 0 and tr * ncols * 4 > CAST_BLOCK_BYTES:
        tr //= 2
    return pl.pallas_call(
        _cast_kernel,
        grid=(r // tr,),
        in_specs=[pl.BlockSpec((None, tr, ncols), lambda i: (layer, i, 0))],
        out_specs=pl.BlockSpec((tr, ncols), lambda i: (i, 0)),
        out_shape=jax.ShapeDtypeStruct((r, ncols), BF16),
        compiler_params=_params("parallel"),
        name=name,
    )(w)


def _select_lanes(x, sel):
    hi = x.astype(BF16)
    r1 = x - hi.astype(F32)
    mid = r1.astype(BF16)
    lo = (r1 - mid.astype(F32)).astype(BF16)
    return _dot(hi, sel) + _dot(mid, sel) + _dot(lo, sel)


def _rope_tab_kernel(pos_ref, f_ref, selc_ref, sels_ref, c_ref, s_ref, cm_ref, sm_ref):
    ang = pos_ref[...].astype(F32) * f_ref[...]
    c, s = jnp.cos(ang), jnp.sin(ang)
    c_ref[...] = c
    s_ref[...] = s
    cm_ref[...] = _select_lanes(c, selc_ref[...])
    sm_ref[...] = _select_lanes(s, sels_ref[...])


def rope_tables(pos_col, freq, sel_cos, sel_sin):
    t = pos_col.shape[0]
    tm = _tile(t, ROWS)
    const = lambda a: pl.BlockSpec(a.shape, lambda i: (0, 0))
    return pl.pallas_call(
        _rope_tab_kernel,
        grid=(t // tm,),
        in_specs=[pl.BlockSpec((tm, 1), lambda i: (i, 0)), const(freq), const(sel_cos), const(sel_sin)],
        out_specs=[pl.BlockSpec((tm, LANES), lambda i: (i, 0))] * 4,
        out_shape=[jax.ShapeDtypeStruct((t, LANES), F32)] * 4,
        compiler_params=_params("parallel"),
        name="rope_tables",
    )(pos_col, freq, sel_cos, sel_sin)


def _norm_matmul_kernel(x_ref, g_ref, w_ref, o_ref, xn_ref):
    j = pl.program_id(1)
    tn = o_ref.shape[1]

    @pl.when(j == 0)
    def _():
        _norm_into(x_ref, g_ref, xn_ref)

    w = w_ref[:, pl.ds(pl.multiple_of(j * tn, tn), tn)]
    o_ref[...] = _dot(xn_ref[...], w).astype(o_ref.dtype)


def norm_matmul(x, g, w, out_dtype, tm, tn, name):
    t, k = x.shape
    n = w.shape[1]
    tm, tn = _tile(t, tm), _tile(n, tn)
    return pl.pallas_call(
        _norm_matmul_kernel,
        grid=(t // tm, n // tn),
        in_specs=[pl.BlockSpec((tm, k), lambda i, j: (i, 0)),
                  pl.BlockSpec((1, k), lambda i, j: (0, 0)),
                  pl.BlockSpec((k, n), lambda i, j: (0, 0), pipeline_mode=pl.Buffered(1))],
        out_specs=pl.BlockSpec((tm, tn), lambda i, j: (i, j)),
        out_shape=jax.ShapeDtypeStruct((t, n), out_dtype),
        scratch_shapes=[pltpu.VMEM((tm, k), BF16)],
        compiler_params=_params("parallel", "arbitrary"),
        name=name,
    )(x, g, w)


def _ret_in_kernel(x_ref, g_ref, w_ref, cos_ref, sin_ref, o_ref, xn_ref, *, n_q_tiles, n_rope_tiles):
    j = pl.program_id(1)

    @pl.when(j == 0)
    def _():
        _norm_into(x_ref, g_ref, xn_ref)

    acc = _dot(xn_ref[...], w_ref[...])
    is_rope = j < n_rope_tiles
    scale = jnp.where(j >= n_q_tiles, RET_DK ** -0.5, 1.0).astype(F32)
    cos = cos_ref[...] * scale
    sin = sin_ref[...] * scale
    half = RET_DK // 2
    for hh in range(acc.shape[1] // RET_DK):
        x1 = acc[:, hh * RET_DK:hh * RET_DK + half]
        x2 = acc[:, hh * RET_DK + half:(hh + 1) * RET_DK]
        o_ref[:, hh * RET_DK:hh * RET_DK + half] = jnp.where(
            is_rope, x1 * cos - x2 * sin, x1).astype(o_ref.dtype)
        o_ref[:, hh * RET_DK + half:(hh + 1) * RET_DK] = jnp.where(
            is_rope, x2 * cos + x1 * sin, x2).astype(o_ref.dtype)


def ret_in_proj(x, g, w, cos, sin, tm, tn):
    t, k = x.shape
    n = w.shape[1]
    tm, tn = _tile(t, tm), _tile(n, tn)
    qw = RET_HEADS * RET_DK
    kern = functools.partial(_ret_in_kernel, n_q_tiles=qw // tn, n_rope_tiles=2 * qw // tn)
    return pl.pallas_call(
        kern,
        grid=(t // tm, n // tn),
        in_specs=[pl.BlockSpec((tm, k), lambda i, j: (i, 0)),
                  pl.BlockSpec((1, k), lambda i, j: (0, 0)),
                  pl.BlockSpec((k, tn), lambda i, j: (0, j)),
                  pl.BlockSpec((tm, RET_DK // 2), lambda i, j: (i, 0)),
                  pl.BlockSpec((tm, RET_DK // 2), lambda i, j: (i, 0))],
        out_specs=pl.BlockSpec((tm, tn), lambda i, j: (i, j)),
        out_shape=jax.ShapeDtypeStruct((t, n), BF16),
        scratch_shapes=[pltpu.VMEM((tm, k), BF16)],
        compiler_params=_params("parallel", "arbitrary"),
        name="ret_in_proj",
    )(x, g, w, cos, sin)


def _res_matmul_kernel(*refs):
    h_ref, w_ref, o_ref = refs[0], refs[-2], refs[-1]
    lhs = [a_ref[...] for a_ref in refs[1:-2]]
    lhs = lhs[0] if len(lhs) == 1 else jnp.concatenate(lhs, axis=1)
    o_ref[...] = h_ref[...] + _dot(lhs, w_ref[...])


def res_matmul(h, lhs, w, tm, tn, name):
    t, n = h.shape
    tm, tn = _tile(t, tm), _tile(n, tn)
    assert sum(a.shape[1] for a in lhs) == w.shape[0]
    in_specs = [pl.BlockSpec((tm, tn), lambda i, j: (i, j))]
    in_specs += [pl.BlockSpec((tm, a.shape[1]), lambda i, j: (i, 0)) for a in lhs]
    in_specs += [pl.BlockSpec((w.shape[0], tn), lambda i, j: (0, j),
                              pipeline_mode=pl.Buffered(1) if tn == n else None)]
    return pl.pallas_call(
        _res_matmul_kernel,
        grid=(t // tm, n // tn),
        in_specs=in_specs,
        out_specs=pl.BlockSpec((tm, tn), lambda i, j: (i, j)),
        out_shape=jax.ShapeDtypeStruct((t, n), F32),
        compiler_params=_params("parallel", "arbitrary"),
        name=name,
    )(h, *lhs, w)


def _ffn_kernel(h_ref, g_ref, wg_ref, wu_ref, cw_ref, cb_ref, wd_ref, o_ref,
                u_ref, gs_ref, tail_ref, *, blocks_per_seq):
    i, j = pl.program_id(0), pl.program_id(1)
    tm = h_ref.shape[0]

    @pl.when(j == 0)
    def _():
        _norm_into(h_ref, g_ref, u_ref)
        o_ref[...] = h_ref[...]

    @pl.when(jnp.logical_and(i == 0, j == 0))
    def _():
        tail_ref[...] = jnp.zeros(tail_ref.shape, F32)

    seq_start = (i % blocks_per_seq) == 0
    gs_ref[0:SUBLANES, :] = jnp.where(seq_start, 0.0, tail_ref[j])
    cw, cb = cw_ref[...], cb_ref[...]
    hm = tm // 2
    for r in range(0, tm, hm):
        u = u_ref[r:r + hm, :]
        a = _dot(u, wg_ref[...])
        up = _dot(u, wu_ref[...])
        gs_ref[SUBLANES + r:SUBLANES + r + hm, :] = a
        c = cb + cw[CONV_W - 1:CONV_W, :] * a
        for tap in range(CONV_W - 1):
            r0 = SUBLANES + r - (CONV_W - 1 - tap)
            c = c + cw[tap:tap + 1, :] * gs_ref[r0:r0 + hm, :]
        act = (c * jax.nn.sigmoid(c) * up).astype(BF16)
        o_ref[r:r + hm, :] += _dot(act, wd_ref[...])
    tail_ref[j] = gs_ref[tm:tm + SUBLANES, :]


def conv_ffn(h, g, wg, wu, cw, cb, wd, seq, tm, tf):
    t, d = h.shape
    f = wg.shape[1]
    tm, tf = _tile(seq, tm), _tile(f, tf)
    kern = functools.partial(_ffn_kernel, blocks_per_seq=seq // tm)
    return pl.pallas_call(
        kern,
        grid=(t // tm, f // tf),
        in_specs=[pl.BlockSpec((tm, d), lambda i, j: (i, 0)),
                  pl.BlockSpec((1, d), lambda i, j: (0, 0)),
                  pl.BlockSpec((d, tf), lambda i, j: (0, j)),
                  pl.BlockSpec((d, tf), lambda i, j: (0, j)),
                  pl.BlockSpec((CONV_W, tf), lambda i, j: (0, j)),
                  pl.BlockSpec((1, tf), lambda i, j: (0, j)),
                  pl.BlockSpec((tf, d), lambda i, j: (j, 0))],
        out_specs=pl.BlockSpec((tm, d), lambda i, j: (i, 0)),
        out_shape=jax.ShapeDtypeStruct((t, d), F32),
        scratch_shapes=[pltpu.VMEM((tm, d), BF16),
                        pltpu.VMEM((tm + SUBLANES, tf), F32),
                        pltpu.VMEM((f // tf, SUBLANES, tf), F32)],
        compiler_params=_params("arbitrary", "arbitrary"),
        name="conv_ffn",
    )(h, g, wg, wu, cw, cb, wd)


def _ple_kernel(h_ref, g_ref, wgate_ref, p_ref, wproj_ref, o_ref, xn_ref, pb_ref):
    j = pl.program_id(1)
    tn = o_ref.shape[1]

    @pl.when(j == 0)
    def _():
        _norm_into(h_ref, g_ref, xn_ref)
        pb_ref[...] = p_ref[...].astype(BF16)

    gate = jax.nn.sigmoid(_dot(xn_ref[...], wgate_ref[...]))
    proj = _dot(pb_ref[...], wproj_ref[...])
    c0 = pl.multiple_of(j * tn, tn)
    o_ref[...] = h_ref[:, pl.ds(c0, tn)] + proj * gate


def ple(h, g, wgate, p, layer, wproj, tm, tn):
    t, d = h.shape
    tm, tn = _tile(t, tm), _tile(d, tn)
    return pl.pallas_call(
        _ple_kernel,
        grid=(t // tm, d // tn),
        in_specs=[pl.BlockSpec((tm, d), lambda i, j: (i, 0)),
                  pl.BlockSpec((1, d), lambda i, j: (0, 0)),
                  pl.BlockSpec((d, tn), lambda i, j: (0, j)),
                  pl.BlockSpec((None, tm, PLE_DIM), lambda i, j: (layer, i, 0)),
                  pl.BlockSpec((PLE_DIM, tn), lambda i, j: (0, j))],
        out_specs=pl.BlockSpec((tm, tn), lambda i, j: (i, j)),
        out_shape=jax.ShapeDtypeStruct((t, d), F32),
        scratch_shapes=[pltpu.VMEM((tm, d), BF16), pltpu.VMEM((tm, PLE_DIM), BF16)],
        compiler_params=_params("parallel", "arbitrary"),
        name="ple",
    )(h, g, wgate, p, wproj)


def _hgrn_kernel(q_ref, f_ref, i_ref, g_ref, lbl_ref, on_ref, o_ref, st_ref):
    c = HG_CHUNK

    @pl.when(pl.program_id(1) == 0)
    def _():
        st_ref[...] = jnp.zeros(st_ref.shape, F32)

    lg = lbl_ref[...]
    e = jnp.exp(lg - jnp.max(lg, axis=0, keepdims=True))
    lb = e[0:1, :] / jnp.sum(e, axis=0, keepdims=True)
    onorm = on_ref[...]
    pw = 2 * HG_DK
    row = lax.broadcasted_iota(jnp.int32, (c, c), 0)
    col = lax.broadcasted_iota(jnp.int32, (c, c), 1)
    tril = jnp.where(row >= col, 1.0, 0.0).astype(BF16)
    row2 = lax.broadcasted_iota(jnp.int32, (c, 2 * c), 0)
    col2 = lax.broadcasted_iota(jnp.int32, (c, 2 * c), 1)
    causal2 = row2 >= jnp.where(col2 >= c, col2 - c, col2)
    left = lax.broadcasted_iota(jnp.int32, (c, pw), 1) < HG_DK
    same_head = ((lax.broadcasted_iota(jnp.int32, (pw, pw), 0) < HG_DV)
                 == (lax.broadcasted_iota(jnp.int32, (pw, pw), 1) < HG_DK))

    def block_diag(x):
        zero = jnp.zeros_like(x)
        return jnp.concatenate([jnp.where(left, x, zero), jnp.where(left, zero, x)], axis=0)

    def chunk(ci, carry):
        rows = pl.ds(pl.multiple_of(ci * c, c), c)
        q = q_ref[rows, :]
        v = i_ref[rows, :].astype(BF16)
        f = lb + (1.0 - lb) * jax.nn.sigmoid(f_ref[rows, :])
        k = 1.0 - f
        lf = jnp.log(f)
        hi = lf.astype(BF16)
        r1 = lf - hi.astype(F32)
        mid = r1.astype(BF16)
        lo = (r1 - mid.astype(F32)).astype(BF16)
        bc = _dot(tril, hi) + _dot(tril, mid) + _dot(tril, lo)
        ref = bc[c // 2 - 1:c // 2, :]
        b_last = bc[c - 1:c, :]
        q_rel = (q * jnp.exp(bc - ref)).astype(BF16)
        k_rel = (k * jnp.exp(ref - bc)).astype(BF16)
        q_dec = (q * jnp.exp(bc)).astype(BF16)
        k_dec = (k * jnp.exp(b_last - bc)).astype(BF16)
        decay = jnp.exp(b_last)
        for p in range(HG_HEADS // 2):
            cols = slice(p * pw, (p + 1) * pw)
            a = lax.dot_general(q_rel[:, cols], block_diag(k_rel[:, cols]), NT, preferred_element_type=F32)
            a = jnp.where(causal2, a, 0.0).astype(BF16)
            st = st_ref[p]
            out = _dot(a, block_diag(v[:, cols])) + lax.dot_general(q_dec[:, cols], st.astype(BF16), NT,
                                                                    preferred_element_type=F32)
            upd = lax.dot_general(v[:, cols], k_dec[:, cols], TN, preferred_element_type=F32)
            st_ref[p] = st * decay[:, cols] + jnp.where(same_head, upd, 0.0)
            for hh in range(2):
                hc = slice(p * pw + hh * HG_DV, p * pw + (hh + 1) * HG_DV)
                g = g_ref[rows, hc]
                o_ref[rows, hc] = (_rms(out[:, hh * HG_DV:(hh + 1) * HG_DV], onorm)
                                   * (g * jax.nn.sigmoid(g))).astype(o_ref.dtype)
        return carry

    lax.fori_loop(0, q_ref.shape[0] // c, chunk, 0, unroll=4)


def hgrn(z, lb_logits, onorm, batch, seq, cb):
    t = z.shape[0]
    cb = _tile(seq, cb)
    nblk = seq // cb
    spec = lambda part: pl.BlockSpec((cb, HG_W), lambda b, s, part=part: (b * nblk + s, part))
    return pl.pallas_call(
        _hgrn_kernel,
        grid=(batch, nblk),
        in_specs=[spec(0), spec(1), spec(2), spec(3),
                  pl.BlockSpec(lb_logits.shape, lambda b, s: (0, 0)),
                  pl.BlockSpec((1, HG_DV), lambda b, s: (0, 0))],
        out_specs=pl.BlockSpec((cb, HG_W), lambda b, s: (b * nblk + s, 0)),
        out_shape=jax.ShapeDtypeStruct((t, HG_W), BF16),
        scratch_shapes=[pltpu.VMEM((HG_HEADS // 2, 2 * HG_DV, 2 * HG_DK), F32)],
        compiler_params=_params("parallel", "arbitrary"),
        name="hgrn2",
    )(z, z, z, z, lb_logits, onorm)


def _mla_prep_kernel(z_ref, qa_ref, kva_ref, wuq_ref, wukv_ref, qn_ref, kn_ref, cos_ref, sin_ref,
                     qt_out, k_out, vt_out):
    cq = _rms(z_ref[:, 0:MLA_Q_RANK], qa_ref[...]).astype(BF16)
    ckv = _rms(z_ref[:, MLA_Q_RANK:MLA_Q_RANK + MLA_KV_RANK], kva_ref[...]).astype(BF16)
    kpe = z_ref[:, MLA_Q_RANK + MLA_KV_RANK:]
    qf = _dot(cq, wuq_ref[...])
    kvf = _dot(ckv, wukv_ref[...])
    cos, sin = cos_ref[...], sin_ref[...]
    qn, kn = qn_ref[...], kn_ref[...]
    scale = MLA_QK ** -0.5 * LOG2E
    tm = z_ref.shape[0]

    def rope(x):
        return x * cos + pltpu.roll(x, LANES // 2, 1) * sin

    kpe_ss = jnp.sum(kpe * kpe, axis=-1, keepdims=True)
    k_rope = rope(kpe * kn[:, MLA_NOPE:])
    ones_rows = jnp.where(lax.broadcasted_iota(jnp.int32, (VT_ROWS - MLA_V, tm), 0) == 0, 1.0, 0.0)
    for h in range(MLA_HEADS):
        c0 = h * MLA_PAD
        qh = qf[:, c0:c0 + MLA_PAD]
        rq = lax.rsqrt(jnp.sum(qh * qh, axis=-1, keepdims=True) * (1.0 / MLA_QK) + EPS) * scale
        qh = qh * rq * qn
        qt_out[h, 0:MLA_NOPE, :] = qh[:, :MLA_NOPE].T.astype(BF16)
        qt_out[h, MLA_NOPE:MLA_PAD, :] = rope(qh[:, MLA_NOPE:]).T.astype(BF16)
        kh = kvf[:, c0:c0 + MLA_NOPE]
        rk = lax.rsqrt((jnp.sum(kh * kh, axis=-1, keepdims=True) + kpe_ss) * (1.0 / MLA_QK) + EPS)
        k_out[:, c0:c0 + MLA_NOPE] = (kh * rk * kn[:, :MLA_NOPE]).astype(BF16)
        k_out[:, c0 + MLA_NOPE:c0 + MLA_PAD] = (k_rope * rk).astype(BF16)
        vt_out[h, 0:MLA_V, :] = kvf[:, c0 + MLA_NOPE:c0 + MLA_PAD].T.astype(BF16)
        vt_out[h, MLA_V:VT_ROWS, :] = ones_rows.astype(BF16)


def mla_prep(z, qa, kva, wuq, wukv, qn, kn, cos, sin, tm):
    t, zw = z.shape
    tm = _tile(t, tm)
    full = lambda a: pl.BlockSpec(a.shape, lambda i: (0, 0))
    rows = lambda w: pl.BlockSpec((tm, w), lambda i: (i, 0))
    cols = lambda r: pl.BlockSpec((MLA_HEADS, r, tm), lambda i: (0, 0, i))
    return pl.pallas_call(
        _mla_prep_kernel,
        grid=(t // tm,),
        in_specs=[rows(zw), full(qa), full(kva), full(wuq), full(wukv), full(qn), full(kn),
                  rows(LANES), rows(LANES)],
        out_specs=[cols(MLA_PAD), rows(MLA_HEADS * MLA_PAD), cols(VT_ROWS)],
        out_shape=[jax.ShapeDtypeStruct((MLA_HEADS, MLA_PAD, t), BF16),
                   jax.ShapeDtypeStruct((t, MLA_HEADS * MLA_PAD), BF16),
                   jax.ShapeDtypeStruct((MLA_HEADS, VT_ROWS, t), BF16)],
        compiler_params=_params("parallel"),
        name="mla_prep",
    )(z, qa, kva, wuq, wukv, qn, kn, cos, sin)


def _flash_kernel(qt_ref, k_ref, vt_ref, o_ref, acc_ref, s_ref, *, tq, tk, tk_main):
    qi = pl.program_id(2)
    qt = qt_ref[0]
    acc_ref[...] = jnp.zeros(acc_ref.shape, F32)

    def scores(k0, kw, q_lo):
        return _dot(k_ref[pl.ds(k0, kw), :], qt[:, q_lo:])

    def update(s, k0, kw, m_all, q_lo):
        m_prev = m_all[:, q_lo:]
        m_new = jnp.maximum(m_prev, jnp.max(s, axis=0, keepdims=True))
        alpha = jnp.exp2(m_prev - m_new)
        p = jnp.exp2(s - m_new).astype(BF16)
        acc_ref[:, q_lo:] = alpha * acc_ref[:, q_lo:] + _dot(vt_ref[0, :, pl.ds(k0, kw)], p)
        return m_new if q_lo == 0 else jnp.concatenate([m_all[:, :q_lo], m_new], axis=1)

    n_pairs = qi
    s_ref[0] = scores(0, tk_main, 0)

    def pair(t, m):
        k0 = pl.multiple_of(t * (2 * tk_main), 2 * tk_main)
        s_ref[1] = scores(k0 + tk_main, tk_main, 0)
        m = update(s_ref[0], k0, tk_main, m, 0)
        s_ref[0] = scores(k0 + 2 * tk_main, tk_main, 0)
        return update(s_ref[1], k0 + tk_main, tk_main, m, 0)

    m = lax.fori_loop(0, n_pairs, pair, jnp.full((1, tq), -jnp.inf, F32))
    base = pl.multiple_of(qi * tq, tq)
    s_ref[1, :, tk_main:] = scores(base + tk_main, tk_main, tk_main)
    for d in range(tq // tk):
        e, r0, q_lo = (d * tk) // tk_main, (d * tk) % tk_main, d * tk
        s = s_ref[e, r0:r0 + tk, q_lo:]
        row = lax.broadcasted_iota(jnp.int32, s.shape, 0)
        col = lax.broadcasted_iota(jnp.int32, s.shape, 1)
        m = update(jnp.where(row <= col, s, -jnp.inf), base + q_lo, tk, m, q_lo)
    acc = acc_ref[...]
    o_ref[...] = (acc[0:MLA_V, :] / acc[MLA_V:MLA_V + 1, :]).T.astype(o_ref.dtype)


def flash_attention(qt, k, vt, batch, seq, tq, tk, tk_main):
    t = k.shape[0]
    tq = _tile(seq, tq)
    tk = _tile(tq, tk)
    tk_main = _tile(tq, tk_main)
    assert tq == 2 * tk_main and tk_main % tk == 0
    nq = seq // tq
    return pl.pallas_call(
        functools.partial(_flash_kernel, tq=tq, tk=tk, tk_main=tk_main),
        grid=(batch, MLA_HEADS, nq),
        in_specs=[pl.BlockSpec((1, MLA_PAD, tq), lambda b, h, i: (h, 0, b * nq + i)),
                  pl.BlockSpec((seq, MLA_PAD), lambda b, h, i: (b, h)),
                  pl.BlockSpec((1, VT_ROWS, seq), lambda b, h, i: (h, 0, b))],
        out_specs=pl.BlockSpec((tq, MLA_V), lambda b, h, i: (b * nq + i, h)),
        out_shape=jax.ShapeDtypeStruct((t, MLA_HEADS * MLA_V), BF16),
        scratch_shapes=[pltpu.VMEM((VT_ROWS, tq), F32), pltpu.VMEM((2, tk_main, tq), F32)],
        compiler_params=_params("parallel", "parallel", "arbitrary"),
        name="mla_flash",
    )(qt, k, vt)


def _ret_kernel(q_ref, k_ref, v_ref, g_ref, o_ref, r_ref):
    c = RET_CHUNK
    hf = pl.program_id(1).astype(F32)

    @pl.when(pl.program_id(2) == 0)
    def _():
        r_ref[...] = jnp.zeros(r_ref.shape, F32)

    def log_gamma(shape):
        return jnp.log(1.0 - jnp.exp2(-5.0 - jnp.full(shape, hf, F32)))

    row = lax.broadcasted_iota(jnp.int32, (c, c), 0)
    col = lax.broadcasted_iota(jnp.int32, (c, c), 1)
    diff = (row - col).astype(F32)
    d_intra = jnp.where(diff >= 0, jnp.exp(jnp.where(diff >= 0, diff, 0.0) * log_gamma((c, c))), 0.0)
    idx = lax.broadcasted_iota(jnp.int32, (c, RET_DK), 0).astype(F32)
    lg_k = log_gamma((c, RET_DK))
    q_dec = jnp.exp((idx + 1.0) * lg_k).astype(BF16)
    k_dec = jnp.exp((c - 1.0 - idx) * lg_k).astype(BF16)
    c_dec = jnp.exp(c * log_gamma((1, RET_DV)))

    def chunk(ci, carry):
        rows = pl.ds(pl.multiple_of(ci * c, c), c)
        q, k, v = q_ref[rows, :], k_ref[rows, :], v_ref[rows, :]
        a = lax.dot_general(q, k, NT, preferred_element_type=F32) * d_intra
        r = r_ref[...]
        out = _dot(a.astype(BF16), v) + _dot(q * q_dec, r.astype(BF16))
        r_ref[...] = r * c_dec + lax.dot_general(k * k_dec, v, TN, preferred_element_type=F32)
        g = g_ref[rows, :].astype(F32)
        o_ref[rows, :] = (_rms(out) * (g * jax.nn.sigmoid(g))).astype(o_ref.dtype)
        return carry

    lax.fori_loop(0, q_ref.shape[0] // c, chunk, 0, unroll=2)


def retention(z, batch, seq, cb):
    t = z.shape[0]
    cb = _tile(seq, cb)
    assert cb % RET_CHUNK == 0
    nblk = seq // cb
    nqk = RET_HEADS * RET_DK // RET_DK
    nv = (2 * RET_HEADS * RET_DK) // RET_DV
    return pl.pallas_call(
        _ret_kernel,
        grid=(batch, RET_HEADS, nblk),
        in_specs=[pl.BlockSpec((cb, RET_DK), lambda b, h, s: (b * nblk + s, h)),
                  pl.BlockSpec((cb, RET_DK), lambda b, h, s: (b * nblk + s, nqk + h)),
                  pl.BlockSpec((cb, RET_DV), lambda b, h, s: (b * nblk + s, nv + h)),
                  pl.BlockSpec((cb, RET_DV), lambda b, h, s: (b * nblk + s, nv + RET_HEADS + h))],
        out_specs=pl.BlockSpec((cb, RET_DV), lambda b, h, s: (b * nblk + s, h)),
        out_shape=jax.ShapeDtypeStruct((t, RET_HEADS * RET_DV), BF16),
        scratch_shapes=[pltpu.VMEM((RET_DK, RET_DV), F32)],
        compiler_params=_params("parallel", "parallel", "arbitrary"),
        name="retention",
    )(z, z, z, z)


def _rope_pad(a, axis):
    x1, x2 = jnp.split(a, 2, axis=axis)
    z = jnp.zeros_like(x1)
    return jnp.concatenate([x1, z, x2, z], axis=axis)


def _pad_qk_gain(g):
    return jnp.concatenate([g[:MLA_NOPE], _rope_pad(g[MLA_NOPE:], 0)])[None, :]


def kernel(x, p, positions, norm_mix, norm_ffn, norm_ple, e_w_in, e_lb_logits, e_q_a_norm, e_kv_a_norm, e_w_uq, e_w_ukv, e_q_norm, e_k_norm, e_hg_onorm, e_w_out, o_w_in, o_w_out, ffn_w_gate, ffn_w_up, ffn_conv_w, ffn_conv_b, ffn_w_down, ple_w_proj, ple_w_gate):
    batch, seq, d = x.shape
    assert norm_mix.shape[0] == 2 and e_w_in.shape[0] == 1 and o_w_in.shape[0] == 1
    t = batch * seq
    h = x.reshape(t, d)
    pos_col = positions.reshape(t, 1)

    f_ret = (ROPE_BASE ** (-jnp.arange(RET_DK // 2, dtype=F32) / (RET_DK // 2)))[None, :]
    n_mla, stride = MLA_ROPE // 2, (RET_DK // 2) // (MLA_ROPE // 2)
    src, x1, x2 = jnp.arange(n_mla) * stride, jnp.arange(n_mla), LANES // 2 + jnp.arange(n_mla)
    sel = jnp.zeros((RET_DK // 2, LANES), F32)
    sel_cos = sel.at[src, x1].set(1.0).at[src, x2].set(1.0)
    sel_sin = sel.at[src, x1].set(-1.0).at[src, x2].set(1.0)
    cos_ret, sin_ret, cos_mla, sin_mla = rope_tables(pos_col, f_ret, sel_cos.astype(BF16),
                                                     sel_sin.astype(BF16))

    w_in = e_w_in[0]
    w_hg = cast_bf16(e_w_in, 0, 4 * HG_W, "cast_w_hgrn")
    off = 4 * HG_W + MLA_Q_RANK + MLA_KV_RANK
    w_mla = jnp.concatenate([w_in[:, 4 * HG_W:off], _rope_pad(w_in[:, off:], 1)], axis=1).astype(BF16)
    wuq = e_w_uq[0].reshape(MLA_Q_RANK, MLA_HEADS, MLA_QK)
    wuq = jnp.concatenate([wuq[..., :MLA_NOPE], _rope_pad(wuq[..., MLA_NOPE:], 2)], axis=-1)
    wuq = wuq.reshape(MLA_Q_RANK, MLA_HEADS * MLA_PAD).astype(BF16)
    wukv = e_w_ukv[0].astype(BF16)
    g_mix0 = norm_mix[0][None, :]

    z_hg = norm_matmul(h, g_mix0, w_hg, F32, ROWS, COLS, "in_proj_hgrn")
    z_mla = norm_matmul(h, g_mix0, w_mla, F32, ROWS, w_mla.shape[1], "in_proj_mla")
    o_a = hgrn(z_hg, e_lb_logits, e_hg_onorm[0][None, :], batch, seq, HGRN_ROWS)
    qt, k, vt = mla_prep(z_mla, e_q_a_norm[0][None, :], e_kv_a_norm[0][None, :], wuq, wukv,
                         _pad_qk_gain(e_q_norm[0]), _pad_qk_gain(e_k_norm[0]), cos_mla, sin_mla, PREP_ROWS)
    o_b = flash_attention(qt, k, vt, batch, seq, FLASH_Q, FLASH_KV_DIAG, FLASH_KV)
    h = res_matmul(h, [o_a, o_b], cast_bf16(e_w_out, 0, name="cast_w_even_out"), ROWS, COLS,
                   "out_proj_even")

    def channel_mix(h, i):
        h = conv_ffn(h, norm_ffn[i][None, :], cast_bf16(ffn_w_gate, i, name="cast_w_gate"),
                     cast_bf16(ffn_w_up, i, name="cast_w_up"), ffn_conv_w[i], ffn_conv_b[i][None, :],
                     cast_bf16(ffn_w_down, i, name="cast_w_down"), seq, ROWS, FFN_COLS)
        return ple(h, norm_ple[i][None, :], cast_bf16(ple_w_gate, i, name="cast_w_ple"),
                   p.reshape(p.shape[0], t, PLE_DIM), i, ple_w_proj[i].astype(BF16), ROWS, COLS)

    h = channel_mix(h, 0)

    z_ret = ret_in_proj(h, norm_mix[1][None, :], cast_bf16(o_w_in, 0, name="cast_w_ret_in"),
                        cos_ret, sin_ret, ROWS, COLS)
    o_r = retention(z_ret, batch, seq, RET_ROWS)
    h = res_matmul(h, [o_r], cast_bf16(o_w_out, 0, name="cast_w_ret_out"), ROWS, COLS // 2, "out_proj_odd")
    h = channel_mix(h, 1)
    return h.reshape(batch, seq, d)
```

```python
import functools

import jax
import jax.numpy as jnp
from jax import lax
from jax.experimental import pallas as pl
from jax.experimental.pallas import tpu as pltpu

F32 = jnp.float32
BF16 = jnp.bfloat16

LANES = 128
SUBLANES = 8
BF16_SUBLANES = 16
MXU_DIM = 256
VMEM_BYTES = 64 * 2**20

PLE_DIM = 256
HG_HEADS = 8
HG_DK = 128
HG_DV = 128
HG_W = HG_HEADS * HG_DK
HG_CHUNK = 64
MLA_HEADS = 8
MLA_Q_RANK = 512
MLA_KV_RANK = 512
MLA_NOPE = 128
MLA_ROPE = 64
MLA_V = 128
MLA_QK = MLA_NOPE + MLA_ROPE
MLA_PAD = MLA_NOPE + LANES
VT_ROWS = MLA_V + BF16_SUBLANES
LOG2E = 1.4426950408889634
RET_HEADS = 8
RET_DK = 256
RET_DV = 512
RET_CHUNK = 256
CONV_W = 3
FFN_STRIP = MXU_DIM
ROPE_BASE = 10000.0
EPS = 1e-6

VMEM_LIMIT = VMEM_BYTES - 4 * 2**20
CAST_BLOCK_BYTES = 8 * 2**20

ROWS = 1024
NORM_ROWS = 256
COLS = 2048
FFN_COLS = 512
PREP_ROWS = 512
HGRN_ROWS = 512
RET_ROWS = 2048
FLASH_Q = 2048
FLASH_KV = 1024
FLASH_KV_DIAG = 512

NT = (((1,), (1,)), ((), ()))
TN = (((0,), (0,)), ((), ()))


def _params(*sem):
    return pltpu.CompilerParams(dimension_semantics=sem, vmem_limit_bytes=VMEM_LIMIT)


def _tile(n, pref):
    t = min(n, pref)
    assert n % t == 0, (n, pref)
    return t


def _dot(a, b):
    return jnp.dot(a, b, preferred_element_type=F32)


def _rms(x, g=None):
    y = x * lax.rsqrt(jnp.sum(x * x, axis=-1, keepdims=True) * (1.0 / x.shape[-1]) + EPS)
    return y if g is None else y * g


def _norm_into(x_ref, g_ref, dst_ref, rows=NORM_ROWS):
    g = g_ref[...]

    def body(c, carry):
        r0 = pl.multiple_of(c * rows, rows)
        dst_ref[pl.ds(r0, rows), :] = _rms(x_ref[pl.ds(r0, rows), :], g).astype(dst_ref.dtype)
        return carry

    lax.fori_loop(0, x_ref.shape[0] // rows, body, 0)


def _cast_kernel(w_ref, o_ref):
    o_ref[...] = w_ref[...].astype(o_ref.dtype)


def cast_bf16(w, layer, ncols=None, name="cast_bf16"):
    _, r, c = w.shape
    ncols = c if ncols is None else ncols
    tr = r
    while tr % 16 == 0 and tr * ncols * 4 > CAST_BLOCK_BYTES:
        tr //= 2
    return pl.pallas_call(
        _cast_kernel,
        grid=(r // tr,),
        in_specs=[pl.BlockSpec((None, tr, ncols), lambda i: (layer, i, 0))],
        out_specs=pl.BlockSpec((tr, ncols), lambda i: (i, 0)),
        out_shape=jax.ShapeDtypeStruct((r, ncols), BF16),
        compiler_params=_params("parallel"),
        name=name,
    )(w)


def _select_lanes(x, sel):
    hi = x.astype(BF16)
    r1 = x - hi.astype(F32)
    mid = r1.astype(BF16)
    lo = (r1 - mid.astype(F32)).astype(BF16)
    return _dot(hi, sel) + _dot(mid, sel) + _dot(lo, sel)


def _rope_tab_kernel(pos_ref, f_ref, selc_ref, sels_ref, c_ref, s_ref, cm_ref, sm_ref):
    ang = pos_ref[...].astype(F32) * f_ref[...]
    c, s = jnp.cos(ang), jnp.sin(ang)
    c_ref[...] = c
    s_ref[...] = s
    cm_ref[...] = _select_lanes(c, selc_ref[...])
    sm_ref[...] = _select_lanes(s, sels_ref[...])


def rope_tables(pos_col, freq, sel_cos, sel_sin):
    t = pos_col.shape[0]
    tm = _tile(t, ROWS)
    const = lambda a: pl.BlockSpec(a.shape, lambda i: (0, 0))
    return pl.pallas_call(
        _rope_tab_kernel,
        grid=(t // tm,),
        in_specs=[pl.BlockSpec((tm, 1), lambda i: (i, 0)), const(freq), const(sel_cos), const(sel_sin)],
        out_specs=[pl.BlockSpec((tm, LANES), lambda i: (i, 0))] * 4,
        out_shape=[jax.ShapeDtypeStruct((t, LANES), F32)] * 4,
        compiler_params=_params("parallel"),
        name="rope_tables",
    )(pos_col, freq, sel_cos, sel_sin)


def _norm_matmul_kernel(x_ref, g_ref, w_ref, o_ref, xn_ref):
    j = pl.program_id(1)
    tn = o_ref.shape[1]

    @pl.when(j == 0)
    def _():
        _norm_into(x_ref, g_ref, xn_ref)

    w = w_ref[:, pl.ds(pl.multiple_of(j * tn, tn), tn)]
    o_ref[...] = _dot(xn_ref[...], w).astype(o_ref.dtype)


def norm_matmul(x, g, w, out_dtype, tm, tn, name):
    t, k = x.shape
    n = w.shape[1]
    tm, tn = _tile(t, tm), _tile(n, tn)
    return pl.pallas_call(
        _norm_matmul_kernel,
        grid=(t // tm, n // tn),
        in_specs=[pl.BlockSpec((tm, k), lambda i, j: (i, 0)),
                  pl.BlockSpec((1, k), lambda i, j: (0, 0)),
                  pl.BlockSpec((k, n), lambda i, j: (0, 0), pipeline_mode=pl.Buffered(1))],
        out_specs=pl.BlockSpec((tm, tn), lambda i, j: (i, j)),
        out_shape=jax.ShapeDtypeStruct((t, n), out_dtype),
        scratch_shapes=[pltpu.VMEM((tm, k), BF16)],
        compiler_params=_params("parallel", "arbitrary"),
        name=name,
    )(x, g, w)


def _ret_in_kernel(x_ref, g_ref, w_ref, cos_ref, sin_ref, o_ref, xn_ref, *, n_q_tiles, n_rope_tiles):
    j = pl.program_id(1)

    @pl.when(j == 0)
    def _():
        _norm_into(x_ref, g_ref, xn_ref)

    acc = _dot(xn_ref[...], w_ref[...])
    is_rope = j < n_rope_tiles
    scale = jnp.where(j >= n_q_tiles, RET_DK ** -0.5, 1.0).astype(F32)
    cos = cos_ref[...] * scale
    sin = sin_ref[...] * scale
    half = RET_DK // 2
    for hh in range(acc.shape[1] // RET_DK):
        x1 = acc[:, hh * RET_DK:hh * RET_DK + half]
        x2 = acc[:, hh * RET_DK + half:(hh + 1) * RET_DK]
        o_ref[:, hh * RET_DK:hh * RET_DK + half] = jnp.where(
            is_rope, x1 * cos - x2 * sin, x1).astype(o_ref.dtype)
        o_ref[:, hh * RET_DK + half:(hh + 1) * RET_DK] = jnp.where(
            is_rope, x2 * cos + x1 * sin, x2).astype(o_ref.dtype)


def ret_in_proj(x, g, w, cos, sin, tm, tn):
    t, k = x.shape
    n = w.shape[1]
    tm, tn = _tile(t, tm), _tile(n, tn)
    qw = RET_HEADS * RET_DK
    kern = functools.partial(_ret_in_kernel, n_q_tiles=qw // tn, n_rope_tiles=2 * qw // tn)
    return pl.pallas_call(
        kern,
        grid=(t // tm, n // tn),
        in_specs=[pl.BlockSpec((tm, k), lambda i, j: (i, 0)),
                  pl.BlockSpec((1, k), lambda i, j: (0, 0)),
                  pl.BlockSpec((k, tn), lambda i, j: (0, j)),
                  pl.BlockSpec((tm, RET_DK // 2), lambda i, j: (i, 0)),
                  pl.BlockSpec((tm, RET_DK // 2), lambda i, j: (i, 0))],
        out_specs=pl.BlockSpec((tm, tn), lambda i, j: (i, j)),
        out_shape=jax.ShapeDtypeStruct((t, n), BF16),
        scratch_shapes=[pltpu.VMEM((tm, k), BF16)],
        compiler_params=_params("parallel", "arbitrary"),
        name="ret_in_proj",
    )(x, g, w, cos, sin)


def _res_matmul_kernel(*refs):
    h_ref, w_ref, o_ref = refs[0], refs[-2], refs[-1]
    lhs = [a_ref[...] for a_ref in refs[1:-2]]
    lhs = lhs[0] if len(lhs) == 1 else jnp.concatenate(lhs, axis=1)
    o_ref[...] = h_ref[...] + _dot(lhs, w_ref[...])


def res_matmul(h, lhs, w, tm, tn, name):
    t, n = h.shape
    tm, tn = _tile(t, tm), _tile(n, tn)
    assert sum(a.shape[1] for a in lhs) == w.shape[0]
    in_specs = [pl.BlockSpec((tm, tn), lambda i, j: (i, j))]
    in_specs += [pl.BlockSpec((tm, a.shape[1]), lambda i, j: (i, 0)) for a in lhs]
    in_specs += [pl.BlockSpec((w.shape[0], tn), lambda i, j: (0, j),
                              pipeline_mode=pl.Buffered(1) if tn == n else None)]
    return pl.pallas_call(
        _res_matmul_kernel,
        grid=(t // tm, n // tn),
        in_specs=in_specs,
        out_specs=pl.BlockSpec((tm, tn), lambda i, j: (i, j)),
        out_shape=jax.ShapeDtypeStruct((t, n), F32),
        compiler_params=_params("parallel", "arbitrary"),
        name=name,
    )(h, *lhs, w)


def _ffn_kernel(h_ref, g_ref, wg_ref, wu_ref, cw_ref, cb_ref, wd_ref, o_ref,
                u_ref, gs_ref, tail_ref, *, blocks_per_seq):
    i, j = pl.program_id(0), pl.program_id(1)
    tm = h_ref.shape[0]

    @pl.when(j == 0)
    def _():
        _norm_into(h_ref, g_ref, u_ref)
        o_ref[...] = h_ref[...]

    @pl.when(jnp.logical_and(i == 0, j == 0))
    def _():
        tail_ref[...] = jnp.zeros(tail_ref.shape, F32)

    seq_start = (i % blocks_per_seq) == 0
    gs_ref[0:SUBLANES, :] = jnp.where(seq_start, 0.0, tail_ref[j])
    cw, cb = cw_ref[...], cb_ref[...]
    hm = tm // 2
    for r in range(0, tm, hm):
        u = u_ref[r:r + hm, :]
        a = _dot(u, wg_ref[...])
        up = _dot(u, wu_ref[...])
        gs_ref[SUBLANES + r:SUBLANES + r + hm, :] = a
        c = cb + cw[CONV_W - 1:CONV_W, :] * a
        for tap in range(CONV_W - 1):
            r0 = SUBLANES + r - (CONV_W - 1 - tap)
            c = c + cw[tap:tap + 1, :] * gs_ref[r0:r0 + hm, :]
        act = (c * jax.nn.sigmoid(c) * up).astype(BF16)
        o_ref[r:r + hm, :] += _dot(act, wd_ref[...])
    tail_ref[j] = gs_ref[tm:tm + SUBLANES, :]


def conv_ffn(h, g, wg, wu, cw, cb, wd, seq, tm, tf):
    t, d = h.shape
    f = wg.shape[1]
    tm, tf = _tile(seq, tm), _tile(f, tf)
    kern = functools.partial(_ffn_kernel, blocks_per_seq=seq // tm)
    return pl.pallas_call(
        kern,
        grid=(t // tm, f // tf),
        in_specs=[pl.BlockSpec((tm, d), lambda i, j: (i, 0)),
                  pl.BlockSpec((1, d), lambda i, j: (0, 0)),
                  pl.BlockSpec((d, tf), lambda i, j: (0, j)),
                  pl.BlockSpec((d, tf), lambda i, j: (0, j)),
                  pl.BlockSpec((CONV_W, tf), lambda i, j: (0, j)),
                  pl.BlockSpec((1, tf), lambda i, j: (0, j)),
                  pl.BlockSpec((tf, d), lambda i, j: (j, 0))],
        out_specs=pl.BlockSpec((tm, d), lambda i, j: (i, 0)),
        out_shape=jax.ShapeDtypeStruct((t, d), F32),
        scratch_shapes=[pltpu.VMEM((tm, d), BF16),
                        pltpu.VMEM((tm + SUBLANES, tf), F32),
                        pltpu.VMEM((f // tf, SUBLANES, tf), F32)],
        compiler_params=_params("arbitrary", "arbitrary"),
        name="conv_ffn",
    )(h, g, wg, wu, cw, cb, wd)


def _ple_kernel(h_ref, g_ref, wgate_ref, p_ref, wproj_ref, o_ref, xn_ref, pb_ref):
    j = pl.program_id(1)
    tn = o_ref.shape[1]

    @pl.when(j == 0)
    def _():
        _norm_into(h_ref, g_ref, xn_ref)
        pb_ref[...] = p_ref[...].astype(BF16)

    gate = jax.nn.sigmoid(_dot(xn_ref[...], wgate_ref[...]))
    proj = _dot(pb_ref[...], wproj_ref[...])
    c0 = pl.multiple_of(j * tn, tn)
    o_ref[...] = h_ref[:, pl.ds(c0, tn)] + proj * gate


def ple(h, g, wgate, p, layer, wproj, tm, tn):
    t, d = h.shape
    tm, tn = _tile(t, tm), _tile(d, tn)
    return pl.pallas_call(
        _ple_kernel,
        grid=(t // tm, d // tn),
        in_specs=[pl.BlockSpec((tm, d), lambda i, j: (i, 0)),
                  pl.BlockSpec((1, d), lambda i, j: (0, 0)),
                  pl.BlockSpec((d, tn), lambda i, j: (0, j)),
                  pl.BlockSpec((None, tm, PLE_DIM), lambda i, j: (layer, i, 0)),
                  pl.BlockSpec((PLE_DIM, tn), lambda i, j: (0, j))],
        out_specs=pl.BlockSpec((tm, tn), lambda i, j: (i, j)),
        out_shape=jax.ShapeDtypeStruct((t, d), F32),
        scratch_shapes=[pltpu.VMEM((tm, d), BF16), pltpu.VMEM((tm, PLE_DIM), BF16)],
        compiler_params=_params("parallel", "arbitrary"),
        name="ple",
    )(h, g, wgate, p, wproj)


def _hgrn_kernel(q_ref, f_ref, i_ref, g_ref, lbl_ref, on_ref, o_ref, st_ref):
    c = HG_CHUNK

    @pl.when(pl.program_id(1) == 0)
    def _():
        st_ref[...] = jnp.zeros(st_ref.shape, F32)

    lg = lbl_ref[...]
    e = jnp.exp(lg - jnp.max(lg, axis=0, keepdims=True))
    lb = e[0:1, :] / jnp.sum(e, axis=0, keepdims=True)
    onorm = on_ref[...]
    pw = 2 * HG_DK
    row = lax.broadcasted_iota(jnp.int32, (c, c), 0)
    col = lax.broadcasted_iota(jnp.int32, (c, c), 1)
    tril = jnp.where(row >= col, 1.0, 0.0).astype(BF16)
    tril3 = jnp.concatenate([tril, tril, tril], axis=1)
    row2 = lax.broadcasted_iota(jnp.int32, (c, 2 * c), 0)
    col2 = lax.broadcasted_iota(jnp.int32, (c, 2 * c), 1)
    causal2 = row2 >= jnp.where(col2 >= c, col2 - c, col2)
    left = lax.broadcasted_iota(jnp.int32, (c, pw), 1) < HG_DK
    zero_state = jnp.zeros((HG_DV, HG_DK), BF16)

    def block_diag(x):
        zero = jnp.zeros_like(x)
        return jnp.concatenate([jnp.where(left, x, zero), jnp.where(left, zero, x)], axis=0)

    def chunk(ci, carry):
        rows = pl.ds(pl.multiple_of(ci * c, c), c)
        q = q_ref[rows, :]
        v = i_ref[rows, :].astype(BF16)
        f = lb + (1.0 - lb) * jax.nn.sigmoid(f_ref[rows, :])
        k = 1.0 - f
        lf = jnp.log(f)
        hi = lf.astype(BF16)
        r1 = lf - hi.astype(F32)
        mid = r1.astype(BF16)
        lo = (r1 - mid.astype(F32)).astype(BF16)
        bc = _dot(tril3, jnp.concatenate([hi, mid, lo], axis=0))
        ref = bc[c // 2 - 1:c // 2, :]
        b_last = bc[c - 1:c, :]
        q_rel = (q * jnp.exp(bc - ref)).astype(BF16)
        k_rel = (k * jnp.exp(ref - bc)).astype(BF16)
        q_dec = (q * jnp.exp(bc)).astype(BF16)
        k_dec = (k * jnp.exp(b_last - bc)).astype(BF16)
        decay = jnp.exp(b_last)
        for p in range(HG_HEADS // 2):
            cols = slice(p * pw, (p + 1) * pw)
            a = lax.dot_general(q_rel[:, cols], block_diag(k_rel[:, cols]), NT, preferred_element_type=F32)
            a = jnp.where(causal2, a, 0.0).astype(BF16)
            st0, st1 = st_ref[2 * p], st_ref[2 * p + 1]
            st_pair = jnp.concatenate(
                [jnp.concatenate([st0.astype(BF16), zero_state], axis=1),
                 jnp.concatenate([zero_state, st1.astype(BF16)], axis=1)], axis=0)
            out = _dot(a, block_diag(v[:, cols])) + lax.dot_general(q_dec[:, cols], st_pair, NT,
                                                                    preferred_element_type=F32)
            upd = lax.dot_general(v[:, cols], k_dec[:, cols], TN, preferred_element_type=F32)
            st_ref[2 * p] = st0 * decay[:, p * pw:p * pw + HG_DK] + upd[:HG_DV, :HG_DK]
            st_ref[2 * p + 1] = st1 * decay[:, p * pw + HG_DK:(p + 1) * pw] + upd[HG_DV:, HG_DK:]
            for hh in range(2):
                hc = slice(p * pw + hh * HG_DV, p * pw + (hh + 1) * HG_DV)
                g = g_ref[rows, hc]
                o_ref[rows, hc] = (_rms(out[:, hh * HG_DV:(hh + 1) * HG_DV], onorm)
                                   * (g * jax.nn.sigmoid(g))).astype(o_ref.dtype)
        return carry

    lax.fori_loop(0, q_ref.shape[0] // c, chunk, 0, unroll=4)


def hgrn(z, lb_logits, onorm, batch, seq, cb):
    t = z.shape[0]
    cb = _tile(seq, cb)
    nblk = seq // cb
    spec = lambda part: pl.BlockSpec((cb, HG_W), lambda b, s, part=part: (b * nblk + s, part))
    return pl.pallas_call(
        _hgrn_kernel,
        grid=(batch, nblk),
        in_specs=[spec(0), spec(1), spec(2), spec(3),
                  pl.BlockSpec(lb_logits.shape, lambda b, s: (0, 0)),
                  pl.BlockSpec((1, HG_DV), lambda b, s: (0, 0))],
        out_specs=pl.BlockSpec((cb, HG_W), lambda b, s: (b * nblk + s, 0)),
        out_shape=jax.ShapeDtypeStruct((t, HG_W), BF16),
        scratch_shapes=[pltpu.VMEM((HG_HEADS, HG_DV, HG_DK), F32)],
        compiler_params=_params("parallel", "arbitrary"),
        name="hgrn2",
    )(z, z, z, z, lb_logits, onorm)


def _mla_prep_kernel(z_ref, qa_ref, kva_ref, wuq_ref, wukv_ref, qn_ref, kn_ref, cos_ref, sin_ref,
                     qt_out, k_out, vt_out):
    cq = _rms(z_ref[:, 0:MLA_Q_RANK], qa_ref[...]).astype(BF16)
    ckv = _rms(z_ref[:, MLA_Q_RANK:MLA_Q_RANK + MLA_KV_RANK], kva_ref[...]).astype(BF16)
    kpe = z_ref[:, MLA_Q_RANK + MLA_KV_RANK:]
    qf = _dot(cq, wuq_ref[...])
    kvf = _dot(ckv, wukv_ref[...])
    cos, sin = cos_ref[...], sin_ref[...]
    qn, kn = qn_ref[...], kn_ref[...]
    scale = MLA_QK ** -0.5 * LOG2E
    tm = z_ref.shape[0]

    def rope(x):
        return x * cos + pltpu.roll(x, LANES // 2, 1) * sin

    kpe_ss = jnp.sum(kpe * kpe, axis=-1, keepdims=True)
    k_rope = rope(kpe * kn[:, MLA_NOPE:])
    ones_rows = jnp.where(lax.broadcasted_iota(jnp.int32, (VT_ROWS - MLA_V, tm), 0) == 0, 1.0, 0.0)
    for h in range(MLA_HEADS):
        c0 = h * MLA_PAD
        qh = qf[:, c0:c0 + MLA_PAD]
        rq = lax.rsqrt(jnp.sum(qh * qh, axis=-1, keepdims=True) * (1.0 / MLA_QK) + EPS) * scale
        qh = qh * rq * qn
        qt_out[h, 0:MLA_NOPE, :] = qh[:, :MLA_NOPE].T.astype(BF16)
        qt_out[h, MLA_NOPE:MLA_PAD, :] = rope(qh[:, MLA_NOPE:]).T.astype(BF16)
        kh = kvf[:, c0:c0 + MLA_NOPE]
        rk = lax.rsqrt((jnp.sum(kh * kh, axis=-1, keepdims=True) + kpe_ss) * (1.0 / MLA_QK) + EPS)
        k_out[:, c0:c0 + MLA_NOPE] = (kh * rk * kn[:, :MLA_NOPE]).astype(BF16)
        k_out[:, c0 + MLA_NOPE:c0 + MLA_PAD] = (k_rope * rk).astype(BF16)
        vt_out[h, 0:MLA_V, :] = kvf[:, c0 + MLA_NOPE:c0 + MLA_PAD].T.astype(BF16)
        vt_out[h, MLA_V:VT_ROWS, :] = ones_rows.astype(BF16)


def mla_prep(z, qa, kva, wuq, wukv, qn, kn, cos, sin, tm):
    t, zw = z.shape
    tm = _tile(t, tm)
    full = lambda a: pl.BlockSpec(a.shape, lambda i: (0, 0))
    rows = lambda w: pl.BlockSpec((tm, w), lambda i: (i, 0))
    cols = lambda r: pl.BlockSpec((MLA_HEADS, r, tm), lambda i: (0, 0, i))
    return pl.pallas_call(
        _mla_prep_kernel,
        grid=(t // tm,),
        in_specs=[rows(zw), full(qa), full(kva), full(wuq), full(wukv), full(qn), full(kn),
                  rows(LANES), rows(LANES)],
        out_specs=[cols(MLA_PAD), rows(MLA_HEADS * MLA_PAD), cols(VT_ROWS)],
        out_shape=[jax.ShapeDtypeStruct((MLA_HEADS, MLA_PAD, t), BF16),
                   jax.ShapeDtypeStruct((t, MLA_HEADS * MLA_PAD), BF16),
                   jax.ShapeDtypeStruct((MLA_HEADS, VT_ROWS, t), BF16)],
        compiler_params=_params("parallel"),
        name="mla_prep",
    )(z, qa, kva, wuq, wukv, qn, kn, cos, sin)


def _flash_kernel(qt_ref, k_ref, vt_ref, o_ref, acc_ref, s_ref, *, tq, tk, tk_main):
    qi = pl.program_id(2)
    qt = qt_ref[0]
    acc_ref[...] = jnp.zeros(acc_ref.shape, F32)

    def scores(k0, kw, q_lo):
        return _dot(k_ref[pl.ds(k0, kw), :], qt[:, q_lo:])

    def update(s, k0, kw, m_all, q_lo):
        m_prev = m_all[:, q_lo:]
        m_new = jnp.maximum(m_prev, jnp.max(s, axis=0, keepdims=True))
        alpha = jnp.exp2(m_prev - m_new)
        p = jnp.exp2(s - m_new).astype(BF16)
        acc_ref[:, q_lo:] = alpha * acc_ref[:, q_lo:] + _dot(vt_ref[0, :, pl.ds(k0, kw)], p)
        return m_new if q_lo == 0 else jnp.concatenate([m_all[:, :q_lo], m_new], axis=1)

    n_pairs = qi
    s_ref[0] = scores(0, tk_main, 0)

    def pair(t, m):
        k0 = pl.multiple_of(t * (2 * tk_main), 2 * tk_main)
        s_ref[1] = scores(k0 + tk_main, tk_main, 0)
        m = update(s_ref[0], k0, tk_main, m, 0)
        s_ref[0] = scores(k0 + 2 * tk_main, tk_main, 0)
        return update(s_ref[1], k0 + tk_main, tk_main, m, 0)

    m = lax.fori_loop(0, n_pairs, pair, jnp.full((1, tq), -jnp.inf, F32))
    base = pl.multiple_of(qi * tq, tq)
    s_ref[1, :, tk_main:] = scores(base + tk_main, tk_main, tk_main)
    for d in range(tq // tk):
        e, r0, q_lo = (d * tk) // tk_main, (d * tk) % tk_main, d * tk
        s = s_ref[e, r0:r0 + tk, q_lo:]
        row = lax.broadcasted_iota(jnp.int32, s.shape, 0)
        col = lax.broadcasted_iota(jnp.int32, s.shape, 1)
        m = update(jnp.where(row <= col, s, -jnp.inf), base + q_lo, tk, m, q_lo)
    acc = acc_ref[...]
    o_ref[...] = (acc[0:MLA_V, :] / acc[MLA_V:MLA_V + 1, :]).T.astype(o_ref.dtype)


def flash_attention(qt, k, vt, batch, seq, tq, tk, tk_main):
    t = k.shape[0]
    tq = _tile(seq, tq)
    tk = _tile(tq, tk)
    tk_main = _tile(tq, tk_main)
    assert tq == 2 * tk_main and tk_main % tk == 0
    nq = seq // tq
    return pl.pallas_call(
        functools.partial(_flash_kernel, tq=tq, tk=tk, tk_main=tk_main),
        grid=(batch, MLA_HEADS, nq),
        in_specs=[pl.BlockSpec((1, MLA_PAD, tq), lambda b, h, i: (h, 0, b * nq + i)),
                  pl.BlockSpec((seq, MLA_PAD), lambda b, h, i: (b, h)),
                  pl.BlockSpec((1, VT_ROWS, seq), lambda b, h, i: (h, 0, b))],
        out_specs=pl.BlockSpec((tq, MLA_V), lambda b, h, i: (b * nq + i, h)),
        out_shape=jax.ShapeDtypeStruct((t, MLA_HEADS * MLA_V), BF16),
        scratch_shapes=[pltpu.VMEM((VT_ROWS, tq), F32), pltpu.VMEM((2, tk_main, tq), F32)],
        compiler_params=_params("parallel", "parallel", "arbitrary"),
        name="mla_flash",
    )(qt, k, vt)


def _ret_kernel(q_ref, k_ref, v_ref, g_ref, o_ref, r_ref):
    c = RET_CHUNK
    hf = pl.program_id(1).astype(F32)

    @pl.when(pl.program_id(2) == 0)
    def _():
        r_ref[...] = jnp.zeros(r_ref.shape, F32)

    def log_gamma(shape):
        return jnp.log(1.0 - jnp.exp2(-5.0 - jnp.full(shape, hf, F32)))

    row = lax.broadcasted_iota(jnp.int32, (c, c), 0)
    col = lax.broadcasted_iota(jnp.int32, (c, c), 1)
    diff = (row - col).astype(F32)
    d_intra = jnp.where(diff >= 0, jnp.exp(jnp.where(diff >= 0, diff, 0.0) * log_gamma((c, c))), 0.0)
    idx = lax.broadcasted_iota(jnp.int32, (c, RET_DK), 0).astype(F32)
    lg_k = log_gamma((c, RET_DK))
    q_dec = jnp.exp((idx + 1.0) * lg_k).astype(BF16)
    k_dec = jnp.exp((c - 1.0 - idx) * lg_k).astype(BF16)
    c_dec = jnp.exp(c * log_gamma((1, RET_DV)))

    def chunk(ci, carry):
        rows = pl.ds(pl.multiple_of(ci * c, c), c)
        q, k, v = q_ref[rows, :], k_ref[rows, :], v_ref[rows, :]
        a = lax.dot_general(q, k, NT, preferred_element_type=F32) * d_intra
        r = r_ref[...]
        out = _dot(a.astype(BF16), v) + _dot(q * q_dec, r.astype(BF16))
        r_ref[...] = r * c_dec + lax.dot_general(k * k_dec, v, TN, preferred_element_type=F32)
        g = g_ref[rows, :].astype(F32)
        o_ref[rows, :] = (_rms(out) * (g * jax.nn.sigmoid(g))).astype(o_ref.dtype)
        return carry

    lax.fori_loop(0, q_ref.shape[0] // c, chunk, 0, unroll=2)


def retention(z, batch, seq, cb):
    t = z.shape[0]
    cb = _tile(seq, cb)
    assert cb % RET_CHUNK == 0
    nblk = seq // cb
    nqk = RET_HEADS * RET_DK // RET_DK
    nv = (2 * RET_HEADS * RET_DK) // RET_DV
    return pl.pallas_call(
        _ret_kernel,
        grid=(batch, RET_HEADS, nblk),
        in_specs=[pl.BlockSpec((cb, RET_DK), lambda b, h, s: (b * nblk + s, h)),
                  pl.BlockSpec((cb, RET_DK), lambda b, h, s: (b * nblk + s, nqk + h)),
                  pl.BlockSpec((cb, RET_DV), lambda b, h, s: (b * nblk + s, nv + h)),
                  pl.BlockSpec((cb, RET_DV), lambda b, h, s: (b * nblk + s, nv + RET_HEADS + h))],
        out_specs=pl.BlockSpec((cb, RET_DV), lambda b, h, s: (b * nblk + s, h)),
        out_shape=jax.ShapeDtypeStruct((t, RET_HEADS * RET_DV), BF16),
        scratch_shapes=[pltpu.VMEM((RET_DK, RET_DV), F32)],
        compiler_params=_params("parallel", "parallel", "arbitrary"),
        name="retention",
    )(z, z, z, z)


def _rope_pad(a, axis):
    x1, x2 = jnp.split(a, 2, axis=axis)
    z = jnp.zeros_like(x1)
    return jnp.concatenate([x1, z, x2, z], axis=axis)


def _pad_qk_gain(g):
    return jnp.concatenate([g[:MLA_NOPE], _rope_pad(g[MLA_NOPE:], 0)])[None, :]


def kernel(x, p, positions, norm_mix, norm_ffn, norm_ple, e_w_in, e_lb_logits, e_q_a_norm, e_kv_a_norm, e_w_uq, e_w_ukv, e_q_norm, e_k_norm, e_hg_onorm, e_w_out, o_w_in, o_w_out, ffn_w_gate, ffn_w_up, ffn_conv_w, ffn_conv_b, ffn_w_down, ple_w_proj, ple_w_gate):
    batch, seq, d = x.shape
    assert norm_mix.shape[0] == 2 and e_w_in.shape[0] == 1 and o_w_in.shape[0] == 1
    t = batch * seq
    h = x.reshape(t, d)
    pos_col = positions.reshape(t, 1)

    f_ret = (ROPE_BASE ** (-jnp.arange(RET_DK // 2, dtype=F32) / (RET_DK // 2)))[None, :]
    n_mla, stride = MLA_ROPE // 2, (RET_DK // 2) // (MLA_ROPE // 2)
    src, x1, x2 = jnp.arange(n_mla) * stride, jnp.arange(n_mla), LANES // 2 + jnp.arange(n_mla)
    sel = jnp.zeros((RET_DK // 2, LANES), F32)
    sel_cos = sel.at[src, x1].set(1.0).at[src, x2].set(1.0)
    sel_sin = sel.at[src, x1].set(-1.0).at[src, x2].set(1.0)
    cos_ret, sin_ret, cos_mla, sin_mla = rope_tables(pos_col, f_ret, sel_cos.astype(BF16),
                                                     sel_sin.astype(BF16))

    w_in = e_w_in[0]
    w_hg = cast_bf16(e_w_in, 0, 4 * HG_W, "cast_w_hgrn")
    off = 4 * HG_W + MLA_Q_RANK + MLA_KV_RANK
    w_mla = jnp.concatenate([w_in[:, 4 * HG_W:off], _rope_pad(w_in[:, off:], 1)], axis=1).astype(BF16)
    wuq = e_w_uq[0].reshape(MLA_Q_RANK, MLA_HEADS, MLA_QK)
    wuq = jnp.concatenate([wuq[..., :MLA_NOPE], _rope_pad(wuq[..., MLA_NOPE:], 2)], axis=-1)
    wuq = wuq.reshape(MLA_Q_RANK, MLA_HEADS * MLA_PAD).astype(BF16)
    wukv = e_w_ukv[0].astype(BF16)
    g_mix0 = norm_mix[0][None, :]

    z_hg = norm_matmul(h, g_mix0, w_hg, F32, ROWS, COLS, "in_proj_hgrn")
    z_mla = norm_matmul(h, g_mix0, w_mla, F32, ROWS, w_mla.shape[1], "in_proj_mla")
    o_a = hgrn(z_hg, e_lb_logits, e_hg_onorm[0][None, :], batch, seq, HGRN_ROWS)
    qt, k, vt = mla_prep(z_mla, e_q_a_norm[0][None, :], e_kv_a_norm[0][None, :], wuq, wukv,
                         _pad_qk_gain(e_q_norm[0]), _pad_qk_gain(e_k_norm[0]), cos_mla, sin_mla, PREP_ROWS)
    o_b = flash_attention(qt, k, vt, batch, seq, FLASH_Q, FLASH_KV_DIAG, FLASH_KV)
    h = res_matmul(h, [o_a, o_b], cast_bf16(e_w_out, 0, name="cast_w_even_out"), ROWS, COLS,
                   "out_proj_even")

    def channel_mix(h, i):
        h = conv_ffn(h, norm_ffn[i][None, :], cast_bf16(ffn_w_gate, i, name="cast_w_gate"),
                     cast_bf16(ffn_w_up, i, name="cast_w_up"), ffn_conv_w[i], ffn_conv_b[i][None, :],
                     cast_bf16(ffn_w_down, i, name="cast_w_down"), seq, ROWS, FFN_COLS)
        return ple(h, norm_ple[i][None, :], cast_bf16(ple_w_gate, i, name="cast_w_ple"),
                   p.reshape(p.shape[0], t, PLE_DIM), i, ple_w_proj[i].astype(BF16), ROWS, COLS)

    h = channel_mix(h, 0)

    z_ret = ret_in_proj(h, norm_mix[1][None, :], cast_bf16(o_w_in, 0, name="cast_w_ret_in"),
                        cos_ret, sin_ret, ROWS, COLS)
    o_r = retention(z_ret, batch, seq, RET_ROWS)
    h = res_matmul(h, [o_r], cast_bf16(o_w_out, 0, name="cast_w_ret_out"), ROWS, COLS // 2, "out_proj_odd")
    h = channel_mix(h, 1)
    return h.reshape(batch, seq, d)
```

```python
import functools

import jax
import jax.numpy as jnp
from jax import lax
from jax.experimental import pallas as pl
from jax.experimental.pallas import tpu as pltpu

F32 = jnp.float32
BF16 = jnp.bfloat16

LANES = 128
SUBLANES = 8
BF16_SUBLANES = 16
MXU_DIM = 256
VMEM_BYTES = 64 * 2**20

PLE_DIM = 256
HG_HEADS = 8
HG_DK = 128
HG_DV = 128
HG_W = HG_HEADS * HG_DK
HG_CHUNK = 64
MLA_HEADS = 8
MLA_Q_RANK = 512
MLA_KV_RANK = 512
MLA_NOPE = 128
MLA_ROPE = 64
MLA_V = 128
MLA_QK = MLA_NOPE + MLA_ROPE
MLA_PAD = MLA_NOPE + LANES
VT_ROWS = MLA_V + BF16_SUBLANES
LOG2E = 1.4426950408889634
RET_HEADS = 8
RET_DK = 256
RET_DV = 512
RET_CHUNK = 256
CONV_W = 3
FFN_STRIP = MXU_DIM
ROPE_BASE = 10000.0
EPS = 1e-6

VMEM_LIMIT = VMEM_BYTES - 4 * 2**20
CAST_BLOCK_BYTES = 8 * 2**20

ROWS = 1024
NORM_ROWS = 256
COLS = 2048
FFN_COLS = 512
PREP_ROWS = 1024
HGRN_ROWS = 1024
RET_ROWS = 2048
FLASH_Q = 2048
FLASH_KV = 1024
FLASH_KV_DIAG = 512

NT = (((1,), (1,)), ((), ()))
TN = (((0,), (0,)), ((), ()))


def _params(*sem):
    return pltpu.CompilerParams(dimension_semantics=sem, vmem_limit_bytes=VMEM_LIMIT)


def _tile(n, pref):
    t = min(n, pref)
    assert n % t == 0, (n, pref)
    return t


def _dot(a, b):
    return jnp.dot(a, b, preferred_element_type=F32)


def _rms(x, g=None):
    y = x * lax.rsqrt(jnp.sum(x * x, axis=-1, keepdims=True) * (1.0 / x.shape[-1]) + EPS)
    return y if g is None else y * g


def _norm_into(x_ref, g_ref, dst_ref, rows=NORM_ROWS):
    g = g_ref[...]

    def body(c, carry):
        r0 = pl.multiple_of(c * rows, rows)
        dst_ref[pl.ds(r0, rows), :] = _rms(x_ref[pl.ds(r0, rows), :], g).astype(dst_ref.dtype)
        return carry

    lax.fori_loop(0, x_ref.shape[0] // rows, body, 0)


def _cast_kernel(w_ref, o_ref):
    o_ref[...] = w_ref[...].astype(o_ref.dtype)


def cast_bf16(w, layer, ncols=None, name="cast_bf16"):
    _, r, c = w.shape
    ncols = c if ncols is None else ncols
    tr = r
    while tr % 16 == 0 and tr * ncols * 4 > CAST_BLOCK_BYTES:
        tr //= 2
    return pl.pallas_call(
        _cast_kernel,
        grid=(r // tr,),
        in_specs=[pl.BlockSpec((None, tr, ncols), lambda i: (layer, i, 0))],
        out_specs=pl.BlockSpec((tr, ncols), lambda i: (i, 0)),
        out_shape=jax.ShapeDtypeStruct((r, ncols), BF16),
        compiler_params=_params("parallel"),
        name=name,
    )(w)


def _select_lanes(x, sel):
    hi = x.astype(BF16)
    r1 = x - hi.astype(F32)
    mid = r1.astype(BF16)
    lo = (r1 - mid.astype(F32)).astype(BF16)
    return _dot(hi, sel) + _dot(mid, sel) + _dot(lo, sel)


def _rope_tab_kernel(pos_ref, f_ref, selc_ref, sels_ref, c_ref, s_ref, cm_ref, sm_ref):
    ang = pos_ref[...].astype(F32) * f_ref[...]
    c, s = jnp.cos(ang), jnp.sin(ang)
    c_ref[...] = c
    s_ref[...] = s
    cm_ref[...] = _select_lanes(c, selc_ref[...])
    sm_ref[...] = _select_lanes(s, sels_ref[...])


def rope_tables(pos_col, freq, sel_cos, sel_sin):
    t = pos_col.shape[0]
    tm = _tile(t, ROWS)
    const = lambda a: pl.BlockSpec(a.shape, lambda i: (0, 0))
    return pl.pallas_call(
        _rope_tab_kernel,
        grid=(t // tm,),
        in_specs=[pl.BlockSpec((tm, 1), lambda i: (i, 0)), const(freq), const(sel_cos), const(sel_sin)],
        out_specs=[pl.BlockSpec((tm, LANES), lambda i: (i, 0))] * 4,
        out_shape=[jax.ShapeDtypeStruct((t, LANES), F32)] * 4,
        compiler_params=_params("parallel"),
        name="rope_tables",
    )(pos_col, freq, sel_cos, sel_sin)


def _norm_matmul_kernel(x_ref, g_ref, w_ref, o_ref, xn_ref):
    j = pl.program_id(1)
    tn = o_ref.shape[1]

    @pl.when(j == 0)
    def _():
        _norm_into(x_ref, g_ref, xn_ref)

    w = w_ref[:, pl.ds(pl.multiple_of(j * tn, tn), tn)]
    o_ref[...] = _dot(xn_ref[...], w).astype(o_ref.dtype)


def norm_matmul(x, g, w, out_dtype, tm, tn, name):
    t, k = x.shape
    n = w.shape[1]
    tm, tn = _tile(t, tm), _tile(n, tn)
    return pl.pallas_call(
        _norm_matmul_kernel,
        grid=(t // tm, n // tn),
        in_specs=[pl.BlockSpec((tm, k), lambda i, j: (i, 0)),
                  pl.BlockSpec((1, k), lambda i, j: (0, 0)),
                  pl.BlockSpec((k, n), lambda i, j: (0, 0), pipeline_mode=pl.Buffered(1))],
        out_specs=pl.BlockSpec((tm, tn), lambda i, j: (i, j)),
        out_shape=jax.ShapeDtypeStruct((t, n), out_dtype),
        scratch_shapes=[pltpu.VMEM((tm, k), BF16)],
        compiler_params=_params("parallel", "arbitrary"),
        name=name,
    )(x, g, w)


def _ret_in_kernel(x_ref, g_ref, w_ref, cos_ref, sin_ref, o_ref, xn_ref, *, n_q_tiles, n_rope_tiles):
    j = pl.program_id(1)

    @pl.when(j == 0)
    def _():
        _norm_into(x_ref, g_ref, xn_ref)

    acc = _dot(xn_ref[...], w_ref[...])
    is_rope = j < n_rope_tiles
    scale = jnp.where(j >= n_q_tiles, RET_DK ** -0.5, 1.0).astype(F32)
    cos = cos_ref[...] * scale
    sin = sin_ref[...] * scale
    half = RET_DK // 2
    for hh in range(acc.shape[1] // RET_DK):
        x1 = acc[:, hh * RET_DK:hh * RET_DK + half]
        x2 = acc[:, hh * RET_DK + half:(hh + 1) * RET_DK]
        o_ref[:, hh * RET_DK:hh * RET_DK + half] = jnp.where(
            is_rope, x1 * cos - x2 * sin, x1).astype(o_ref.dtype)
        o_ref[:, hh * RET_DK + half:(hh + 1) * RET_DK] = jnp.where(
            is_rope, x2 * cos + x1 * sin, x2).astype(o_ref.dtype)


def ret_in_proj(x, g, w, cos, sin, tm, tn):
    t, k = x.shape
    n = w.shape[1]
    tm, tn = _tile(t, tm), _tile(n, tn)
    qw = RET_HEADS * RET_DK
    kern = functools.partial(_ret_in_kernel, n_q_tiles=qw // tn, n_rope_tiles=2 * qw // tn)
    return pl.pallas_call(
        kern,
        grid=(t // tm, n // tn),
        in_specs=[pl.BlockSpec((tm, k), lambda i, j: (i, 0)),
                  pl.BlockSpec((1, k), lambda i, j: (0, 0)),
                  pl.BlockSpec((k, tn), lambda i, j: (0, j)),
                  pl.BlockSpec((tm, RET_DK // 2), lambda i, j: (i, 0)),
                  pl.BlockSpec((tm, RET_DK // 2), lambda i, j: (i, 0))],
        out_specs=pl.BlockSpec((tm, tn), lambda i, j: (i, j)),
        out_shape=jax.ShapeDtypeStruct((t, n), BF16),
        scratch_shapes=[pltpu.VMEM((tm, k), BF16)],
        compiler_params=_params("parallel", "arbitrary"),
        name="ret_in_proj",
    )(x, g, w, cos, sin)


def _res_matmul_kernel(*refs):
    h_ref, w_ref, o_ref = refs[0], refs[-2], refs[-1]
    lhs = [a_ref[...] for a_ref in refs[1:-2]]
    lhs = lhs[0] if len(lhs) == 1 else jnp.concatenate(lhs, axis=1)
    o_ref[...] = h_ref[...] + _dot(lhs, w_ref[...])


def res_matmul(h, lhs, w, tm, tn, name):
    t, n = h.shape
    tm, tn = _tile(t, tm), _tile(n, tn)
    assert sum(a.shape[1] for a in lhs) == w.shape[0]
    in_specs = [pl.BlockSpec((tm, tn), lambda i, j: (i, j))]
    in_specs += [pl.BlockSpec((tm, a.shape[1]), lambda i, j: (i, 0)) for a in lhs]
    in_specs += [pl.BlockSpec((w.shape[0], tn), lambda i, j: (0, j),
                              pipeline_mode=pl.Buffered(1) if tn == n else None)]
    return pl.pallas_call(
        _res_matmul_kernel,
        grid=(t // tm, n // tn),
        in_specs=in_specs,
        out_specs=pl.BlockSpec((tm, tn), lambda i, j: (i, j)),
        out_shape=jax.ShapeDtypeStruct((t, n), F32),
        compiler_params=_params("parallel", "arbitrary"),
        name=name,
    )(h, *lhs, w)


def _ffn_kernel(h_ref, g_ref, wg_ref, wu_ref, cw_ref, cb_ref, wd_ref, o_ref,
                u_ref, gs_ref, tail_ref, *, blocks_per_seq):
    i, j = pl.program_id(0), pl.program_id(1)
    tm = h_ref.shape[0]

    @pl.when(j == 0)
    def _():
        _norm_into(h_ref, g_ref, u_ref)
        o_ref[...] = h_ref[...]

    @pl.when(jnp.logical_and(i == 0, j == 0))
    def _():
        tail_ref[...] = jnp.zeros(tail_ref.shape, F32)

    seq_start = (i % blocks_per_seq) == 0
    gs_ref[0:SUBLANES, :] = jnp.where(seq_start, 0.0, tail_ref[j])
    cw, cb = cw_ref[...], cb_ref[...]
    hm = tm // 2
    for r in range(0, tm, hm):
        u = u_ref[r:r + hm, :]
        a = _dot(u, wg_ref[...])
        up = _dot(u, wu_ref[...])
        gs_ref[SUBLANES + r:SUBLANES + r + hm, :] = a
        c = cb + cw[CONV_W - 1:CONV_W, :] * a
        for tap in range(CONV_W - 1):
            r0 = SUBLANES + r - (CONV_W - 1 - tap)
            c = c + cw[tap:tap + 1, :] * gs_ref[r0:r0 + hm, :]
        act = (c * jax.nn.sigmoid(c) * up).astype(BF16)
        o_ref[r:r + hm, :] += _dot(act, wd_ref[...])
    tail_ref[j] = gs_ref[tm:tm + SUBLANES, :]


def conv_ffn(h, g, wg, wu, cw, cb, wd, seq, tm, tf):
    t, d = h.shape
    f = wg.shape[1]
    tm, tf = _tile(seq, tm), _tile(f, tf)
    kern = functools.partial(_ffn_kernel, blocks_per_seq=seq // tm)
    return pl.pallas_call(
        kern,
        grid=(t // tm, f // tf),
        in_specs=[pl.BlockSpec((tm, d), lambda i, j: (i, 0)),
                  pl.BlockSpec((1, d), lambda i, j: (0, 0)),
                  pl.BlockSpec((d, tf), lambda i, j: (0, j)),
                  pl.BlockSpec((d, tf), lambda i, j: (0, j)),
                  pl.BlockSpec((CONV_W, tf), lambda i, j: (0, j)),
                  pl.BlockSpec((1, tf), lambda i, j: (0, j)),
                  pl.BlockSpec((tf, d), lambda i, j: (j, 0))],
        out_specs=pl.BlockSpec((tm, d), lambda i, j: (i, 0)),
        out_shape=jax.ShapeDtypeStruct((t, d), F32),
        scratch_shapes=[pltpu.VMEM((tm, d), BF16),
                        pltpu.VMEM((tm + SUBLANES, tf), F32),
                        pltpu.VMEM((f // tf, SUBLANES, tf), F32)],
        compiler_params=_params("arbitrary", "arbitrary"),
        name="conv_ffn",
    )(h, g, wg, wu, cw, cb, wd)


def _ple_kernel(h_ref, g_ref, wgate_ref, p_ref, wproj_ref, o_ref, xn_ref, pb_ref):
    j = pl.program_id(1)
    tn = o_ref.shape[1]

    @pl.when(j == 0)
    def _():
        _norm_into(h_ref, g_ref, xn_ref)
        pb_ref[...] = p_ref[...].astype(BF16)

    gate = jax.nn.sigmoid(_dot(xn_ref[...], wgate_ref[...]))
    proj = _dot(pb_ref[...], wproj_ref[...])
    c0 = pl.multiple_of(j * tn, tn)
    o_ref[...] = h_ref[:, pl.ds(c0, tn)] + proj * gate


def ple(h, g, wgate, p, layer, wproj, tm, tn):
    t, d = h.shape
    tm, tn = _tile(t, tm), _tile(d, tn)
    return pl.pallas_call(
        _ple_kernel,
        grid=(t // tm, d // tn),
        in_specs=[pl.BlockSpec((tm, d), lambda i, j: (i, 0)),
                  pl.BlockSpec((1, d), lambda i, j: (0, 0)),
                  pl.BlockSpec((d, tn), lambda i, j: (0, j)),
                  pl.BlockSpec((None, tm, PLE_DIM), lambda i, j: (layer, i, 0)),
                  pl.BlockSpec((PLE_DIM, tn), lambda i, j: (0, j))],
        out_specs=pl.BlockSpec((tm, tn), lambda i, j: (i, j)),
        out_shape=jax.ShapeDtypeStruct((t, d), F32),
        scratch_shapes=[pltpu.VMEM((tm, d), BF16), pltpu.VMEM((tm, PLE_DIM), BF16)],
        compiler_params=_params("parallel", "arbitrary"),
        name="ple",
    )(h, g, wgate, p, wproj)


def _hgrn_kernel(q_ref, f_ref, i_ref, g_ref, lbl_ref, on_ref, o_ref, st_ref):
    c = HG_CHUNK

    @pl.when(pl.program_id(1) == 0)
    def _():
        st_ref[...] = jnp.zeros(st_ref.shape, F32)

    lg = lbl_ref[...]
    e = jnp.exp(lg - jnp.max(lg, axis=0, keepdims=True))
    lb = e[0:1, :] / jnp.sum(e, axis=0, keepdims=True)
    onorm = on_ref[...]
    pw = 2 * HG_DK
    row = lax.broadcasted_iota(jnp.int32, (c, c), 0)
    col = lax.broadcasted_iota(jnp.int32, (c, c), 1)
    tril = jnp.where(row >= col, 1.0, 0.0).astype(BF16)
    tril3 = jnp.concatenate([tril, tril, tril], axis=1)
    row2 = lax.broadcasted_iota(jnp.int32, (c, 2 * c), 0)
    col2 = lax.broadcasted_iota(jnp.int32, (c, 2 * c), 1)
    causal2 = row2 >= jnp.where(col2 >= c, col2 - c, col2)
    left = lax.broadcasted_iota(jnp.int32, (c, pw), 1) < HG_DK
    zero_state = jnp.zeros((HG_DV, HG_DK), BF16)

    def block_diag(x):
        zero = jnp.zeros_like(x)
        return jnp.concatenate([jnp.where(left, x, zero), jnp.where(left, zero, x)], axis=0)

    def chunk(ci, carry):
        rows = pl.ds(pl.multiple_of(ci * c, c), c)
        q = q_ref[rows, :]
        v = i_ref[rows, :].astype(BF16)
        f = lb + (1.0 - lb) * jax.nn.sigmoid(f_ref[rows, :])
        k = 1.0 - f
        lf = jnp.log(f)
        hi = lf.astype(BF16)
        r1 = lf - hi.astype(F32)
        mid = r1.astype(BF16)
        lo = (r1 - mid.astype(F32)).astype(BF16)
        bc = _dot(tril3, jnp.concatenate([hi, mid, lo], axis=0))
        ref = bc[c // 2 - 1:c // 2, :]
        b_last = bc[c - 1:c, :]
        q_rel = (q * jnp.exp(bc - ref)).astype(BF16)
        k_rel = (k * jnp.exp(ref - bc)).astype(BF16)
        q_dec = (q * jnp.exp(bc)).astype(BF16)
        k_dec = (k * jnp.exp(b_last - bc)).astype(BF16)
        decay = jnp.exp(b_last)
        for p in range(HG_HEADS // 2):
            cols = slice(p * pw, (p + 1) * pw)
            a = lax.dot_general(q_rel[:, cols], block_diag(k_rel[:, cols]), NT, preferred_element_type=F32)
            a = jnp.where(causal2, a, 0.0).astype(BF16)
            st0, st1 = st_ref[2 * p], st_ref[2 * p + 1]
            st_pair = jnp.concatenate(
                [jnp.concatenate([st0.astype(BF16), zero_state], axis=1),
                 jnp.concatenate([zero_state, st1.astype(BF16)], axis=1)], axis=0)
            out = _dot(a, block_diag(v[:, cols])) + lax.dot_general(q_dec[:, cols], st_pair, NT,
                                                                    preferred_element_type=F32)
            upd = lax.dot_general(v[:, cols], k_dec[:, cols], TN, preferred_element_type=F32)
            st_ref[2 * p] = st0 * decay[:, p * pw:p * pw + HG_DK] + upd[:HG_DV, :HG_DK]
            st_ref[2 * p + 1] = st1 * decay[:, p * pw + HG_DK:(p + 1) * pw] + upd[HG_DV:, HG_DK:]
            for hh in range(2):
                hc = slice(p * pw + hh * HG_DV, p * pw + (hh + 1) * HG_DV)
                g = g_ref[rows, hc]
                o_ref[rows, hc] = (_rms(out[:, hh * HG_DV:(hh + 1) * HG_DV], onorm)
                                   * (g * jax.nn.sigmoid(g))).astype(o_ref.dtype)
        return carry

    lax.fori_loop(0, q_ref.shape[0] // c, chunk, 0, unroll=4)


def hgrn(z, lb_logits, onorm, batch, seq, cb):
    t = z.shape[0]
    cb = _tile(seq, cb)
    nblk = seq // cb
    spec = lambda part: pl.BlockSpec((cb, HG_W), lambda b, s, part=part: (b * nblk + s, part))
    return pl.pallas_call(
        _hgrn_kernel,
        grid=(batch, nblk),
        in_specs=[spec(0), spec(1), spec(2), spec(3),
                  pl.BlockSpec(lb_logits.shape, lambda b, s: (0, 0)),
                  pl.BlockSpec((1, HG_DV), lambda b, s: (0, 0))],
        out_specs=pl.BlockSpec((cb, HG_W), lambda b, s: (b * nblk + s, 0)),
        out_shape=jax.ShapeDtypeStruct((t, HG_W), BF16),
        scratch_shapes=[pltpu.VMEM((HG_HEADS, HG_DV, HG_DK), F32)],
        compiler_params=_params("parallel", "arbitrary"),
        name="hgrn2",
    )(z, z, z, z, lb_logits, onorm)


def _mla_prep_kernel(z_ref, qa_ref, kva_ref, wuq_ref, wukv_ref, qn_ref, kn_ref, cos_ref, sin_ref,
                     qt_out, k_out, vt_out):
    cq = _rms(z_ref[:, 0:MLA_Q_RANK], qa_ref[...]).astype(BF16)
    ckv = _rms(z_ref[:, MLA_Q_RANK:MLA_Q_RANK + MLA_KV_RANK], kva_ref[...]).astype(BF16)
    kpe = z_ref[:, MLA_Q_RANK + MLA_KV_RANK:]
    qf = _dot(cq, wuq_ref[...])
    kvf = _dot(ckv, wukv_ref[...])
    cos, sin = cos_ref[...], sin_ref[...]
    qn, kn = qn_ref[...], kn_ref[...]
    scale = MLA_QK ** -0.5 * LOG2E
    tm = z_ref.shape[0]

    def rope(x):
        return x * cos + pltpu.roll(x, LANES // 2, 1) * sin

    kpe_ss = jnp.sum(kpe * kpe, axis=-1, keepdims=True)
    k_rope = rope(kpe * kn[:, MLA_NOPE:])
    ones_rows = jnp.where(lax.broadcasted_iota(jnp.int32, (VT_ROWS - MLA_V, tm), 0) == 0, 1.0, 0.0)
    for h in range(MLA_HEADS):
        c0 = h * MLA_PAD
        qh = qf[:, c0:c0 + MLA_PAD]
        rq = lax.rsqrt(jnp.sum(qh * qh, axis=-1, keepdims=True) * (1.0 / MLA_QK) + EPS) * scale
        qh = qh * rq * qn
        qt_out[h, 0:MLA_NOPE, :] = qh[:, :MLA_NOPE].T.astype(BF16)
        qt_out[h, MLA_NOPE:MLA_PAD, :] = rope(qh[:, MLA_NOPE:]).T.astype(BF16)
        kh = kvf[:, c0:c0 + MLA_NOPE]
        rk = lax.rsqrt((jnp.sum(kh * kh, axis=-1, keepdims=True) + kpe_ss) * (1.0 / MLA_QK) + EPS)
        k_out[:, c0:c0 + MLA_NOPE] = (kh * rk * kn[:, :MLA_NOPE]).astype(BF16)
        k_out[:, c0 + MLA_NOPE:c0 + MLA_PAD] = (k_rope * rk).astype(BF16)
        vt_out[h, 0:MLA_V, :] = kvf[:, c0 + MLA_NOPE:c0 + MLA_PAD].T.astype(BF16)
        vt_out[h, MLA_V:VT_ROWS, :] = ones_rows.astype(BF16)


def mla_prep(z, qa, kva, wuq, wukv, qn, kn, cos, sin, tm):
    t, zw = z.shape
    tm = _tile(t, tm)
    full = lambda a: pl.BlockSpec(a.shape, lambda i: (0, 0))
    rows = lambda w: pl.BlockSpec((tm, w), lambda i: (i, 0))
    cols = lambda r: pl.BlockSpec((MLA_HEADS, r, tm), lambda i: (0, 0, i))
    return pl.pallas_call(
        _mla_prep_kernel,
        grid=(t // tm,),
        in_specs=[rows(zw), full(qa), full(kva), full(wuq), full(wukv), full(qn), full(kn),
                  rows(LANES), rows(LANES)],
        out_specs=[cols(MLA_PAD), rows(MLA_HEADS * MLA_PAD), cols(VT_ROWS)],
        out_shape=[jax.ShapeDtypeStruct((MLA_HEADS, MLA_PAD, t), BF16),
                   jax.ShapeDtypeStruct((t, MLA_HEADS * MLA_PAD), BF16),
                   jax.ShapeDtypeStruct((MLA_HEADS, VT_ROWS, t), BF16)],
        compiler_params=_params("parallel"),
        name="mla_prep",
    )(z, qa, kva, wuq, wukv, qn, kn, cos, sin)


def _flash_kernel(qt_ref, k_ref, vt_ref, o_ref, acc_ref, s_ref, *, tq, tk, tk_main):
    qi = pl.program_id(2)
    qt = qt_ref[0]
    acc_ref[...] = jnp.zeros(acc_ref.shape, F32)

    def scores(k0, kw, q_lo):
        return _dot(k_ref[pl.ds(k0, kw), :], qt[:, q_lo:])

    def update(s, k0, kw, m_all, q_lo):
        m_prev = m_all[:, q_lo:]
        m_new = jnp.maximum(m_prev, jnp.max(s, axis=0, keepdims=True))
        alpha = jnp.exp2(m_prev - m_new)
        p = jnp.exp2(s - m_new).astype(BF16)
        acc_ref[:, q_lo:] = alpha * acc_ref[:, q_lo:] + _dot(vt_ref[0, :, pl.ds(k0, kw)], p)
        return m_new if q_lo == 0 else jnp.concatenate([m_all[:, :q_lo], m_new], axis=1)

    n_pairs = qi
    s_ref[0] = scores(0, tk_main, 0)

    def pair(t, m):
        k0 = pl.multiple_of(t * (2 * tk_main), 2 * tk_main)
        s_ref[1] = scores(k0 + tk_main, tk_main, 0)
        m = update(s_ref[0], k0, tk_main, m, 0)
        s_ref[0] = scores(k0 + 2 * tk_main, tk_main, 0)
        return update(s_ref[1], k0 + tk_main, tk_main, m, 0)

    m = lax.fori_loop(0, n_pairs, pair, jnp.full((1, tq), -jnp.inf, F32))
    base = pl.multiple_of(qi * tq, tq)
    s_ref[1, :, tk_main:] = scores(base + tk_main, tk_main, tk_main)
    for d in range(tq // tk):
        e, r0, q_lo = (d * tk) // tk_main, (d * tk) % tk_main, d * tk
        s = s_ref[e, r0:r0 + tk, q_lo:]
        row = lax.broadcasted_iota(jnp.int32, s.shape, 0)
        col = lax.broadcasted_iota(jnp.int32, s.shape, 1)
        m = update(jnp.where(row <= col, s, -jnp.inf), base + q_lo, tk, m, q_lo)
    acc = acc_ref[...]
    o_ref[...] = (acc[0:MLA_V, :] / acc[MLA_V:MLA_V + 1, :]).T.astype(o_ref.dtype)


def flash_attention(qt, k, vt, batch, seq, tq, tk, tk_main):
    t = k.shape[0]
    tq = _tile(seq, tq)
    tk = _tile(tq, tk)
    tk_main = _tile(tq, tk_main)
    assert tq == 2 * tk_main and tk_main % tk == 0
    nq = seq // tq
    return pl.pallas_call(
        functools.partial(_flash_kernel, tq=tq, tk=tk, tk_main=tk_main),
        grid=(batch, MLA_HEADS, nq),
        in_specs=[pl.BlockSpec((1, MLA_PAD, tq), lambda b, h, i: (h, 0, b * nq + i)),
                  pl.BlockSpec((seq, MLA_PAD), lambda b, h, i: (b, h)),
                  pl.BlockSpec((1, VT_ROWS, seq), lambda b, h, i: (h, 0, b))],
        out_specs=pl.BlockSpec((tq, MLA_V), lambda b, h, i: (b * nq + i, h)),
        out_shape=jax.ShapeDtypeStruct((t, MLA_HEADS * MLA_V), BF16),
        scratch_shapes=[pltpu.VMEM((VT_ROWS, tq), F32), pltpu.VMEM((2, tk_main, tq), F32)],
        compiler_params=_params("parallel", "parallel", "arbitrary"),
        name="mla_flash",
    )(qt, k, vt)


def _ret_kernel(q_ref, k_ref, v_ref, g_ref, o_ref, r_ref):
    c = RET_CHUNK
    hf = pl.program_id(1).astype(F32)

    @pl.when(pl.program_id(2) == 0)
    def _():
        r_ref[...] = jnp.zeros(r_ref.shape, F32)

    def log_gamma(shape):
        return jnp.log(1.0 - jnp.exp2(-5.0 - jnp.full(shape, hf, F32)))

    row = lax.broadcasted_iota(jnp.int32, (c, c), 0)
    col = lax.broadcasted_iota(jnp.int32, (c, c), 1)
    diff = (row - col).astype(F32)
    d_intra = jnp.where(diff >= 0, jnp.exp(jnp.where(diff >= 0, diff, 0.0) * log_gamma((c, c))), 0.0)
    idx = lax.broadcasted_iota(jnp.int32, (c, RET_DK), 0).astype(F32)
    lg_k = log_gamma((c, RET_DK))
    q_dec = jnp.exp((idx + 1.0) * lg_k).astype(BF16)
    k_dec = jnp.exp((c - 1.0 - idx) * lg_k).astype(BF16)
    c_dec = jnp.exp(c * log_gamma((1, RET_DV)))

    def chunk(ci, carry):
        rows = pl.ds(pl.multiple_of(ci * c, c), c)
        q, k, v = q_ref[rows, :], k_ref[rows, :], v_ref[rows, :]
        a = lax.dot_general(q, k, NT, preferred_element_type=F32) * d_intra
        r = r_ref[...]
        out = _dot(a.astype(BF16), v) + _dot(q * q_dec, r.astype(BF16))
        r_ref[...] = r * c_dec + lax.dot_general(k * k_dec, v, TN, preferred_element_type=F32)
        g = g_ref[rows, :].astype(F32)
        o_ref[rows, :] = (_rms(out) * (g * jax.nn.sigmoid(g))).astype(o_ref.dtype)
        return carry

    lax.fori_loop(0, q_ref.shape[0] // c, chunk, 0, unroll=2)


def retention(z, batch, seq, cb):
    t = z.shape[0]
    cb = _tile(seq, cb)
    assert cb % RET_CHUNK == 0
    nblk = seq // cb
    nqk = RET_HEADS * RET_DK // RET_DK
    nv = (2 * RET_HEADS * RET_DK) // RET_DV
    return pl.pallas_call(
        _ret_kernel,
        grid=(batch, RET_HEADS, nblk),
        in_specs=[pl.BlockSpec((cb, RET_DK), lambda b, h, s: (b * nblk + s, h)),
                  pl.BlockSpec((cb, RET_DK), lambda b, h, s: (b * nblk + s, nqk + h)),
                  pl.BlockSpec((cb, RET_DV), lambda b, h, s: (b * nblk + s, nv + h)),
                  pl.BlockSpec((cb, RET_DV), lambda b, h, s: (b * nblk + s, nv + RET_HEADS + h))],
        out_specs=pl.BlockSpec((cb, RET_DV), lambda b, h, s: (b * nblk + s, h)),
        out_shape=jax.ShapeDtypeStruct((t, RET_HEADS * RET_DV), BF16),
        scratch_shapes=[pltpu.VMEM((RET_DK, RET_DV), F32)],
        compiler_params=_params("parallel", "parallel", "arbitrary"),
        name="retention",
    )(z, z, z, z)


def _rope_pad(a, axis):
    x1, x2 = jnp.split(a, 2, axis=axis)
    z = jnp.zeros_like(x1)
    return jnp.concatenate([x1, z, x2, z], axis=axis)


def _pad_qk_gain(g):
    return jnp.concatenate([g[:MLA_NOPE], _rope_pad(g[MLA_NOPE:], 0)])[None, :]


def kernel(x, p, positions, norm_mix, norm_ffn, norm_ple, e_w_in, e_lb_logits, e_q_a_norm, e_kv_a_norm, e_w_uq, e_w_ukv, e_q_norm, e_k_norm, e_hg_onorm, e_w_out, o_w_in, o_w_out, ffn_w_gate, ffn_w_up, ffn_conv_w, ffn_conv_b, ffn_w_down, ple_w_proj, ple_w_gate):
    batch, seq, d = x.shape
    assert norm_mix.shape[0] == 2 and e_w_in.shape[0] == 1 and o_w_in.shape[0] == 1
    t = batch * seq
    h = x.reshape(t, d)
    pos_col = positions.reshape(t, 1)

    f_ret = (ROPE_BASE ** (-jnp.arange(RET_DK // 2, dtype=F32) / (RET_DK // 2)))[None, :]
    n_mla, stride = MLA_ROPE // 2, (RET_DK // 2) // (MLA_ROPE // 2)
    src, x1, x2 = jnp.arange(n_mla) * stride, jnp.arange(n_mla), LANES // 2 + jnp.arange(n_mla)
    sel = jnp.zeros((RET_DK // 2, LANES), F32)
    sel_cos = sel.at[src, x1].set(1.0).at[src, x2].set(1.0)
    sel_sin = sel.at[src, x1].set(-1.0).at[src, x2].set(1.0)
    cos_ret, sin_ret, cos_mla, sin_mla = rope_tables(pos_col, f_ret, sel_cos.astype(BF16),
                                                     sel_sin.astype(BF16))

    w_in = e_w_in[0]
    w_hg = cast_bf16(e_w_in, 0, 4 * HG_W, "cast_w_hgrn")
    off = 4 * HG_W + MLA_Q_RANK + MLA_KV_RANK
    w_mla = jnp.concatenate([w_in[:, 4 * HG_W:off], _rope_pad(w_in[:, off:], 1)], axis=1).astype(BF16)
    wuq = e_w_uq[0].reshape(MLA_Q_RANK, MLA_HEADS, MLA_QK)
    wuq = jnp.concatenate([wuq[..., :MLA_NOPE], _rope_pad(wuq[..., MLA_NOPE:], 2)], axis=-1)
    wuq = wuq.reshape(MLA_Q_RANK, MLA_HEADS * MLA_PAD).astype(BF16)
    wukv = e_w_ukv[0].astype(BF16)
    g_mix0 = norm_mix[0][None, :]

    z_hg = norm_matmul(h, g_mix0, w_hg, F32, ROWS, COLS, "in_proj_hgrn")
    z_mla = norm_matmul(h, g_mix0, w_mla, F32, ROWS, w_mla.shape[1], "in_proj_mla")
    o_a = hgrn(z_hg, e_lb_logits, e_hg_onorm[0][None, :], batch, seq, HGRN_ROWS)
    qt, k, vt = mla_prep(z_mla, e_q_a_norm[0][None, :], e_kv_a_norm[0][None, :], wuq, wukv,
                         _pad_qk_gain(e_q_norm[0]), _pad_qk_gain(e_k_norm[0]), cos_mla, sin_mla, PREP_ROWS)
    o_b = flash_attention(qt, k, vt, batch, seq, FLASH_Q, FLASH_KV_DIAG, FLASH_KV)
    h = res_matmul(h, [o_a, o_b], cast_bf16(e_w_out, 0, name="cast_w_even_out"), ROWS, COLS,
                   "out_proj_even")

    def channel_mix(h, i):
        h = conv_ffn(h, norm_ffn[i][None, :], cast_bf16(ffn_w_gate, i, name="cast_w_gate"),
                     cast_bf16(ffn_w_up, i, name="cast_w_up"), ffn_conv_w[i], ffn_conv_b[i][None, :],
                     cast_bf16(ffn_w_down, i, name="cast_w_down"), seq, ROWS, FFN_COLS)
        return ple(h, norm_ple[i][None, :], cast_bf16(ple_w_gate, i, name="cast_w_ple"),
                   p.reshape(p.shape[0], t, PLE_DIM), i, ple_w_proj[i].astype(BF16), ROWS, COLS)

    h = channel_mix(h, 0)

    z_ret = ret_in_proj(h, norm_mix[1][None, :], cast_bf16(o_w_in, 0, name="cast_w_ret_in"),
                        cos_ret, sin_ret, ROWS, COLS)
    o_r = retention(z_ret, batch, seq, RET_ROWS)
    h = res_matmul(h, [o_r], cast_bf16(o_w_out, 0, name="cast_w_ret_out"), ROWS, COLS // 2, "out_proj_odd")
    h = channel_mix(h, 1)
    return h.reshape(batch, seq, d)
```

```python
import functools

import jax
import jax.numpy as jnp
from jax import lax
from jax.experimental import pallas as pl
from jax.experimental.pallas import tpu as pltpu

F32 = jnp.float32
BF16 = jnp.bfloat16

LANES = 128
SUBLANES = 8
BF16_SUBLANES = 16
VMEM_BYTES = 64 * 2**20

PLE_DIM = 256
HG_HEADS = 8
HG_DK = 128
HG_DV = 128
HG_W = HG_HEADS * HG_DK
HG_CHUNK = 64
MLA_HEADS = 8
MLA_Q_RANK = 512
MLA_KV_RANK = 512
MLA_NOPE = 128
MLA_ROPE = 64
MLA_V = 128
MLA_QK = MLA_NOPE + MLA_ROPE
MLA_PAD = MLA_NOPE + LANES
VT_ROWS = MLA_V + BF16_SUBLANES
LOG2E = 1.4426950408889634
RET_HEADS = 8
RET_DK = 256
RET_DV = 512
RET_CHUNK = 256
CONV_W = 3
FFN_CHAINS = 2
ROPE_BASE = 10000.0
EPS = 1e-6

VMEM_LIMIT = VMEM_BYTES - 4 * 2**20
CAST_BLOCK_BYTES = 8 * 2**20

ROWS = 1024
NORM_ROWS = 256
COLS = 2048
FFN_COLS = 512
PREP_ROWS = 1024
HGRN_ROWS = 1024
RET_ROWS = 4096
FLASH_Q = 2048
FLASH_KV = 1024
FLASH_KV_DIAG = 512

NT = (((1,), (1,)), ((), ()))
TN = (((0,), (0,)), ((), ()))


def _params(*sem):
    return pltpu.CompilerParams(dimension_semantics=sem, vmem_limit_bytes=VMEM_LIMIT)


def _tile(n, pref):
    t = min(n, pref)
    assert n % t == 0, (n, pref)
    return t


def _dot(a, b):
    return jnp.dot(a, b, preferred_element_type=F32)


def _rms(x, g=None):
    y = x * lax.rsqrt(jnp.sum(x * x, axis=-1, keepdims=True) * (1.0 / x.shape[-1]) + EPS)
    return y if g is None else y * g


def _norm_into(x_ref, g_ref, dst_ref, rows=NORM_ROWS):
    g = g_ref[...]

    def body(c, carry):
        r0 = pl.multiple_of(c * rows, rows)
        dst_ref[pl.ds(r0, rows), :] = _rms(x_ref[pl.ds(r0, rows), :], g).astype(dst_ref.dtype)
        return carry

    lax.fori_loop(0, x_ref.shape[0] // rows, body, 0)


def _cast_kernel(w_ref, o_ref):
    o_ref[...] = w_ref[...].astype(o_ref.dtype)


def cast_bf16(w, layer, ncols=None, name="cast_bf16"):
    _, r, c = w.shape
    ncols = c if ncols is None else ncols
    tr = r
    while tr % 16 == 0 and tr * ncols * 4 > CAST_BLOCK_BYTES:
        tr //= 2
    return pl.pallas_call(
        _cast_kernel,
        grid=(r // tr,),
        in_specs=[pl.BlockSpec((None, tr, ncols), lambda i: (layer, i, 0))],
        out_specs=pl.BlockSpec((tr, ncols), lambda i: (i, 0)),
        out_shape=jax.ShapeDtypeStruct((r, ncols), BF16),
        compiler_params=_params("parallel"),
        name=name,
    )(w)


def _select_lanes(x, sel):
    hi = x.astype(BF16)
    r1 = x - hi.astype(F32)
    mid = r1.astype(BF16)
    lo = (r1 - mid.astype(F32)).astype(BF16)
    return _dot(hi, sel) + _dot(mid, sel) + _dot(lo, sel)


def _rope_tab_kernel(pos_ref, f_ref, selc_ref, sels_ref, c_ref, s_ref, cm_ref, sm_ref):
    ang = pos_ref[...].astype(F32) * f_ref[...]
    c, s = jnp.cos(ang), jnp.sin(ang)
    c_ref[...] = c
    s_ref[...] = s
    cm_ref[...] = _select_lanes(c, selc_ref[...])
    sm_ref[...] = _select_lanes(s, sels_ref[...])


def rope_tables(pos_col, freq, sel_cos, sel_sin):
    t = pos_col.shape[0]
    tm = _tile(t, ROWS)
    const = lambda a: pl.BlockSpec(a.shape, lambda i: (0, 0))
    return pl.pallas_call(
        _rope_tab_kernel,
        grid=(t // tm,),
        in_specs=[pl.BlockSpec((tm, 1), lambda i: (i, 0)), const(freq), const(sel_cos), const(sel_sin)],
        out_specs=[pl.BlockSpec((tm, LANES), lambda i: (i, 0))] * 4,
        out_shape=[jax.ShapeDtypeStruct((t, LANES), F32)] * 4,
        compiler_params=_params("parallel"),
        name="rope_tables",
    )(pos_col, freq, sel_cos, sel_sin)


def _norm_matmul_kernel(x_ref, g_ref, w_ref, o_ref, xn_ref):
    j = pl.program_id(1)
    tn = o_ref.shape[1]

    @pl.when(j == 0)
    def _():
        _norm_into(x_ref, g_ref, xn_ref)

    w = w_ref[:, pl.ds(pl.multiple_of(j * tn, tn), tn)]
    o_ref[...] = _dot(xn_ref[...], w).astype(o_ref.dtype)


def norm_matmul(x, g, w, out_dtype, tm, tn, name):
    t, k = x.shape
    n = w.shape[1]
    tm, tn = _tile(t, tm), _tile(n, tn)
    return pl.pallas_call(
        _norm_matmul_kernel,
        grid=(t // tm, n // tn),
        in_specs=[pl.BlockSpec((tm, k), lambda i, j: (i, 0)),
                  pl.BlockSpec((1, k), lambda i, j: (0, 0)),
                  pl.BlockSpec((k, n), lambda i, j: (0, 0), pipeline_mode=pl.Buffered(1))],
        out_specs=pl.BlockSpec((tm, tn), lambda i, j: (i, j)),
        out_shape=jax.ShapeDtypeStruct((t, n), out_dtype),
        scratch_shapes=[pltpu.VMEM((tm, k), BF16)],
        compiler_params=_params("parallel", "arbitrary"),
        name=name,
    )(x, g, w)


def _ret_in_kernel(x_ref, g_ref, w_ref, cos_ref, sin_ref, o_ref, xn_ref, *, n_q_tiles, n_rope_tiles):
    j = pl.program_id(1)

    @pl.when(j == 0)
    def _():
        _norm_into(x_ref, g_ref, xn_ref)

    acc = _dot(xn_ref[...], w_ref[...])
    is_rope = j < n_rope_tiles
    scale = jnp.where(j >= n_q_tiles, RET_DK ** -0.5, 1.0).astype(F32)
    cos = cos_ref[...] * scale
    sin = sin_ref[...] * scale
    half = RET_DK // 2
    for hh in range(acc.shape[1] // RET_DK):
        x1 = acc[:, hh * RET_DK:hh * RET_DK + half]
        x2 = acc[:, hh * RET_DK + half:(hh + 1) * RET_DK]
        o_ref[:, hh * RET_DK:hh * RET_DK + half] = jnp.where(
            is_rope, x1 * cos - x2 * sin, x1).astype(o_ref.dtype)
        o_ref[:, hh * RET_DK + half:(hh + 1) * RET_DK] = jnp.where(
            is_rope, x2 * cos + x1 * sin, x2).astype(o_ref.dtype)


def ret_in_proj(x, g, w, cos, sin, tm, tn):
    t, k = x.shape
    n = w.shape[1]
    tm, tn = _tile(t, tm), _tile(n, tn)
    qw = RET_HEADS * RET_DK
    kern = functools.partial(_ret_in_kernel, n_q_tiles=qw // tn, n_rope_tiles=2 * qw // tn)
    return pl.pallas_call(
        kern,
        grid=(t // tm, n // tn),
        in_specs=[pl.BlockSpec((tm, k), lambda i, j: (i, 0)),
                  pl.BlockSpec((1, k), lambda i, j: (0, 0)),
                  pl.BlockSpec((k, tn), lambda i, j: (0, j)),
                  pl.BlockSpec((tm, RET_DK // 2), lambda i, j: (i, 0)),
                  pl.BlockSpec((tm, RET_DK // 2), lambda i, j: (i, 0))],
        out_specs=pl.BlockSpec((tm, tn), lambda i, j: (i, j)),
        out_shape=jax.ShapeDtypeStruct((t, n), BF16),
        scratch_shapes=[pltpu.VMEM((tm, k), BF16)],
        compiler_params=_params("parallel", "arbitrary"),
        name="ret_in_proj",
    )(x, g, w, cos, sin)


def _res_matmul_kernel(*refs):
    h_ref, w_ref, o_ref = refs[0], refs[-2], refs[-1]
    lhs = [a_ref[...] for a_ref in refs[1:-2]]
    lhs = lhs[0] if len(lhs) == 1 else jnp.concatenate(lhs, axis=1)
    o_ref[...] = h_ref[...] + _dot(lhs, w_ref[...])


def res_matmul(h, lhs, w, tm, tn, name):
    t, n = h.shape
    tm, tn = _tile(t, tm), _tile(n, tn)
    assert sum(a.shape[1] for a in lhs) == w.shape[0]
    in_specs = [pl.BlockSpec((tm, tn), lambda i, j: (i, j))]
    in_specs += [pl.BlockSpec((tm, a.shape[1]), lambda i, j: (i, 0)) for a in lhs]
    in_specs += [pl.BlockSpec((w.shape[0], tn), lambda i, j: (0, j),
                              pipeline_mode=pl.Buffered(1) if tn == n else None)]
    return pl.pallas_call(
        _res_matmul_kernel,
        grid=(t // tm, n // tn),
        in_specs=in_specs,
        out_specs=pl.BlockSpec((tm, tn), lambda i, j: (i, j)),
        out_shape=jax.ShapeDtypeStruct((t, n), F32),
        compiler_params=_params("parallel", "arbitrary"),
        name=name,
    )(h, *lhs, w)


def _ffn_kernel(h_ref, g_ref, wg_ref, wu_ref, cw_ref, cb_ref, wd_ref, o_ref,
                u_ref, gs_ref, tail_ref, *, blocks_per_seq):
    i, j = pl.program_id(0), pl.program_id(1)
    tm = h_ref.shape[0]

    @pl.when(j == 0)
    def _():
        _norm_into(h_ref, g_ref, u_ref)
        o_ref[...] = h_ref[...]

    @pl.when(jnp.logical_and(i == 0, j == 0))
    def _():
        tail_ref[...] = jnp.zeros(tail_ref.shape, F32)

    seq_start = (i % blocks_per_seq) == 0
    gs_ref[0:SUBLANES, :] = jnp.where(seq_start, 0.0, tail_ref[j])
    cw, cb = cw_ref[...], cb_ref[...]
    hm = tm // FFN_CHAINS
    for r in range(0, tm, hm):
        u = u_ref[r:r + hm, :]
        a = _dot(u, wg_ref[...])
        up = _dot(u, wu_ref[...])
        gs_ref[SUBLANES + r:SUBLANES + r + hm, :] = a
        c = cb + cw[CONV_W - 1:CONV_W, :] * a
        for tap in range(CONV_W - 1):
            r0 = SUBLANES + r - (CONV_W - 1 - tap)
            c = c + cw[tap:tap + 1, :] * gs_ref[r0:r0 + hm, :]
        act = (c * jax.nn.sigmoid(c) * up).astype(BF16)
        o_ref[r:r + hm, :] += _dot(act, wd_ref[...])
    tail_ref[j] = gs_ref[tm:tm + SUBLANES, :]


def conv_ffn(h, g, wg, wu, cw, cb, wd, seq, tm, tf):
    t, d = h.shape
    f = wg.shape[1]
    tm, tf = _tile(seq, tm), _tile(f, tf)
    kern = functools.partial(_ffn_kernel, blocks_per_seq=seq // tm)
    return pl.pallas_call(
        kern,
        grid=(t // tm, f // tf),
        in_specs=[pl.BlockSpec((tm, d), lambda i, j: (i, 0)),
                  pl.BlockSpec((1, d), lambda i, j: (0, 0)),
                  pl.BlockSpec((d, tf), lambda i, j: (0, j)),
                  pl.BlockSpec((d, tf), lambda i, j: (0, j)),
                  pl.BlockSpec((CONV_W, tf), lambda i, j: (0, j)),
                  pl.BlockSpec((1, tf), lambda i, j: (0, j)),
                  pl.BlockSpec((tf, d), lambda i, j: (j, 0))],
        out_specs=pl.BlockSpec((tm, d), lambda i, j: (i, 0)),
        out_shape=jax.ShapeDtypeStruct((t, d), F32),
        scratch_shapes=[pltpu.VMEM((tm, d), BF16),
                        pltpu.VMEM((tm + SUBLANES, tf), F32),
                        pltpu.VMEM((f // tf, SUBLANES, tf), F32)],
        compiler_params=_params("arbitrary", "arbitrary"),
        name="conv_ffn",
    )(h, g, wg, wu, cw, cb, wd)


def _ple_kernel(h_ref, g_ref, wgate_ref, p_ref, wproj_ref, o_ref, xn_ref, pb_ref):
    j = pl.program_id(1)
    tn = o_ref.shape[1]

    @pl.when(j == 0)
    def _():
        _norm_into(h_ref, g_ref, xn_ref)
        pb_ref[...] = p_ref[...].astype(BF16)

    gate = jax.nn.sigmoid(_dot(xn_ref[...], wgate_ref[...]))
    proj = _dot(pb_ref[...], wproj_ref[...])
    c0 = pl.multiple_of(j * tn, tn)
    o_ref[...] = h_ref[:, pl.ds(c0, tn)] + proj * gate


def ple(h, g, wgate, p, layer, wproj, tm, tn):
    t, d = h.shape
    tm, tn = _tile(t, tm), _tile(d, tn)
    return pl.pallas_call(
        _ple_kernel,
        grid=(t // tm, d // tn),
        in_specs=[pl.BlockSpec((tm, d), lambda i, j: (i, 0)),
                  pl.BlockSpec((1, d), lambda i, j: (0, 0)),
                  pl.BlockSpec((d, tn), lambda i, j: (0, j)),
                  pl.BlockSpec((None, tm, PLE_DIM), lambda i, j: (layer, i, 0)),
                  pl.BlockSpec((PLE_DIM, tn), lambda i, j: (0, j))],
        out_specs=pl.BlockSpec((tm, tn), lambda i, j: (i, j)),
        out_shape=jax.ShapeDtypeStruct((t, d), F32),
        scratch_shapes=[pltpu.VMEM((tm, d), BF16), pltpu.VMEM((tm, PLE_DIM), BF16)],
        compiler_params=_params("parallel", "arbitrary"),
        name="ple",
    )(h, g, wgate, p, wproj)


def _hgrn_kernel(q_ref, f_ref, i_ref, g_ref, lbl_ref, on_ref, o_ref, st_ref):
    c = HG_CHUNK

    @pl.when(pl.program_id(1) == 0)
    def _():
        st_ref[...] = jnp.zeros(st_ref.shape, F32)

    lg = lbl_ref[...]
    e = jnp.exp(lg - jnp.max(lg, axis=0, keepdims=True))
    lb = e[0:1, :] / jnp.sum(e, axis=0, keepdims=True)
    onorm = on_ref[...]
    pw = 2 * HG_DK
    row = lax.broadcasted_iota(jnp.int32, (c, c), 0)
    col = lax.broadcasted_iota(jnp.int32, (c, c), 1)
    tril = jnp.where(row >= col, 1.0, 0.0).astype(BF16)
    tril3 = jnp.concatenate([tril, tril, tril], axis=1)
    row2 = lax.broadcasted_iota(jnp.int32, (c, 2 * c), 0)
    col2 = lax.broadcasted_iota(jnp.int32, (c, 2 * c), 1)
    causal2 = row2 >= jnp.where(col2 >= c, col2 - c, col2)
    left = lax.broadcasted_iota(jnp.int32, (c, pw), 1) < HG_DK
    zero_state = jnp.zeros((HG_DV, HG_DK), BF16)

    def block_diag(x):
        zero = jnp.zeros_like(x)
        return jnp.concatenate([jnp.where(left, x, zero), jnp.where(left, zero, x)], axis=0)

    def chunk(ci, carry):
        rows = pl.ds(pl.multiple_of(ci * c, c), c)
        q = q_ref[rows, :]
        v = i_ref[rows, :].astype(BF16)
        f = lb + (1.0 - lb) * jax.nn.sigmoid(f_ref[rows, :])
        k = 1.0 - f
        lf = jnp.log(f)
        hi = lf.astype(BF16)
        r1 = lf - hi.astype(F32)
        mid = r1.astype(BF16)
        lo = (r1 - mid.astype(F32)).astype(BF16)
        bc = _dot(tril3, jnp.concatenate([hi, mid, lo], axis=0))
        ref = bc[c // 2 - 1:c // 2, :]
        b_last = bc[c - 1:c, :]
        q_rel = (q * jnp.exp(bc - ref)).astype(BF16)
        k_rel = (k * jnp.exp(ref - bc)).astype(BF16)
        q_dec = (q * jnp.exp(bc)).astype(BF16)
        k_dec = (k * jnp.exp(b_last - bc)).astype(BF16)
        decay = jnp.exp(b_last)
        for p in range(HG_HEADS // 2):
            cols = slice(p * pw, (p + 1) * pw)
            a = lax.dot_general(q_rel[:, cols], block_diag(k_rel[:, cols]), NT, preferred_element_type=F32)
            a = jnp.where(causal2, a, 0.0).astype(BF16)
            st0, st1 = st_ref[2 * p], st_ref[2 * p + 1]
            st_pair = jnp.concatenate(
                [jnp.concatenate([st0.astype(BF16), zero_state], axis=1),
                 jnp.concatenate([zero_state, st1.astype(BF16)], axis=1)], axis=0)
            out = _dot(a, block_diag(v[:, cols])) + lax.dot_general(q_dec[:, cols], st_pair, NT,
                                                                    preferred_element_type=F32)
            upd = lax.dot_general(v[:, cols], k_dec[:, cols], TN, preferred_element_type=F32)
            st_ref[2 * p] = st0 * decay[:, p * pw:p * pw + HG_DK] + upd[:HG_DV, :HG_DK]
            st_ref[2 * p + 1] = st1 * decay[:, p * pw + HG_DK:(p + 1) * pw] + upd[HG_DV:, HG_DK:]
            for hh in range(2):
                hc = slice(p * pw + hh * HG_DV, p * pw + (hh + 1) * HG_DV)
                g = g_ref[rows, hc]
                o_ref[rows, hc] = (_rms(out[:, hh * HG_DV:(hh + 1) * HG_DV], onorm)
                                   * (g * jax.nn.sigmoid(g))).astype(o_ref.dtype)
        return carry

    lax.fori_loop(0, q_ref.shape[0] // c, chunk, 0, unroll=4)


def hgrn(z, lb_logits, onorm, batch, seq, cb):
    t = z.shape[0]
    cb = _tile(seq, cb)
    nblk = seq // cb
    spec = lambda part: pl.BlockSpec((cb, HG_W), lambda b, s, part=part: (b * nblk + s, part))
    return pl.pallas_call(
        _hgrn_kernel,
        grid=(batch, nblk),
        in_specs=[spec(0), spec(1), spec(2), spec(3),
                  pl.BlockSpec(lb_logits.shape, lambda b, s: (0, 0)),
                  pl.BlockSpec((1, HG_DV), lambda b, s: (0, 0))],
        out_specs=pl.BlockSpec((cb, HG_W), lambda b, s: (b * nblk + s, 0)),
        out_shape=jax.ShapeDtypeStruct((t, HG_W), BF16),
        scratch_shapes=[pltpu.VMEM((HG_HEADS, HG_DV, HG_DK), F32)],
        compiler_params=_params("parallel", "arbitrary"),
        name="hgrn2",
    )(z, z, z, z, lb_logits, onorm)


def _mla_prep_kernel(z_ref, qa_ref, kva_ref, wuq_ref, wukv_ref, qn_ref, kn_ref, cos_ref, sin_ref,
                     qt_out, k_out, vt_out):
    cq = _rms(z_ref[:, 0:MLA_Q_RANK], qa_ref[...]).astype(BF16)
    ckv = _rms(z_ref[:, MLA_Q_RANK:MLA_Q_RANK + MLA_KV_RANK], kva_ref[...]).astype(BF16)
    kpe = z_ref[:, MLA_Q_RANK + MLA_KV_RANK:]
    qf = _dot(cq, wuq_ref[...])
    kvf = _dot(ckv, wukv_ref[...])
    cos, sin = cos_ref[...], sin_ref[...]
    qn, kn = qn_ref[...], kn_ref[...]
    scale = MLA_QK ** -0.5 * LOG2E
    tm = z_ref.shape[0]

    def rope(x):
        return x * cos + pltpu.roll(x, LANES // 2, 1) * sin

    kpe_ss = jnp.sum(kpe * kpe, axis=-1, keepdims=True)
    k_rope = rope(kpe * kn[:, MLA_NOPE:])
    ones_rows = jnp.where(lax.broadcasted_iota(jnp.int32, (VT_ROWS - MLA_V, tm), 0) == 0, 1.0, 0.0)
    for h in range(MLA_HEADS):
        c0 = h * MLA_PAD
        qh = qf[:, c0:c0 + MLA_PAD]
        rq = lax.rsqrt(jnp.sum(qh * qh, axis=-1, keepdims=True) * (1.0 / MLA_QK) + EPS) * scale
        qh = qh * rq * qn
        qt_out[h, 0:MLA_NOPE, :] = qh[:, :MLA_NOPE].T.astype(BF16)
        qt_out[h, MLA_NOPE:MLA_PAD, :] = rope(qh[:, MLA_NOPE:]).T.astype(BF16)
        kh = kvf[:, c0:c0 + MLA_NOPE]
        rk = lax.rsqrt((jnp.sum(kh * kh, axis=-1, keepdims=True) + kpe_ss) * (1.0 / MLA_QK) + EPS)
        k_out[:, c0:c0 + MLA_NOPE] = (kh * rk * kn[:, :MLA_NOPE]).astype(BF16)
        k_out[:, c0 + MLA_NOPE:c0 + MLA_PAD] = (k_rope * rk).astype(BF16)
        vt_out[h, 0:MLA_V, :] = kvf[:, c0 + MLA_NOPE:c0 + MLA_PAD].T.astype(BF16)
        vt_out[h, MLA_V:VT_ROWS, :] = ones_rows.astype(BF16)


def mla_prep(z, qa, kva, wuq, wukv, qn, kn, cos, sin, tm):
    t, zw = z.shape
    tm = _tile(t, tm)
    full = lambda a: pl.BlockSpec(a.shape, lambda i: (0, 0))
    rows = lambda w: pl.BlockSpec((tm, w), lambda i: (i, 0))
    cols = lambda r: pl.BlockSpec((MLA_HEADS, r, tm), lambda i: (0, 0, i))
    return pl.pallas_call(
        _mla_prep_kernel,
        grid=(t // tm,),
        in_specs=[rows(zw), full(qa), full(kva), full(wuq), full(wukv), full(qn), full(kn),
                  rows(LANES), rows(LANES)],
        out_specs=[cols(MLA_PAD), rows(MLA_HEADS * MLA_PAD), cols(VT_ROWS)],
        out_shape=[jax.ShapeDtypeStruct((MLA_HEADS, MLA_PAD, t), BF16),
                   jax.ShapeDtypeStruct((t, MLA_HEADS * MLA_PAD), BF16),
                   jax.ShapeDtypeStruct((MLA_HEADS, VT_ROWS, t), BF16)],
        compiler_params=_params("parallel"),
        name="mla_prep",
    )(z, qa, kva, wuq, wukv, qn, kn, cos, sin)


def _flash_kernel(qt_ref, k_ref, vt_ref, o_ref, acc_ref, s_ref, *, tq, tk, tk_main):
    qi = pl.program_id(2)
    qt = qt_ref[0]
    acc_ref[...] = jnp.zeros(acc_ref.shape, F32)

    def scores(k0, kw, q_lo):
        return _dot(k_ref[pl.ds(k0, kw), :], qt[:, q_lo:])

    def update(s, k0, kw, m_all, q_lo):
        m_prev = m_all[:, q_lo:]
        m_new = jnp.maximum(m_prev, jnp.max(s, axis=0, keepdims=True))
        alpha = jnp.exp2(m_prev - m_new)
        p = jnp.exp2(s - m_new).astype(BF16)
        acc_ref[:, q_lo:] = alpha * acc_ref[:, q_lo:] + _dot(vt_ref[0, :, pl.ds(k0, kw)], p)
        return m_new if q_lo == 0 else jnp.concatenate([m_all[:, :q_lo], m_new], axis=1)

    n_pairs = qi
    s_ref[0] = scores(0, tk_main, 0)

    def pair(t, m):
        k0 = pl.multiple_of(t * (2 * tk_main), 2 * tk_main)
        s_ref[1] = scores(k0 + tk_main, tk_main, 0)
        m = update(s_ref[0], k0, tk_main, m, 0)
        s_ref[0] = scores(k0 + 2 * tk_main, tk_main, 0)
        return update(s_ref[1], k0 + tk_main, tk_main, m, 0)

    m = lax.fori_loop(0, n_pairs, pair, jnp.full((1, tq), -jnp.inf, F32))
    base = pl.multiple_of(qi * tq, tq)
    s_ref[1, :, tk_main:] = scores(base + tk_main, tk_main, tk_main)
    for d in range(tq // tk):
        e, r0, q_lo = (d * tk) // tk_main, (d * tk) % tk_main, d * tk
        s = s_ref[e, r0:r0 + tk, q_lo:]
        row = lax.broadcasted_iota(jnp.int32, s.shape, 0)
        col = lax.broadcasted_iota(jnp.int32, s.shape, 1)
        m = update(jnp.where(row <= col, s, -jnp.inf), base + q_lo, tk, m, q_lo)
    acc = acc_ref[...]
    o_ref[...] = (acc[0:MLA_V, :] / acc[MLA_V:MLA_V + 1, :]).T.astype(o_ref.dtype)


def flash_attention(qt, k, vt, batch, seq, tq, tk, tk_main):
    t = k.shape[0]
    tq = _tile(seq, tq)
    tk = _tile(tq, tk)
    tk_main = _tile(tq, tk_main)
    assert tq == 2 * tk_main and tk_main % tk == 0
    nq = seq // tq
    return pl.pallas_call(
        functools.partial(_flash_kernel, tq=tq, tk=tk, tk_main=tk_main),
        grid=(batch, MLA_HEADS, nq),
        in_specs=[pl.BlockSpec((1, MLA_PAD, tq), lambda b, h, i: (h, 0, b * nq + i)),
                  pl.BlockSpec((seq, MLA_PAD), lambda b, h, i: (b, h)),
                  pl.BlockSpec((1, VT_ROWS, seq), lambda b, h, i: (h, 0, b))],
        out_specs=pl.BlockSpec((tq, MLA_V), lambda b, h, i: (b * nq + i, h)),
        out_shape=jax.ShapeDtypeStruct((t, MLA_HEADS * MLA_V), BF16),
        scratch_shapes=[pltpu.VMEM((VT_ROWS, tq), F32), pltpu.VMEM((2, tk_main, tq), F32)],
        compiler_params=_params("parallel", "parallel", "arbitrary"),
        name="mla_flash",
    )(qt, k, vt)


def _ret_kernel(q_ref, k_ref, v_ref, g_ref, o_ref, r_ref):
    c = RET_CHUNK
    hf = pl.program_id(1).astype(F32)

    @pl.when(pl.program_id(2) == 0)
    def _():
        r_ref[...] = jnp.zeros(r_ref.shape, F32)

    def log_gamma(shape):
        return jnp.log(1.0 - jnp.exp2(-5.0 - jnp.full(shape, hf, F32)))

    row = lax.broadcasted_iota(jnp.int32, (c, c), 0)
    col = lax.broadcasted_iota(jnp.int32, (c, c), 1)
    diff = (row - col).astype(F32)
    d_intra = jnp.where(diff >= 0, jnp.exp(jnp.where(diff >= 0, diff, 0.0) * log_gamma((c, c))), 0.0)
    idx = lax.broadcasted_iota(jnp.int32, (c, RET_DK), 0).astype(F32)
    lg_k = log_gamma((c, RET_DK))
    q_dec = jnp.exp((idx + 1.0) * lg_k).astype(BF16)
    k_dec = jnp.exp((c - 1.0 - idx) * lg_k).astype(BF16)
    c_dec = jnp.exp(c * log_gamma((1, RET_DV)))

    def chunk(ci, carry):
        rows = pl.ds(pl.multiple_of(ci * c, c), c)
        q, k, v = q_ref[rows, :], k_ref[rows, :], v_ref[rows, :]
        a = lax.dot_general(q, k, NT, preferred_element_type=F32) * d_intra
        r = r_ref[...]
        out = _dot(a.astype(BF16), v) + _dot(q * q_dec, r.astype(BF16))
        r_ref[...] = r * c_dec + lax.dot_general(k * k_dec, v, TN, preferred_element_type=F32)
        g = g_ref[rows, :].astype(F32)
        o_ref[rows, :] = (_rms(out) * (g * jax.nn.sigmoid(g))).astype(o_ref.dtype)
        return carry

    lax.fori_loop(0, q_ref.shape[0] // c, chunk, 0, unroll=2)


def retention(z, batch, seq, cb):
    t = z.shape[0]
    cb = _tile(seq, cb)
    assert cb % RET_CHUNK == 0
    nblk = seq // cb
    nqk = RET_HEADS * RET_DK // RET_DK
    nv = (2 * RET_HEADS * RET_DK) // RET_DV
    return pl.pallas_call(
        _ret_kernel,
        grid=(batch, RET_HEADS, nblk),
        in_specs=[pl.BlockSpec((cb, RET_DK), lambda b, h, s: (b * nblk + s, h)),
                  pl.BlockSpec((cb, RET_DK), lambda b, h, s: (b * nblk + s, nqk + h)),
                  pl.BlockSpec((cb, RET_DV), lambda b, h, s: (b * nblk + s, nv + h)),
                  pl.BlockSpec((cb, RET_DV), lambda b, h, s: (b * nblk + s, nv + RET_HEADS + h))],
        out_specs=pl.BlockSpec((cb, RET_DV), lambda b, h, s: (b * nblk + s, h)),
        out_shape=jax.ShapeDtypeStruct((t, RET_HEADS * RET_DV), BF16),
        scratch_shapes=[pltpu.VMEM((RET_DK, RET_DV), F32)],
        compiler_params=_params("parallel", "parallel", "arbitrary"),
        name="retention",
    )(z, z, z, z)


def _rope_pad(a, axis):
    x1, x2 = jnp.split(a, 2, axis=axis)
    z = jnp.zeros_like(x1)
    return jnp.concatenate([x1, z, x2, z], axis=axis)


def _pad_qk_gain(g):
    return jnp.concatenate([g[:MLA_NOPE], _rope_pad(g[MLA_NOPE:], 0)])[None, :]


def kernel(x, p, positions, norm_mix, norm_ffn, norm_ple, e_w_in, e_lb_logits, e_q_a_norm, e_kv_a_norm, e_w_uq, e_w_ukv, e_q_norm, e_k_norm, e_hg_onorm, e_w_out, o_w_in, o_w_out, ffn_w_gate, ffn_w_up, ffn_conv_w, ffn_conv_b, ffn_w_down, ple_w_proj, ple_w_gate):
    batch, seq, d = x.shape
    assert norm_mix.shape[0] == 2 and e_w_in.shape[0] == 1 and o_w_in.shape[0] == 1
    t = batch * seq
    h = x.reshape(t, d)
    pos_col = positions.reshape(t, 1)

    f_ret = (ROPE_BASE ** (-jnp.arange(RET_DK // 2, dtype=F32) / (RET_DK // 2)))[None, :]
    n_mla, stride = MLA_ROPE // 2, (RET_DK // 2) // (MLA_ROPE // 2)
    src, x1, x2 = jnp.arange(n_mla) * stride, jnp.arange(n_mla), LANES // 2 + jnp.arange(n_mla)
    sel = jnp.zeros((RET_DK // 2, LANES), F32)
    sel_cos = sel.at[src, x1].set(1.0).at[src, x2].set(1.0)
    sel_sin = sel.at[src, x1].set(-1.0).at[src, x2].set(1.0)
    cos_ret, sin_ret, cos_mla, sin_mla = rope_tables(pos_col, f_ret, sel_cos.astype(BF16),
                                                     sel_sin.astype(BF16))

    w_in = e_w_in[0]
    w_hg = cast_bf16(e_w_in, 0, 4 * HG_W, "cast_w_hgrn")
    off = 4 * HG_W + MLA_Q_RANK + MLA_KV_RANK
    w_mla = jnp.concatenate([w_in[:, 4 * HG_W:off], _rope_pad(w_in[:, off:], 1)], axis=1).astype(BF16)
    wuq = e_w_uq[0].reshape(MLA_Q_RANK, MLA_HEADS, MLA_QK)
    wuq = jnp.concatenate([wuq[..., :MLA_NOPE], _rope_pad(wuq[..., MLA_NOPE:], 2)], axis=-1)
    wuq = wuq.reshape(MLA_Q_RANK, MLA_HEADS * MLA_PAD).astype(BF16)
    wukv = e_w_ukv[0].astype(BF16)
    g_mix0 = norm_mix[0][None, :]

    z_hg = norm_matmul(h, g_mix0, w_hg, F32, ROWS, COLS, "in_proj_hgrn")
    z_mla = norm_matmul(h, g_mix0, w_mla, F32, ROWS // 2, w_mla.shape[1], "in_proj_mla")
    o_a = hgrn(z_hg, e_lb_logits, e_hg_onorm[0][None, :], batch, seq, HGRN_ROWS)
    qt, k, vt = mla_prep(z_mla, e_q_a_norm[0][None, :], e_kv_a_norm[0][None, :], wuq, wukv,
                         _pad_qk_gain(e_q_norm[0]), _pad_qk_gain(e_k_norm[0]), cos_mla, sin_mla, PREP_ROWS)
    o_b = flash_attention(qt, k, vt, batch, seq, FLASH_Q, FLASH_KV_DIAG, FLASH_KV)
    h = res_matmul(h, [o_a, o_b], cast_bf16(e_w_out, 0, name="cast_w_even_out"), ROWS, COLS,
                   "out_proj_even")

    def channel_mix(h, i):
        h = conv_ffn(h, norm_ffn[i][None, :], cast_bf16(ffn_w_gate, i, name="cast_w_gate"),
                     cast_bf16(ffn_w_up, i, name="cast_w_up"), ffn_conv_w[i], ffn_conv_b[i][None, :],
                     cast_bf16(ffn_w_down, i, name="cast_w_down"), seq, ROWS, FFN_COLS)
        return ple(h, norm_ple[i][None, :], cast_bf16(ple_w_gate, i, name="cast_w_ple"),
                   p.reshape(p.shape[0], t, PLE_DIM), i, ple_w_proj[i].astype(BF16), ROWS, COLS)

    h = channel_mix(h, 0)

    z_ret = ret_in_proj(h, norm_mix[1][None, :], cast_bf16(o_w_in, 0, name="cast_w_ret_in"),
                        cos_ret, sin_ret, ROWS, COLS)
    o_r = retention(z_ret, batch, seq, RET_ROWS)
    h = res_matmul(h, [o_r], cast_bf16(o_w_out, 0, name="cast_w_ret_out"), ROWS, COLS // 2, "out_proj_odd")
    h = channel_mix(h, 1)
    return h.reshape(batch, seq, d)
```
